```python
import math
import jax, jax.numpy as jnp
from jax import lax
import numpy as np

D_MODEL = 2048
BATCH = 4
SEQ = 8192
DEPTH = 4

GRID_W = 64
CTX_LEN = 256
NORM_EPS = 1e-6
N_MOD = 6
HEAD_DIM = 128
N_Q_HEADS = D_MODEL // 256
N_KV_HEADS = 2
GQA_GROUP = N_Q_HEADS // N_KV_HEADS
ATTN_WIDTH = N_Q_HEADS * HEAD_DIM
KV_WIDTH = N_KV_HEADS * HEAD_DIM
ATTN_SCALE = HEAD_DIM ** -0.5
Q_BLOCK = 128
ROPE_AXIS_DIM = HEAD_DIM // 2
ROPE_BASE = 10000.0
CONV_WIDTH = D_MODEL // 2
CONV_K = 3
SSM_GROUP = 16
SSM_WIDTH = 3 * D_MODEL // 8
SSM_GROUPS = SSM_WIDTH // SSM_GROUP
SSM_STATE = 64
SSM_RE_MAX = -1e-4
N_BRANCH = 3
D_FF = 4 * D_MODEL
PROJ_WIDTHS = (CONV_WIDTH, CONV_WIDTH, CONV_WIDTH, SSM_WIDTH, ATTN_WIDTH, KV_WIDTH, KV_WIDTH, N_BRANCH * D_MODEL)
IN_WIDTH = sum(PROJ_WIDTHS)

kernel_name = "hybrid_conv_s5_gqa_parallel_diffusion_block"


def rms_norm(x, g):
    xf = x.astype(jnp.float32)
    y = xf * lax.rsqrt(jnp.mean(xf * xf, axis=-1, keepdims=True) + NORM_EPS)
    return (y * g.astype(jnp.float32)).astype(x.dtype)


def modulate(h, shift, scale):
    return h * (1 + scale) + shift


def split_proj(p):
    offs = np.cumsum(PROJ_WIDTHS)[:-1].tolist()
    return jnp.split(p, offs, axis=-1)


def axial_rope_tables(n, dtype):
    rows = n // GRID_W
    row = jnp.repeat(jnp.arange(rows), GRID_W)
    col = jnp.tile(jnp.arange(GRID_W), rows)
    half = ROPE_AXIS_DIM // 2
    inv_freq = ROPE_BASE ** (-jnp.arange(half, dtype=jnp.float32) / half)
    ang_r = row.astype(jnp.float32)[:, None] * inv_freq
    ang_c = col.astype(jnp.float32)[:, None] * inv_freq
    return (jnp.cos(ang_r).astype(dtype), jnp.sin(ang_r).astype(dtype),
            jnp.cos(ang_c).astype(dtype), jnp.sin(ang_c).astype(dtype))


def rotate_half_rope(x, cos, sin):
    x1, x2 = jnp.split(x, 2, axis=-1)
    cos = cos[None, :, None, :]
    sin = sin[None, :, None, :]
    return jnp.concatenate([x1 * cos - x2 * sin, x2 * cos + x1 * sin], axis=-1)


def apply_axial_rope(x, tables):
    cos_r, sin_r, cos_c, sin_c = tables
    x_row, x_col = jnp.split(x, 2, axis=-1)
    return jnp.concatenate([rotate_half_rope(x_row, cos_r, sin_r), rotate_half_rope(x_col, cos_c, sin_c)], axis=-1)


def gqa_attend(q, k, v):
    b, lq = q.shape[:2]
    qg = q.reshape(b, lq, N_KV_HEADS, GQA_GROUP, HEAD_DIM)
    s = jnp.einsum('bqhgd,bkhd->bhgqk', qg, k).astype(jnp.float32) * ATTN_SCALE
    p = jax.nn.softmax(s, axis=-1).astype(v.dtype)
    o = jnp.einsum('bhgqk,bkhd->bqhgd', p, v)
    return o.reshape(b, lq, ATTN_WIDTH)


def blocked_attention(q, k, v):
    b, n = q.shape[:2]
    nblk = n // Q_BLOCK
    qb = jnp.moveaxis(q.reshape(b, nblk, Q_BLOCK, N_Q_HEADS, HEAD_DIM), 1, 0)
    o = lax.map(lambda qq: gqa_attend(qq, k, v), qb)
    return jnp.moveaxis(o, 0, 1).reshape(b, n, ATTN_WIDTH)


def dwconv3_centred(u, w):
    n = u.shape[1]
    up = jnp.pad(u, ((0, 0), (1, 1), (0, 0)))
    return up[:, :n] * w[0] + up[:, 1:n + 1] * w[1] + up[:, 2:] * w[2]


def short_conv_branch(gate_b, gate_c, v, conv_w, w_conv_out):
    return (gate_b * dwconv3_centred(gate_c * v, conv_w)) @ w_conv_out


def ssm_discretise(lam_re, lam_im, log_dt, b_re, b_im):
    lam_re = jnp.minimum(lam_re.astype(jnp.float32), SSM_RE_MAX)
    lam_im = lam_im.astype(jnp.float32)
    dt = jnp.exp(log_dt.astype(jnp.float32))[:, None]
    mag = jnp.exp(lam_re * dt)
    ab_re = mag * jnp.cos(lam_im * dt)
    ab_im = mag * jnp.sin(lam_im * dt)
    nr = ab_re - 1
    den = lam_re * lam_re + lam_im * lam_im
    f_re = (nr * lam_re + ab_im * lam_im) / den
    f_im = (ab_im * lam_re - nr * lam_im) / den
    b_re = b_re.astype(jnp.float32)
    b_im = b_im.astype(jnp.float32)
    bb_re = f_re[..., None] * b_re - f_im[..., None] * b_im
    bb_im = f_re[..., None] * b_im + f_im[..., None] * b_re
    return ab_re, ab_im, bb_re, bb_im


def complex_linear_combine(e1, e2):
    a1r, a1i, b1r, b1i = e1
    a2r, a2i, b2r, b2i = e2
    return (a2r * a1r - a2i * a1i, a2r * a1i + a2i * a1r,
            a2r * b1r - a2i * b1i + b2r, a2r * b1i + a2i * b1r + b2i)


def ssm_states(u, disc, h0):
    ab_re, ab_im, bb_re, bb_im = disc
    bu_re = jnp.einsum('blgp,gnp->blgn', u, bb_re)
    bu_im = jnp.einsum('blgp,gnp->blgn', u, bb_im)
    if h0 is not None:
        h_re, h_im = h0
        bu_re = bu_re.at[:, 0].add(ab_re * h_re - ab_im * h_im)
        bu_im = bu_im.at[:, 0].add(ab_re * h_im + ab_im * h_re)
    l = u.shape[1]
    a_re = jnp.broadcast_to(ab_re, (1, l) + ab_re.shape)
    a_im = jnp.broadcast_to(ab_im, (1, l) + ab_im.shape)
    _, _, h_re, h_im = lax.associative_scan(complex_linear_combine, (a_re, a_im, bu_re, bu_im), axis=1)
    return h_re, h_im


def ssm_readout(h_re, h_im, c_re, c_im):
    return jnp.einsum('blgn,gpn->blgp', h_re, c_re) - jnp.einsum('blgn,gpn->blgp', h_im, c_im)


def glu_out(y, w_glu):
    a, g = jnp.split(jax.nn.gelu(y) @ w_glu, 2, axis=-1)
    return a * jax.nn.sigmoid(g)


def ssm_branch(u_x, u_c, lam_re, lam_im, log_dt, b_re, b_im, c_re, c_im, d_skip, w_glu, need_ctx):
    dtype = u_x.dtype
    bx, n, _ = u_x.shape
    bc, m, _ = u_c.shape
    ux = u_x.astype(jnp.float32).reshape(bx, n, SSM_GROUPS, SSM_GROUP)
    uc = u_c.astype(jnp.float32).reshape(bc, m, SSM_GROUPS, SSM_GROUP)
    dsk = d_skip.astype(jnp.float32).reshape(SSM_GROUPS, SSM_GROUP)
    y_x = dsk * ux
    y_c = dsk * uc if need_ctx else None
    for d in range(2):
        disc = ssm_discretise(lam_re[d], lam_im[d], log_dt[d], b_re[d], b_im[d])
        cr = c_re[d].astype(jnp.float32)
        ci = c_im[d].astype(jnp.float32)
        sx = ux if d == 0 else jnp.flip(ux, axis=1)
        sc = uc if d == 0 else jnp.flip(uc, axis=1)
        hc_re, hc_im = ssm_states(sc, disc, None)
        hx_re, hx_im = ssm_states(sx, disc, (hc_re[:, -1], hc_im[:, -1]))
        yx = ssm_readout(hx_re, hx_im, cr, ci)
        y_x = y_x + (yx if d == 0 else jnp.flip(yx, axis=1))
        if need_ctx:
            yc = ssm_readout(hc_re, hc_im, cr, ci)
            y_c = y_c + (yc if d == 0 else jnp.flip(yc, axis=1))
    out_x = glu_out(y_x.reshape(bx, n, SSM_WIDTH).astype(dtype), w_glu)
    out_c = glu_out(y_c.reshape(bc, m, SSM_WIDTH).astype(dtype), w_glu) if need_ctx else None
    return out_x, out_c


def gated_merge(y_conv, y_ssm, y_attn, gate_logits, w_out):
    g = jax.nn.sigmoid(gate_logits.astype(jnp.float32)).astype(y_conv.dtype)
    g_conv, g_ssm, g_attn = jnp.split(g, N_BRANCH, axis=-1)
    return (g_conv * y_conv + g_ssm * y_ssm + g_attn * y_attn) @ w_out


def hybrid_mixer(hx, hc, w_in, conv_w, w_conv_out, lam_re, lam_im, log_dt, b_re, b_im, c_re, c_im,
                 d_skip, w_glu, q_gain, k_gain, w_attn_out, w_out, rope, need_ctx):
    bx, n, _ = hx.shape
    bc, m, _ = hc.shape
    xa_b, xa_c, xa_v, xs_u, xq, xk, xv, xg = split_proj(hx @ w_in)
    ca_b, ca_c, ca_v, cs_u, cq, ck, cv, cg = split_proj(hc @ w_in)
    ya_x = short_conv_branch(xa_b, xa_c, xa_v, conv_w, w_conv_out)
    ys_x, ys_c = ssm_branch(xs_u, cs_u, lam_re, lam_im, log_dt, b_re, b_im, c_re, c_im, d_skip, w_glu, need_ctx)
    q_x = apply_axial_rope(rms_norm(xq.reshape(bx, n, N_Q_HEADS, HEAD_DIM), q_gain), rope)
    k_x = apply_axial_rope(rms_norm(xk.reshape(bx, n, N_KV_HEADS, HEAD_DIM), k_gain), rope)
    v_x = xv.reshape(bx, n, N_KV_HEADS, HEAD_DIM)
    k_c = rms_norm(ck.reshape(bc, m, N_KV_HEADS, HEAD_DIM), k_gain)
    v_c = cv.reshape(bc, m, N_KV_HEADS, HEAD_DIM)
    k_all = jnp.concatenate([k_c, k_x], axis=1)
    v_all = jnp.concatenate([v_c, v_x], axis=1)
    yc_x = blocked_attention(q_x, k_all, v_all) @ w_attn_out
    out_x = gated_merge(ya_x, ys_x, yc_x, xg, w_out)
    if not need_ctx:
        return out_x, None
    ya_c = short_conv_branch(ca_b, ca_c, ca_v, conv_w, w_conv_out)
    q_c = rms_norm(cq.reshape(bc, m, N_Q_HEADS, HEAD_DIM), q_gain)
    yc_c = gqa_attend(q_c, k_c, v_c) @ w_attn_out
    out_c = gated_merge(ya_c, ys_c, yc_c, cg, w_out)
    return out_x, out_c


def sq_relu_mlp(h, w_up, w_down):
    return jnp.square(jax.nn.relu(h @ w_up)) @ w_down


def setup_inputs(seed: int = 0) -> dict:
    key = jax.random.key(seed)
    ks = jax.random.split(key, 32)
    f32 = jnp.float32

    def nrm(k, shape, s):
        return jax.random.normal(k, shape, f32) * s

    g_shape = (DEPTH, SSM_GROUPS, SSM_STATE)
    n_idx = jnp.arange(SSM_STATE, dtype=f32)
    return {
        "x": nrm(ks[0], (BATCH, SEQ, D_MODEL), 1.0),
        "c": nrm(ks[1], (BATCH, D_MODEL), 1.0),
        "ctx": nrm(ks[2], (BATCH, CTX_LEN, D_MODEL), 1.0),
        "c_ctx": nrm(ks[3], (D_MODEL,), 1.0),
        "w_mod": nrm(ks[4], (DEPTH, D_MODEL, N_MOD * D_MODEL), 0.5 * D_MODEL ** -0.5),
        "b_mod": nrm(ks[5], (DEPTH, N_MOD * D_MODEL), 0.02),
        "g_pre_mix": 1.0 + nrm(ks[6], (DEPTH, D_MODEL), 0.05),
        "g_post_mix": 1.0 + nrm(ks[7], (DEPTH, D_MODEL), 0.05),
        "g_pre_mlp": 1.0 + nrm(ks[8], (DEPTH, D_MODEL), 0.05),
        "g_post_mlp": 1.0 + nrm(ks[9], (DEPTH, D_MODEL), 0.05),
        "w_in": nrm(ks[10], (DEPTH, D_MODEL, IN_WIDTH), D_MODEL ** -0.5),
        "conv_w": nrm(ks[11], (DEPTH, CONV_K, CONV_WIDTH), CONV_K ** -0.5),
        "w_conv_out": nrm(ks[12], (DEPTH, CONV_WIDTH, D_MODEL), CONV_WIDTH ** -0.5),
        "ssm_lam_re": -0.5 + nrm(ks[13], (DEPTH, 2, SSM_GROUPS, SSM_STATE), 0.01),
        "ssm_lam_im": math.pi * n_idx + nrm(ks[14], (DEPTH, 2, SSM_GROUPS, SSM_STATE), 0.01),
        "ssm_log_dt": jax.random.uniform(ks[15], (DEPTH, 2, SSM_GROUPS), f32, math.log(1e-3), math.log(1e-1)),
        "ssm_b_re": nrm(ks[16], (DEPTH, 2, SSM_GROUPS, SSM_STATE, SSM_GROUP), (2 * SSM_GROUP) ** -0.5),
        "ssm_b_im": nrm(ks[17], (DEPTH, 2, SSM_GROUPS, SSM_STATE, SSM_GROUP), (2 * SSM_GROUP) ** -0.5),
        "ssm_c_re": nrm(ks[18], (DEPTH, 2, SSM_GROUPS, SSM_GROUP, SSM_STATE), SSM_STATE ** -0.5),
        "ssm_c_im": nrm(ks[19], (DEPTH, 2, SSM_GROUPS, SSM_GROUP, SSM_STATE), SSM_STATE ** -0.5),
        "ssm_d": nrm(ks[20], (DEPTH, SSM_WIDTH), 1.0),
        "w_glu": nrm(ks[21], (DEPTH, SSM_WIDTH, 2 * D_MODEL), SSM_WIDTH ** -0.5),
        "q_gain": 1.0 + nrm(ks[22], (DEPTH, HEAD_DIM), 0.05),
        "k_gain": 1.0 + nrm(ks[23], (DEPTH, HEAD_DIM), 0.05),
        "w_attn_out": nrm(ks[24], (DEPTH, ATTN_WIDTH, D_MODEL), ATTN_WIDTH ** -0.5),
        "w_out": nrm(ks[25], (DEPTH, D_MODEL, D_MODEL), D_MODEL ** -0.5),
        "w_up": nrm(ks[26], (DEPTH, D_MODEL, D_FF), D_MODEL ** -0.5),
        "w_down": nrm(ks[27], (DEPTH, D_FF, D_MODEL), D_FF ** -0.5),
    }


def reference(x, c, ctx, c_ctx, w_mod, b_mod, g_pre_mix, g_post_mix, g_pre_mlp, g_post_mlp, w_in, conv_w,
              w_conv_out, ssm_lam_re, ssm_lam_im, ssm_log_dt, ssm_b_re, ssm_b_im, ssm_c_re, ssm_c_im, ssm_d,
              w_glu, q_gain, k_gain, w_attn_out, w_out, w_up, w_down):
    b, n, _ = x.shape
    rope = axial_rope_tables(n, x.dtype)
    for l in range(DEPTH):
        need_ctx = l < DEPTH - 1
        mod_x = (jax.nn.silu(c) @ w_mod[l] + b_mod[l]).reshape(b, 1, N_MOD, D_MODEL)
        mod_c = (jax.nn.silu(c_ctx) @ w_mod[l] + b_mod[l]).reshape(1, 1, N_MOD, D_MODEL)
        hx = modulate(rms_norm(x, g_pre_mix[l]), mod_x[:, :, 0], mod_x[:, :, 1])
        hc = modulate(rms_norm(ctx, g_pre_mix[l]), mod_c[:, :, 0], mod_c[:, :, 1])
        mx, mc = hybrid_mixer(hx, hc, w_in[l], conv_w[l], w_conv_out[l], ssm_lam_re[l], ssm_lam_im[l],
                              ssm_log_dt[l], ssm_b_re[l], ssm_b_im[l], ssm_c_re[l], ssm_c_im[l], ssm_d[l],
                              w_glu[l], q_gain[l], k_gain[l], w_attn_out[l], w_out[l], rope, need_ctx)
        x = x + mod_x[:, :, 2] * rms_norm(mx, g_post_mix[l])
        hx = modulate(rms_norm(x, g_pre_mlp[l]), mod_x[:, :, 3], mod_x[:, :, 4])
        x = x + mod_x[:, :, 5] * rms_norm(sq_relu_mlp(hx, w_up[l], w_down[l]), g_post_mlp[l])
        if need_ctx:
            ctx = ctx + mod_c[:, :, 2] * rms_norm(mc, g_post_mix[l])
            hc = modulate(rms_norm(ctx, g_pre_mlp[l]), mod_c[:, :, 3], mod_c[:, :, 4])
            ctx = ctx + mod_c[:, :, 5] * rms_norm(sq_relu_mlp(hc, w_up[l], w_down[l]), g_post_mlp[l])
    return x
```

```python
import functools
import math

import jax
import jax.numpy as jnp
from jax import lax
from jax.experimental import pallas as pl
from jax.experimental.pallas import tpu as pltpu

F32 = jnp.float32
BF16 = jnp.bfloat16

NORM_EPS = 1e-6
N_MOD = 6
HEAD_DIM = 128
N_KV_HEADS = 2
GRID_W = 64
ROPE_BASE = 10000.0
CONV_K = 3
SSM_GROUP = 16
SSM_STATE = 64
SSM_RE_MAX = -1e-4
SSM_CHUNK = 16
SUBLANES = 8
LANES = 128
HALO_ROWS = 16
MOD_ROWS = 8
VMEM_LIMIT = 56 * 2**20


def _params(sem, vmem=VMEM_LIMIT):
    return pltpu.CompilerParams(dimension_semantics=sem, vmem_limit_bytes=vmem)


def _pick(dims, cands):
    for c in cands:
        if all(d % c == 0 for d in dims):
            return c
    raise ValueError(f"no tile in {cands} divides {dims}")


def _rms(x, g):
    ms = jnp.mean(x * x, axis=-1, keepdims=True)
    return x * lax.rsqrt(ms + NORM_EPS) * g


def _mod_kernel(c_ref, w_ref, b_ref, o_ref):
    c = c_ref[...]
    s = (c * jax.nn.sigmoid(c)).astype(BF16)
    o_ref[...] = jnp.dot(s, w_ref[...].astype(BF16), preferred_element_type=F32) + b_ref[...]


def _modulation(cc, w_mod, b_mod):
    depth, d, nout = w_mod.shape
    tn = _pick([nout], [1024, 512, 256, 128])
    return pl.pallas_call(
        _mod_kernel,
        grid=(depth, nout // tn),
        in_specs=[
            pl.BlockSpec((MOD_ROWS, d), lambda l, j: (0, 0)),
            pl.BlockSpec((None, d, tn), lambda l, j: (l, 0, j)),
            pl.BlockSpec((None, 1, tn), lambda l, j: (l, 0, j)),
        ],
        out_specs=pl.BlockSpec((None, MOD_ROWS, tn), lambda l, j: (l, 0, j)),
        out_shape=jax.ShapeDtypeStruct((depth, MOD_ROWS, nout), F32),
        compiler_params=_params(("parallel", "parallel")),
        name="modulation",
    )(cc, w_mod, b_mod.reshape(depth, 1, nout))


def _inproj_kernel(x_ref, g_ref, sh_ref, sc_ref, w_ref, om_ref, og_ref, h_scr, *, n_main):
    j = pl.program_id(1)

    @pl.when(j == 0)
    def _():
        h = _rms(x_ref[...], g_ref[...]) * (1.0 + sc_ref[...]) + sh_ref[...]
        h_scr[...] = h.astype(BF16)

    r = jnp.dot(h_scr[...], w_ref[...], preferred_element_type=F32)

    @pl.when(j < n_main)
    def _():
        om_ref[...] = r.astype(om_ref.dtype)

    @pl.when(j >= n_main)
    def _():
        og_ref[...] = r.astype(og_ref.dtype)


def _in_projection(xt, gain, modr, w_in, l, dims):
    nt, d = xt.shape
    main_w, gate_w = dims["main_w"], dims["gate_w"]
    tm = _pick([dims["seq"], dims["n_ctx"]], [1024, 512, 256, 128])
    tn = _pick([main_w, gate_w], [768, 512, 384, 256, 128])
    n_main = main_w // tn
    grp = lambda i: jnp.minimum((i * tm) // dims["seq"], dims["batch"])
    return pl.pallas_call(
        functools.partial(_inproj_kernel, n_main=n_main),
        grid=(nt // tm, (main_w + gate_w) // tn),
        in_specs=[
            pl.BlockSpec((tm, d), lambda i, j: (i, 0)),
            pl.BlockSpec((None, 1, d), lambda i, j: (l, 0, 0)),
            pl.BlockSpec((None, None, 1, d), lambda i, j: (l, grp(i), 0, 0)),
            pl.BlockSpec((None, None, 1, d), lambda i, j: (l, grp(i), 0, 1)),
            pl.BlockSpec((None, d, tn), lambda i, j: (l, 0, j)),
        ],
        out_specs=[
            pl.BlockSpec((tm, tn), lambda i, j: (i, jnp.minimum(j, n_main - 1))),
            pl.BlockSpec((tm, tn), lambda i, j: (i, jnp.maximum(j - n_main, 0))),
        ],
        out_shape=[
            jax.ShapeDtypeStruct((nt, main_w), BF16),
            jax.ShapeDtypeStruct((nt, gate_w), BF16),
        ],
        scratch_shapes=[pltpu.VMEM((tm, d), BF16)],
        compiler_params=_params(("parallel", "arbitrary")),
        name="in_projection",
    )(xt, gain, modr, modr, w_in)


def _qk_kernel(x_ref, cos_ref, sin_ref, qg_ref, kg_ref, q_ref, k_ref, *, n_q, n_kv, scale):
    cos = cos_ref[...]
    sin = sin_ref[...]
    lane = lax.broadcasted_iota(jnp.int32, cos.shape, 1)
    first_half = (lane % (HEAD_DIM // 2)) < (HEAD_DIM // 4)
    for h in range(n_q + n_kv):
        xh = x_ref[:, HEAD_DIM * h:HEAD_DIM * (h + 1)].astype(F32)
        gain = qg_ref[...] * scale if h < n_q else kg_ref[...]
        y = _rms(xh, gain)
        partner = jnp.where(first_half, pltpu.roll(y, HEAD_DIM - HEAD_DIM // 4, 1),
                            pltpu.roll(y, HEAD_DIM // 4, 1))
        r = (y * cos + partner * sin).astype(BF16)
        if h < n_q:
            q_ref[:, HEAD_DIM * h:HEAD_DIM * (h + 1)] = r
        else:
            k_ref[:, HEAD_DIM * (h - n_q):HEAD_DIM * (h - n_q + 1)] = r


def _qk_prepare(main, cos_t, sin_t, q_gain, k_gain, l, dims):
    nt = main.shape[0]
    attn_w, kv_w = dims["attn_w"], dims["kv_w"]
    tm = _pick([dims["seq"], dims["ctx"]], [256, 128, 64, 32])
    qk_w = attn_w + kv_w
    assert dims["o_q"] % qk_w == 0
    n_lat = dims["n_lat"] // tm
    n_seq = dims["seq"] // tm
    tab = lambda i: (jnp.where(i < n_lat, i % n_seq, n_seq), 0)
    return pl.pallas_call(
        functools.partial(_qk_kernel, n_q=attn_w // HEAD_DIM, n_kv=kv_w // HEAD_DIM,
                          scale=HEAD_DIM ** -0.5),
        grid=(nt // tm,),
        in_specs=[
            pl.BlockSpec((tm, qk_w), lambda i: (i, dims["o_q"] // qk_w)),
            pl.BlockSpec((tm, HEAD_DIM), tab),
            pl.BlockSpec((tm, HEAD_DIM), tab),
            pl.BlockSpec((None, 1, HEAD_DIM), lambda i: (l, 0, 0)),
            pl.BlockSpec((None, 1, HEAD_DIM), lambda i: (l, 0, 0)),
        ],
        out_specs=[
            pl.BlockSpec((tm, attn_w), lambda i: (i, 0)),
            pl.BlockSpec((tm, kv_w), lambda i: (i, 0)),
        ],
        out_shape=[
            jax.ShapeDtypeStruct((nt, attn_w), BF16),
            jax.ShapeDtypeStruct((nt, kv_w), BF16),
        ],
        compiler_params=_params(("parallel",)),
        name="qk_prepare",
    )(main, cos_t, sin_t, q_gain, k_gain)


def _rope_tables(seq, pad_rows):
    rows = seq // GRID_W
    row = jnp.repeat(jnp.arange(rows), GRID_W)
    col = jnp.tile(jnp.arange(GRID_W), rows)
    half = HEAD_DIM // 4
    inv_freq = ROPE_BASE ** (-jnp.arange(half, dtype=F32) / half)
    ang_r = row.astype(F32)[:, None] * inv_freq
    ang_c = col.astype(F32)[:, None] * inv_freq
    cr, sr, cc, sc = jnp.cos(ang_r), jnp.sin(ang_r), jnp.cos(ang_c), jnp.sin(ang_c)
    cos_t = jnp.concatenate([cr, cr, cc, cc], axis=-1)
    sin_t = jnp.concatenate([-sr, sr, -sc, sc], axis=-1)
    cos_t = jnp.concatenate([cos_t, jnp.ones((pad_rows, HEAD_DIM), F32)], axis=0)
    sin_t = jnp.concatenate([sin_t, jnp.zeros((pad_rows, HEAD_DIM), F32)], axis=0)
    return cos_t, sin_t


def _attn_kernel(*refs, tq, tk, n_blocks, group):
    if n_blocks:
        q_ref, kc_ref, vc_ref, kx_ref, vx_ref, _, o_ref, m_scr, l_scr, acc_scr = refs
    else:
        q_ref, kc_ref, vc_ref, _, o_ref, m_scr, l_scr, acc_scr = refs
    q = jnp.concatenate([q_ref[:, HEAD_DIM * g:HEAD_DIM * (g + 1)] for g in range(group)], axis=0)

    def step(k, v):
        s = lax.dot_general(q, k, (((1,), (1,)), ((), ())), preferred_element_type=F32)
        m_prev = m_scr[...]
        m_new = jnp.maximum(m_prev, jnp.max(s, axis=-1, keepdims=True))
        alpha = jnp.exp(m_prev - m_new)
        p = jnp.exp(s - m_new)
        l_scr[...] = alpha * l_scr[...] + jnp.sum(p, axis=-1, keepdims=True)
        acc_scr[...] = alpha * acc_scr[...] + jnp.dot(p.astype(BF16), v, preferred_element_type=F32)
        m_scr[...] = m_new

    m_scr[...] = jnp.full(m_scr.shape, -jnp.inf, F32)
    l_scr[...] = jnp.zeros(l_scr.shape, F32)
    acc_scr[...] = jnp.zeros(acc_scr.shape, F32)
    step(kc_ref[...], vc_ref[...])
    if n_blocks:
        def body(j, carry):
            off = pl.multiple_of(j * tk, tk)
            step(kx_ref[pl.ds(off, tk), :], vx_ref[pl.ds(off, tk), :])
            return carry

        lax.fori_loop(0, n_blocks, body, 0)
    o = acc_scr[...] / l_scr[...]
    for g in range(group):
        o_ref[:, HEAD_DIM * g:HEAD_DIM * (g + 1)] = o[tq * g:tq * (g + 1)].astype(o_ref.dtype)


def _attention(q, k, main, prev, dims, latent):
    nt = q.shape[0]
    attn_w, seq, ctx, batch = dims["attn_w"], dims["seq"], dims["ctx"], dims["batch"]
    group = attn_w // HEAD_DIM // N_KV_HEADS
    gw = group * HEAD_DIM
    v_blk = dims["o_v"] // HEAD_DIM
    lq = seq if latent else ctx
    tq = _pick([lq], [128, 64, 32, 16])
    tk = _pick([seq], [512, 256, 128])
    row0 = 0 if latent else dims["n_lat"] // tq
    ctx0 = dims["n_lat"] // ctx
    in_specs = [
        pl.BlockSpec((tq, gw), lambda b, h, i: (row0 + b * (lq // tq) + i, h)),
        pl.BlockSpec((ctx, HEAD_DIM), lambda b, h, i: (ctx0 + b, h)),
        pl.BlockSpec((ctx, HEAD_DIM), lambda b, h, i: (ctx0 + b, v_blk + h)),
    ]
    args = [q, k, main]
    if latent:
        in_specs += [
            pl.BlockSpec((seq, HEAD_DIM), lambda b, h, i: (b, h)),
            pl.BlockSpec((seq, HEAD_DIM), lambda b, h, i: (b, v_blk + h)),
        ]
        args += [k, main]
    in_specs.append(pl.BlockSpec(memory_space=pl.ANY))
    args.append(prev)
    rows = group * tq
    return pl.pallas_call(
        functools.partial(_attn_kernel, tq=tq, tk=tk, n_blocks=seq // tk if latent else 0, group=group),
        grid=(batch, N_KV_HEADS, lq // tq),
        in_specs=in_specs,
        out_specs=pl.BlockSpec((tq, gw), lambda b, h, i: (row0 + b * (lq // tq) + i, h)),
        out_shape=jax.ShapeDtypeStruct((nt, attn_w), BF16),
        scratch_shapes=[pltpu.VMEM((rows, 1), F32), pltpu.VMEM((rows, 1), F32),
                        pltpu.VMEM((rows, HEAD_DIM), F32)],
        input_output_aliases={len(args) - 1: 0},
        compiler_params=_params(("parallel", "parallel", "arbitrary")),
        name="attention_latent" if latent else "attention_context",
    )(*args)


def _ssm_params_kernel(lrr_ref, lir_ref, ldr_ref, lrc_ref, lic_ref, ldc_ref, btr_ref, bti_ref,
                       ctr_ref, cti_ref, r1_ref, v_ref, a_ref):
    t = SSM_CHUNK
    pw = SSM_GROUP * t
    r1_ref[...] = jnp.zeros(r1_ref.shape, r1_ref.dtype)
    v_ref[...] = jnp.zeros(v_ref.shape, v_ref.dtype)
    lane = lax.broadcasted_iota(jnp.int32, (SSM_GROUP, LANES), 1)
    col = lax.broadcasted_iota(jnp.int32, (SSM_GROUP, pw), 1)
    tau_col = (lax.broadcasted_iota(jnp.int32, (LANES, pw), 1) // SSM_GROUP).astype(F32)
    tau_row = lax.broadcasted_iota(jnp.int32, (t, LANES), 0).astype(F32)
    masks = (lane < SSM_STATE, lane >= SSM_STATE)
    toep = []
    for d in range(2):
        lr = jnp.minimum(lrr_ref[d], SSM_RE_MAX)
        li = lir_ref[d]
        dt = jnp.exp(ldr_ref[d])
        mag = jnp.exp(lr * dt)
        ab_re = mag * jnp.cos(li * dt)
        ab_im = mag * jnp.sin(li * dt)
        nr = ab_re - 1.0
        den = lr * lr + li * li
        f_re = (nr * lr + ab_im * li) / den
        f_im = (ab_im * lr - nr * li) / den
        bb_re = f_re * btr_ref[d] - f_im * bti_ref[d]
        bb_im = f_re * bti_ref[d] + f_im * btr_ref[d]
        e16 = jnp.exp(lr * dt * float(t))
        a_ref[2 * d:2 * d + 1, :] = e16 * jnp.cos(li * dt * float(t))
        a_ref[2 * d + 1:2 * d + 2, :] = e16 * jnp.sin(li * dt * float(t))
        tau_s = (t - 1.0) - tau_row if d == 0 else tau_row
        es = jnp.exp(lr * dt * tau_s)
        pw_re = es * jnp.cos(li * dt * tau_s)
        pw_im = es * jnp.sin(li * dt * tau_s)
        for s in range(t):
            ar = pw_re[s:s + 1]
            ai = pw_im[s:s + 1]
            w_re = ar * bb_re - ai * bb_im
            w_im = ar * bb_im + ai * bb_re
            for gi in range(2):
                r0 = gi * pw + SSM_GROUP * s
                c0 = 2 * pw + 2 * LANES * d
                r1_ref[r0:r0 + SSM_GROUP, c0:c0 + LANES] = jnp.where(masks[gi], w_re, 0.0).astype(r1_ref.dtype)
                r1_ref[r0:r0 + SSM_GROUP, c0 + LANES:c0 + 2 * LANES] = (
                    jnp.where(masks[gi], w_im, 0.0).astype(r1_ref.dtype))
        lrc = jnp.minimum(lrc_ref[d], SSM_RE_MAX)
        lic = lic_ref[d]
        dtc = jnp.exp(ldc_ref[d])
        tau_q = tau_col if d == 0 else (t - 1.0) - tau_col
        eq = jnp.exp(lrc * dtc * tau_q)
        p_re = eq * jnp.cos(lic * dtc * tau_q)
        p_im = eq * jnp.sin(lic * dtc * tau_q)
        c_re = ctr_ref[d]
        c_im = cti_ref[d]
        q_re = p_re * c_re - p_im * c_im
        q_im = p_re * c_im + p_im * c_re
        per_group = []
        for gi in range(2):
            k_mat = (jnp.dot(jnp.where(masks[gi], bb_re, 0.0), q_re, precision=lax.Precision.HIGHEST,
                             preferred_element_type=F32)
                     - jnp.dot(jnp.where(masks[gi], bb_im, 0.0), q_im, precision=lax.Precision.HIGHEST,
                               preferred_element_type=F32))
            per_group.append(k_mat)
        toep.append(per_group)
        magc = jnp.exp(lrc * dtc)
        abc_re = magc * jnp.cos(lic * dtc)
        abc_im = magc * jnp.sin(lic * dtc)
        q1_re = abc_re * q_re - abc_im * q_im
        q1_im = abc_re * q_im + abc_im * q_re
        for gi in range(2):
            rows = slice(SSM_STATE * gi, SSM_STATE * (gi + 1))
            r0 = 2 * LANES * d + SSM_STATE * gi
            v_ref[r0:r0 + SSM_STATE, pw * gi:pw * (gi + 1)] = q1_re[rows].astype(v_ref.dtype)
            v_ref[r0 + LANES:r0 + LANES + SSM_STATE, pw * gi:pw * (gi + 1)] = (-q1_im[rows]).astype(v_ref.dtype)
    for gi in range(2):
        k_f, k_b = toep[0][gi], toep[1][gi]
        for s in range(t):
            sh = SSM_GROUP * s
            f = k_f if sh == 0 else jnp.where(col >= sh, pltpu.roll(k_f, sh, 1), 0.0)
            back = SSM_GROUP * (t - 1 - s)
            b = k_b if back == 0 else jnp.where(col < pw - back, pltpu.roll(k_b, pw - back, 1), 0.0)
            r0 = gi * pw + sh
            r1_ref[r0:r0 + SSM_GROUP, pw * gi:pw * (gi + 1)] = (f + b).astype(r1_ref.dtype)


def _pair_rows(a):
    depth, _, groups, n = a.shape
    return a.reshape(depth, 2, groups // 2, 2 * n).transpose(0, 2, 1, 3)


def _ssm_parameters(lam_re, lam_im, log_dt, b_re, b_im, c_re, c_im):
    depth, _, groups, n = lam_re.shape
    pairs = groups // 2
    t = SSM_CHUNK
    pw = SSM_GROUP * t
    ld = jnp.broadcast_to(log_dt[..., None], lam_re.shape)
    rows = [_pair_rows(a.astype(F32))[:, :, :, None, :] for a in (lam_re, lam_im, ld)]
    cols = [_pair_rows(a.astype(F32))[:, :, :, :, None] for a in (lam_re, lam_im, ld)]

    def bt(b):
        return (b.astype(F32).reshape(depth, 2, pairs, 2, n, SSM_GROUP).transpose(0, 2, 1, 5, 3, 4)
                .reshape(depth, pairs, 2, SSM_GROUP, 2 * n))

    def ct(c):
        c = (c.astype(F32).reshape(depth, 2, pairs, 2, SSM_GROUP, n).transpose(0, 2, 1, 3, 5, 4)
             .reshape(depth, pairs, 2, 2 * n, SSM_GROUP))
        return jnp.tile(c, (1, 1, 1, 1, t))

    row_spec = pl.BlockSpec((None, None, 2, 1, 2 * n), lambda l, j: (l, j, 0, 0, 0))
    col_spec = pl.BlockSpec((None, None, 2, 2 * n, 1), lambda l, j: (l, j, 0, 0, 0))
    bt_spec = pl.BlockSpec((None, None, 2, SSM_GROUP, 2 * n), lambda l, j: (l, j, 0, 0, 0))
    ct_spec = pl.BlockSpec((None, None, 2, 2 * n, pw), lambda l, j: (l, j, 0, 0, 0))
    return pl.pallas_call(
        _ssm_params_kernel,
        grid=(depth, pairs),
        in_specs=[row_spec] * 3 + [col_spec] * 3 + [bt_spec] * 2 + [ct_spec] * 2,
        out_specs=[
            pl.BlockSpec((None, None, 2 * pw, 4 * pw), lambda l, j: (l, j, 0, 0)),
            pl.BlockSpec((None, None, 8 * n, 2 * pw), lambda l, j: (l, j, 0, 0)),
            pl.BlockSpec((None, 4, 2 * n), lambda l, j: (l, 0, j)),
        ],
        out_shape=[
            jax.ShapeDtypeStruct((depth, pairs, 2 * pw, 4 * pw), BF16),
            jax.ShapeDtypeStruct((depth, pairs, 8 * n, 2 * pw), BF16),
            jax.ShapeDtypeStruct((depth, 4, groups * n), F32),
        ],
        compiler_params=_params(("parallel", "parallel")),
        name="ssm_parameters",
    )(*rows, *cols, bt(b_re), bt(b_im), ct(c_re), ct(c_im))


def _ssm_state_kernel(u_ref, w_ref, s_ref):
    r = jnp.dot(u_ref[...], w_ref[...], preferred_element_type=F32)
    for k in range(4):
        s_ref[k] = r[:, LANES * k:LANES * (k + 1)]


def _ssm_chunk_states(u, r1, l):
    pairs, nc, kw = u.shape
    tc = _pick([nc], [2112, 1056, 704, 512, 256, 128, 64, 32, 16, 8])
    return pl.pallas_call(
        _ssm_state_kernel,
        grid=(pairs, nc // tc),
        in_specs=[
            pl.BlockSpec((None, tc, kw), lambda j, i: (j, i, 0)),
            pl.BlockSpec((None, None, kw, kw), lambda j, i: (l, j, 0, 1)),
        ],
        out_specs=pl.BlockSpec((4, tc, LANES), lambda j, i: (0, i, j)),
        out_shape=jax.ShapeDtypeStruct((4, nc, pairs * LANES), F32),
        compiler_params=_params(("parallel", "parallel")),
        name="ssm_chunk_states",
    )(u, r1)


def _ssm_scan_kernel(s_ref, a_ref, h_ref, *, batch, lat_tiles, ctx_tiles):
    w = s_ref.shape[-1]
    per_tile = SUBLANES // batch
    row_grp = lax.broadcasted_iota(jnp.int32, (SUBLANES, w), 0) // batch
    zero = jnp.zeros((SUBLANES, w), F32)

    def make_tile(k, order, shift):
        ar = a_ref[k:k + 1, :]
        ai = a_ref[k + 1:k + 2, :]

        def tile(ti, carry):
            cur_re, cur_im = carry
            rows = pl.ds(pl.multiple_of(ti * SUBLANES, SUBLANES), SUBLANES)
            s_re = s_ref[k, rows, :]
            s_im = s_ref[k + 1, rows, :]
            h_re, h_im = zero, zero
            for pos in order:
                sel = row_grp == pos
                h_re = jnp.where(sel, cur_re, h_re)
                h_im = jnp.where(sel, cur_im, h_im)
                nxt_re = ar * cur_re - ai * cur_im + s_re
                nxt_im = ar * cur_im + ai * cur_re + s_im
                if batch == SUBLANES:
                    cur_re, cur_im = nxt_re, nxt_im
                else:
                    cur_re = pltpu.roll(nxt_re, shift, 0)
                    cur_im = pltpu.roll(nxt_im, shift, 0)
            h_ref[k, rows, :] = h_re
            h_ref[k + 1, rows, :] = h_im
            return cur_re, cur_im

        return tile

    fwd = make_tile(0, range(per_tile), batch)
    carry = lax.fori_loop(0, ctx_tiles, lambda i, c: fwd(lat_tiles + i, c), (zero, zero))
    lax.fori_loop(0, lat_tiles, fwd, carry)
    bwd = make_tile(2, range(per_tile - 1, -1, -1), SUBLANES - batch)
    carry = lax.fori_loop(0, ctx_tiles, lambda i, c: bwd(lat_tiles + ctx_tiles - 1 - i, c), (zero, zero))
    lax.fori_loop(0, lat_tiles, lambda i, c: bwd(lat_tiles - 1 - i, c), carry)


def _ssm_scan(s, a16, l, dims):
    _, nc, width = s.shape
    batch = dims["batch"]
    lat_tiles = dims["seq"] // SSM_CHUNK * batch // SUBLANES
    ctx_tiles = dims["ctx"] // SSM_CHUNK * batch // SUBLANES
    return pl.pallas_call(
        functools.partial(_ssm_scan_kernel, batch=batch, lat_tiles=lat_tiles, ctx_tiles=ctx_tiles),
        grid=(width // LANES,),
        in_specs=[
            pl.BlockSpec((4, nc, LANES), lambda j: (0, 0, j)),
            pl.BlockSpec((None, 4, LANES), lambda j: (l, 0, j)),
        ],
        out_specs=pl.BlockSpec((4, nc, LANES), lambda j: (0, 0, j)),
        out_shape=jax.ShapeDtypeStruct((4, nc, width), F32),
        compiler_params=_params(("parallel",)),
        name="ssm_scan",
    )(s, a16)


def _ssm_out_kernel(u_ref, m_ref, h_ref, v_ref, d_ref, y_ref):
    u = u_ref[...]
    h = jnp.concatenate([h_ref[k] for k in range(4)], axis=-1).astype(BF16)
    y = jnp.dot(u, m_ref[...], preferred_element_type=F32)
    y = y + jnp.dot(h, v_ref[...], preferred_element_type=F32)
    y_ref[...] = (y + d_ref[...] * u.astype(F32)).astype(y_ref.dtype)


def _ssm_outputs(u, r1, h, v, dsk, l):
    pairs, nc, kw = u.shape
    tc = _pick([nc], [2112, 1056, 704, 512, 256, 128, 64, 32, 16, 8])
    return pl.pallas_call(
        _ssm_out_kernel,
        grid=(pairs, nc // tc),
        in_specs=[
            pl.BlockSpec((None, tc, kw), lambda j, i: (j, i, 0)),
            pl.BlockSpec((None, None, kw, kw), lambda j, i: (l, j, 0, 0)),
            pl.BlockSpec((4, tc, LANES), lambda j, i: (0, i, j)),
            pl.BlockSpec((None, None, 4 * LANES, kw), lambda j, i: (l, j, 0, 0)),
            pl.BlockSpec((None, None, 1, kw), lambda j, i: (l, j, 0, 0)),
        ],
        out_specs=pl.BlockSpec((None, tc, kw), lambda j, i: (j, i, 0)),
        out_shape=jax.ShapeDtypeStruct((pairs, nc, kw), F32),
        compiler_params=_params(("parallel", "parallel")),
        name="ssm_outputs",
    )(u, r1, h, v, dsk)


def _to_chunks(a, dims):
    batch, pairs = dims["batch"], dims["pairs"]

    def part(x, length):
        x = x.reshape(batch, length // SSM_CHUNK, SSM_CHUNK, pairs, 2, SSM_GROUP)
        x = x.transpose(3, 1, 0, 4, 2, 5)
        return x.reshape(pairs, length // SSM_CHUNK * batch, 2 * SSM_CHUNK * SSM_GROUP)

    return jnp.concatenate([part(a[:dims["n_lat"]], dims["seq"]), part(a[dims["n_lat"]:], dims["ctx"])], axis=1)


def _from_chunks(y, dims):
    batch, pairs = dims["batch"], dims["pairs"]

    def part(x, length):
        x = x.reshape(pairs, length // SSM_CHUNK, batch, 2, SSM_CHUNK, SSM_GROUP)
        x = x.transpose(2, 1, 4, 0, 3, 5)
        return x.reshape(batch * length, pairs * 2 * SSM_GROUP)

    n_lat_rows = dims["n_lat"] // SSM_CHUNK
    return jnp.concatenate([part(y[:, :n_lat_rows], dims["seq"]), part(y[:, n_lat_rows:], dims["ctx"])], axis=0)


def _merge_kernel(pb_ref, pc_ref, pv_ref, hpc_ref, hpv_ref, hnc_ref, hnv_ref, cw_ref, ys_ref, at_ref,
                  g0_ref, g1_ref, g2_ref, wc_ref, wga_ref, wgg_ref, wa_ref, o_ref, cf_scr, yg_scr,
                  *, tm, seq, ctx, n_lat):
    i = pl.program_id(0)
    j = pl.program_id(1)

    @pl.when(j == 0)
    def _():
        r0 = i * tm
        period = jnp.where(r0 >= n_lat, ctx, seq)
        starts = r0 % period == 0
        ends = (r0 + tm) % period == 0
        w = pc_ref[...].astype(F32) * pv_ref[...].astype(F32)
        prev = (hpc_ref[...].astype(F32) * hpv_ref[...].astype(F32))[HALO_ROWS - 1:HALO_ROWS]
        nxt = (hnc_ref[...].astype(F32) * hnv_ref[...].astype(F32))[0:1]
        prev = jnp.where(starts, 0.0, prev)
        nxt = jnp.where(ends, 0.0, nxt)
        row = lax.broadcasted_iota(jnp.int32, w.shape, 0)
        up = jnp.where(row == 0, prev, pltpu.roll(w, 1, 0))
        dn = jnp.where(row == tm - 1, nxt, pltpu.roll(w, tm - 1, 0))
        cw = cw_ref[...]
        y = pb_ref[...].astype(F32) * (cw[0:1] * up + cw[1:2] * w + cw[2:3] * dn)
        cf_scr[...] = y.astype(BF16)
        yg_scr[...] = jax.nn.gelu(ys_ref[...].astype(F32)).astype(BF16)

    ya = jnp.dot(cf_scr[...], wc_ref[...], preferred_element_type=F32)
    yc = jnp.dot(at_ref[...], wa_ref[...], preferred_element_type=F32)
    yg = yg_scr[...]
    ys = jnp.dot(yg, wga_ref[...], preferred_element_type=F32) * jax.nn.sigmoid(
        jnp.dot(yg, wgg_ref[...], preferred_element_type=F32))
    m = (jax.nn.sigmoid(g0_ref[...].astype(F32)) * ya + jax.nn.sigmoid(g1_ref[...].astype(F32)) * ys
         + jax.nn.sigmoid(g2_ref[...].astype(F32)) * yc)
    o_ref[...] = m.astype(o_ref.dtype)


def _merge(main, gates, ys, attn, conv_w, w_conv_out, w_glu, w_attn_out, l, dims, rows):
    d = dims["d"]
    cw_, sw, aw = dims["conv_w"], dims["ssm_w"], dims["attn_w"]
    tm = _pick([dims["seq"], dims["ctx"]], [512, 256, 128, 64, 32])
    tn = _pick([d], [512, 256, 128])
    n_halo = main.shape[0] // HALO_ROWS
    per = tm // HALO_ROWS
    nd = d // tn
    prev_blk = lambda i: jnp.maximum(i * per - 1, 0)
    next_blk = lambda i: jnp.minimum((i + 1) * per, n_halo - 1)
    return pl.pallas_call(
        functools.partial(_merge_kernel, tm=tm, seq=dims["seq"], ctx=dims["ctx"], n_lat=dims["n_lat"]),
        grid=(rows // tm, nd),
        in_specs=[
            pl.BlockSpec((tm, cw_), lambda i, j: (i, 0)),
            pl.BlockSpec((tm, cw_), lambda i, j: (i, 1)),
            pl.BlockSpec((tm, cw_), lambda i, j: (i, 2)),
            pl.BlockSpec((HALO_ROWS, cw_), lambda i, j: (prev_blk(i), 1)),
            pl.BlockSpec((HALO_ROWS, cw_), lambda i, j: (prev_blk(i), 2)),
            pl.BlockSpec((HALO_ROWS, cw_), lambda i, j: (next_blk(i), 1)),
            pl.BlockSpec((HALO_ROWS, cw_), lambda i, j: (next_blk(i), 2)),
            pl.BlockSpec((None, CONV_K, cw_), lambda i, j: (l, 0, 0)),
            pl.BlockSpec((tm, sw), lambda i, j: (i, 0)),
            pl.BlockSpec((tm, aw), lambda i, j: (i, 0)),
            pl.BlockSpec((tm, tn), lambda i, j: (i, j)),
            pl.BlockSpec((tm, tn), lambda i, j: (i, nd + j)),
            pl.BlockSpec((tm, tn), lambda i, j: (i, 2 * nd + j)),
            pl.BlockSpec((None, cw_, tn), lambda i, j: (l, 0, j)),
            pl.BlockSpec((None, sw, tn), lambda i, j: (l, 0, j)),
            pl.BlockSpec((None, sw, tn), lambda i, j: (l, 0, nd + j)),
            pl.BlockSpec((None, aw, tn), lambda i, j: (l, 0, j)),
        ],
        out_specs=pl.BlockSpec((tm, tn), lambda i, j: (i, j)),
        out_shape=jax.ShapeDtypeStruct((rows, d), BF16),
        scratch_shapes=[pltpu.VMEM((tm, cw_), BF16), pltpu.VMEM((tm, sw), BF16)],
        compiler_params=_params(("parallel", "arbitrary")),
        name="branch_merge",
    )(main, main, main, main, main, main, main, conv_w, ys, attn, gates, gates, gates,
      w_conv_out, w_glu, w_glu, w_attn_out)


def _outproj_kernel(m_ref, w_ref, x_ref, gate_ref, g_ref, o_ref):
    r = jnp.dot(m_ref[...], w_ref[...], preferred_element_type=F32)
    o_ref[...] = x_ref[...] + gate_ref[...] * _rms(r, g_ref[...])


def _out_projection(m, w_out, xt, modr, g_post, l, dims):
    rows, d = m.shape
    tm = _pick([dims["seq"], dims["n_ctx"]], [512, 256, 128])
    grp = lambda i: jnp.minimum((i * tm) // dims["seq"], dims["batch"])
    return pl.pallas_call(
        _outproj_kernel,
        grid=(rows // tm,),
        in_specs=[
            pl.BlockSpec((tm, d), lambda i: (i, 0)),
            pl.BlockSpec((None, d, d), lambda i: (l, 0, 0)),
            pl.BlockSpec((tm, d), lambda i: (i, 0)),
            pl.BlockSpec((None, None, 1, d), lambda i: (l, grp(i), 0, 2)),
            pl.BlockSpec((None, 1, d), lambda i: (l, 0, 0)),
        ],
        out_specs=pl.BlockSpec((tm, d), lambda i: (i, 0)),
        out_shape=jax.ShapeDtypeStruct(xt.shape, F32),
        input_output_aliases={2: 0},
        compiler_params=_params(("parallel",)),
        name="out_projection",
    )(m, w_out, xt, modr, g_post)


def _mlp_kernel(x_ref, gpre_ref, sh_ref, sc_ref, wu_ref, wd_ref, gate_ref, gpost_ref, o_ref, h_scr, acc_scr):
    j = pl.program_id(1)

    @pl.when(j == 0)
    def _():
        h = _rms(x_ref[...], gpre_ref[...]) * (1.0 + sc_ref[...]) + sh_ref[...]
        h_scr[...] = h.astype(BF16)
        acc_scr[...] = jnp.zeros(acc_scr.shape, F32)

    u = jnp.maximum(jnp.dot(h_scr[...], wu_ref[...], preferred_element_type=F32), 0.0)
    acc_scr[...] += jnp.dot((u * u).astype(BF16), wd_ref[...], preferred_element_type=F32)

    @pl.when(j == pl.num_programs(1) - 1)
    def _():
        o_ref[...] = x_ref[...] + gate_ref[...] * _rms(acc_scr[...], gpost_ref[...])


def _mlp(xt, g_pre, g_post, modr, w_up, w_down, l, dims, rows):
    d = xt.shape[1]
    d_ff = w_up.shape[-1]
    tm = _pick([dims["seq"], dims["n_ctx"]], [512, 256, 128])
    tf = _pick([d_ff], [512, 256, 128])
    grp = lambda i: jnp.minimum((i * tm) // dims["seq"], dims["batch"])
    mod = lambda k: pl.BlockSpec((None, None, 1, d), lambda i, j: (l, grp(i), 0, k))
    vec = pl.BlockSpec((None, 1, d), lambda i, j: (l, 0, 0))
    return pl.pallas_call(
        _mlp_kernel,
        grid=(rows // tm, d_ff // tf),
        in_specs=[
            pl.BlockSpec((tm, d), lambda i, j: (i, 0)),
            vec, mod(3), mod(4),
            pl.BlockSpec((None, d, tf), lambda i, j: (l, 0, j)),
            pl.BlockSpec((None, tf, d), lambda i, j: (l, j, 0)),
            mod(5), vec,
        ],
        out_specs=pl.BlockSpec((tm, d), lambda i, j: (i, 0)),
        out_shape=jax.ShapeDtypeStruct(xt.shape, F32),
        scratch_shapes=[pltpu.VMEM((tm, d), BF16), pltpu.VMEM((tm, d), F32)],
        input_output_aliases={0: 0},
        compiler_params=_params(("parallel", "arbitrary")),
        name="mlp",
    )(xt, g_pre, modr, modr, w_up, w_down, modr, g_post)


def kernel(x, c, ctx, c_ctx, w_mod, b_mod, g_pre_mix, g_post_mix, g_pre_mlp, g_post_mlp, w_in, conv_w,
           w_conv_out, ssm_lam_re, ssm_lam_im, ssm_log_dt, ssm_b_re, ssm_b_im, ssm_c_re, ssm_c_im, ssm_d,
           w_glu, q_gain, k_gain, w_attn_out, w_out, w_up, w_down):
    batch, seq, d = x.shape
    ctx_len = ctx.shape[1]
    depth = w_in.shape[0]
    in_w = w_in.shape[-1]
    conv_width = conv_w.shape[-1]
    ssm_w = ssm_d.shape[-1]
    attn_w = w_attn_out.shape[1]
    gate_w = 3 * d
    kv_w = (in_w - 3 * conv_width - ssm_w - attn_w - gate_w) // 2
    groups = ssm_lam_re.shape[2]
    assert batch + 1 <= MOD_ROWS and SUBLANES % batch == 0
    assert kv_w == N_KV_HEADS * HEAD_DIM and ssm_lam_re.shape[3] == SSM_STATE
    assert ssm_w == groups * SSM_GROUP and groups % 2 == 0
    assert seq % GRID_W == 0 and (batch * seq) % ctx_len == 0
    dims = dict(
        batch=batch, seq=seq, ctx=ctx_len, d=d, n_lat=batch * seq, n_ctx=batch * ctx_len,
        conv_w=conv_width, ssm_w=ssm_w, attn_w=attn_w, kv_w=kv_w, gate_w=gate_w, pairs=groups // 2,
        o_u=3 * conv_width, o_q=3 * conv_width + ssm_w, o_k=3 * conv_width + ssm_w + attn_w,
        o_v=3 * conv_width + ssm_w + attn_w + kv_w, main_w=in_w - gate_w,
    )
    n_lat, nt = dims["n_lat"], dims["n_lat"] + dims["n_ctx"]

    xt = jnp.concatenate([x.reshape(n_lat, d), ctx.reshape(dims["n_ctx"], d)], axis=0)
    cc = jnp.concatenate([c, c_ctx[None, :], jnp.zeros((MOD_ROWS - batch - 1, d), F32)], axis=0)
    mod = _modulation(cc, w_mod, b_mod)
    modr = mod.reshape(depth, MOD_ROWS, 1, N_MOD * d)

    row_vec = lambda a: a.reshape(depth, 1, a.shape[-1])
    g_pre_mix, g_post_mix, g_pre_mlp, g_post_mlp = map(row_vec, (g_pre_mix, g_post_mix, g_pre_mlp, g_post_mlp))
    q_gain, k_gain = row_vec(q_gain), row_vec(k_gain)
    w_in, w_conv_out, w_glu, w_attn_out, w_out, w_up, w_down = (
        w.astype(BF16) for w in (w_in, w_conv_out, w_glu, w_attn_out, w_out, w_up, w_down))

    qk_tile = _pick([seq, ctx_len], [256, 128, 64, 32])
    cos_t, sin_t = _rope_tables(seq, qk_tile)
    r1, v_mat, a16 = _ssm_parameters(ssm_lam_re, ssm_lam_im, ssm_log_dt, ssm_b_re, ssm_b_im, ssm_c_re, ssm_c_im)
    dsk = jnp.broadcast_to(
        ssm_d.astype(F32).reshape(depth, groups // 2, 2, 1, SSM_GROUP),
        (depth, groups // 2, 2, SSM_CHUNK, SSM_GROUP)).reshape(depth, groups // 2, 1, 2 * SSM_CHUNK * SSM_GROUP)

    for l in range(depth):
        need_ctx = l < depth - 1
        rows = nt if need_ctx else n_lat
        main, gates = _in_projection(xt, g_pre_mix, modr, w_in, l, dims)
        u = _to_chunks(main[:, dims["o_u"]:dims["o_q"]], dims)
        s = _ssm_chunk_states(u, r1, l)
        h = _ssm_scan(s, a16, l, dims)
        ys = _from_chunks(_ssm_outputs(u, r1, h, v_mat, dsk, l), dims)
        q, k = _qk_prepare(main, cos_t, sin_t, q_gain, k_gain, l, dims)
        attn = _attention(q, k, main, jnp.zeros((nt, attn_w), BF16), dims, latent=True)
        if need_ctx:
            attn = _attention(q, k, main, attn, dims, latent=False)
        m = _merge(main, gates, ys, attn, conv_w, w_conv_out, w_glu, w_attn_out, l, dims, rows)
        xt = _out_projection(m, w_out, xt, modr, g_post_mix, l, dims)
        xt = _mlp(xt, g_pre_mlp, g_post_mlp, modr, w_up, w_down, l, dims, rows)
    return xt[:n_lat].reshape(batch, seq, d)
```

```python
import functools
import math

import jax
import jax.numpy as jnp
from jax import lax
from jax.experimental import pallas as pl
from jax.experimental.pallas import tpu as pltpu

F32 = jnp.float32
BF16 = jnp.bfloat16

NORM_EPS = 1e-6
N_MOD = 6
HEAD_DIM = 128
N_KV_HEADS = 2
GRID_W = 64
ROPE_BASE = 10000.0
CONV_K = 3
SSM_GROUP = 16
SSM_STATE = 64
SSM_RE_MAX = -1e-4
SSM_CHUNK = 16
SUBLANES = 8
LANES = 128
HALO_ROWS = 16
MOD_ROWS = 8
VMEM_LIMIT = 56 * 2**20


def _params(sem, vmem=VMEM_LIMIT):
    return pltpu.CompilerParams(dimension_semantics=sem, vmem_limit_bytes=vmem)


def _pick(dims, cands):
    for c in cands:
        if all(d % c == 0 for d in dims):
            return c
    raise ValueError(f"no tile in {cands} divides {dims}")


def _rms(x, g):
    ms = jnp.mean(x * x, axis=-1, keepdims=True)
    return x * lax.rsqrt(ms + NORM_EPS) * g


def _mod_kernel(c_ref, w_ref, b_ref, o_ref):
    c = c_ref[...]
    s = (c * jax.nn.sigmoid(c)).astype(BF16)
    o_ref[...] = jnp.dot(s, w_ref[...].astype(BF16), preferred_element_type=F32) + b_ref[...]


def _modulation(cc, w_mod, b_mod):
    depth, d, nout = w_mod.shape
    tn = _pick([nout], [1024, 512, 256, 128])
    return pl.pallas_call(
        _mod_kernel,
        grid=(depth, nout // tn),
        in_specs=[
            pl.BlockSpec((MOD_ROWS, d), lambda l, j: (0, 0)),
            pl.BlockSpec((None, d, tn), lambda l, j: (l, 0, j)),
            pl.BlockSpec((None, 1, tn), lambda l, j: (l, 0, j)),
        ],
        out_specs=pl.BlockSpec((None, MOD_ROWS, tn), lambda l, j: (l, 0, j)),
        out_shape=jax.ShapeDtypeStruct((depth, MOD_ROWS, nout), F32),
        compiler_params=_params(("parallel", "parallel")),
        name="modulation",
    )(cc, w_mod, b_mod.reshape(depth, 1, nout))


def _inproj_kernel(x_ref, g_ref, sh_ref, sc_ref, w_ref, om_ref, og_ref, ou_ref, h_scr, *, n_main, j_ssm):
    j = pl.program_id(1)

    @pl.when(j == 0)
    def _():
        h = _rms(x_ref[...], g_ref[...]) * (1.0 + sc_ref[...]) + sh_ref[...]
        h_scr[...] = h.astype(BF16)

    r = jnp.dot(h_scr[...], w_ref[...], preferred_element_type=F32)

    @pl.when(j < n_main)
    def _():
        om_ref[...] = r.astype(om_ref.dtype)

    @pl.when(j >= n_main)
    def _():
        og_ref[...] = r.astype(og_ref.dtype)

    @pl.when(j == j_ssm)
    def _():
        for t in range(ou_ref.shape[0]):
            ou_ref[t] = r[:, LANES * t:LANES * (t + 1)]


def _in_projection(xt, gain, modr, w_in, l, dims):
    nt, d = xt.shape
    main_w, gate_w, tn = dims["main_w"], dims["gate_w"], dims["ssm_w"]
    tm = _pick([dims["seq"], dims["n_ctx"]], [1024, 512, 256, 128])
    assert main_w % tn == 0 and gate_w % tn == 0 and dims["o_u"] % tn == 0
    n_main = main_w // tn
    grp = lambda i: jnp.minimum((i * tm) // dims["seq"], dims["batch"])
    return pl.pallas_call(
        functools.partial(_inproj_kernel, n_main=n_main, j_ssm=dims["o_u"] // tn),
        grid=(nt // tm, (main_w + gate_w) // tn),
        in_specs=[
            pl.BlockSpec((tm, d), lambda i, j: (i, 0)),
            pl.BlockSpec((None, 1, d), lambda i, j: (l, 0, 0)),
            pl.BlockSpec((None, None, 1, d), lambda i, j: (l, grp(i), 0, 0)),
            pl.BlockSpec((None, None, 1, d), lambda i, j: (l, grp(i), 0, 1)),
            pl.BlockSpec((None, d, tn), lambda i, j: (l, 0, j)),
        ],
        out_specs=[
            pl.BlockSpec((tm, tn), lambda i, j: (i, jnp.minimum(j, n_main - 1))),
            pl.BlockSpec((tm, tn), lambda i, j: (i, jnp.maximum(j - n_main, 0))),
            pl.BlockSpec((tn // LANES, tm, LANES), lambda i, j: (0, i, 0)),
        ],
        out_shape=[
            jax.ShapeDtypeStruct((nt, main_w), BF16),
            jax.ShapeDtypeStruct((nt, gate_w), BF16),
            jax.ShapeDtypeStruct((tn // LANES, nt, LANES), F32),
        ],
        scratch_shapes=[pltpu.VMEM((tm, d), BF16)],
        compiler_params=_params(("parallel", "arbitrary")),
        name="in_projection",
    )(xt, gain, modr, modr, w_in)


def _qk_kernel(x_ref, cos_ref, sin_ref, qg_ref, kg_ref, q_ref, k_ref, *, n_q, n_kv, scale):
    cos = cos_ref[...]
    sin = sin_ref[...]
    lane = lax.broadcasted_iota(jnp.int32, cos.shape, 1)
    first_half = (lane % (HEAD_DIM // 2)) < (HEAD_DIM // 4)
    for h in range(n_q + n_kv):
        xh = x_ref[:, HEAD_DIM * h:HEAD_DIM * (h + 1)].astype(F32)
        gain = qg_ref[...] * scale if h < n_q else kg_ref[...]
        y = _rms(xh, gain)
        partner = jnp.where(first_half, pltpu.roll(y, HEAD_DIM - HEAD_DIM // 4, 1),
                            pltpu.roll(y, HEAD_DIM // 4, 1))
        r = (y * cos + partner * sin).astype(BF16)
        if h < n_q:
            q_ref[:, HEAD_DIM * h:HEAD_DIM * (h + 1)] = r
        else:
            k_ref[:, HEAD_DIM * (h - n_q):HEAD_DIM * (h - n_q + 1)] = r


def _qk_prepare(main, cos_t, sin_t, q_gain, k_gain, l, dims):
    nt = main.shape[0]
    attn_w, kv_w = dims["attn_w"], dims["kv_w"]
    tm = _pick([dims["seq"], dims["ctx"]], [256, 128, 64, 32])
    qk_w = attn_w + kv_w
    assert dims["o_q"] % qk_w == 0
    n_lat = dims["n_lat"] // tm
    n_seq = dims["seq"] // tm
    tab = lambda i: (jnp.where(i < n_lat, i % n_seq, n_seq), 0)
    return pl.pallas_call(
        functools.partial(_qk_kernel, n_q=attn_w // HEAD_DIM, n_kv=kv_w // HEAD_DIM,
                          scale=HEAD_DIM ** -0.5 * math.log2(math.e)),
        grid=(nt // tm,),
        in_specs=[
            pl.BlockSpec((tm, qk_w), lambda i: (i, dims["o_q"] // qk_w)),
            pl.BlockSpec((tm, HEAD_DIM), tab),
            pl.BlockSpec((tm, HEAD_DIM), tab),
            pl.BlockSpec((None, 1, HEAD_DIM), lambda i: (l, 0, 0)),
            pl.BlockSpec((None, 1, HEAD_DIM), lambda i: (l, 0, 0)),
        ],
        out_specs=[
            pl.BlockSpec((tm, attn_w), lambda i: (i, 0)),
            pl.BlockSpec((tm, kv_w), lambda i: (i, 0)),
        ],
        out_shape=[
            jax.ShapeDtypeStruct((nt, attn_w), BF16),
            jax.ShapeDtypeStruct((nt, kv_w), BF16),
        ],
        compiler_params=_params(("parallel",)),
        name="qk_prepare",
    )(main, cos_t, sin_t, q_gain, k_gain)


def _rope_tables(seq, pad_rows):
    rows = seq // GRID_W
    row = jnp.repeat(jnp.arange(rows), GRID_W)
    col = jnp.tile(jnp.arange(GRID_W), rows)
    half = HEAD_DIM // 4
    inv_freq = ROPE_BASE ** (-jnp.arange(half, dtype=F32) / half)
    ang_r = row.astype(F32)[:, None] * inv_freq
    ang_c = col.astype(F32)[:, None] * inv_freq
    cr, sr, cc, sc = jnp.cos(ang_r), jnp.sin(ang_r), jnp.cos(ang_c), jnp.sin(ang_c)
    cos_t = jnp.concatenate([cr, cr, cc, cc], axis=-1)
    sin_t = jnp.concatenate([-sr, sr, -sc, sc], axis=-1)
    cos_t = jnp.concatenate([cos_t, jnp.ones((pad_rows, HEAD_DIM), F32)], axis=0)
    sin_t = jnp.concatenate([sin_t, jnp.zeros((pad_rows, HEAD_DIM), F32)], axis=0)
    return cos_t, sin_t


def _attn_kernel(*refs, tq, tk, n_blocks, group):
    if n_blocks:
        q_ref, kc_ref, vc_ref, kx_ref, vx_ref = refs[:5]
        o_ref, sc_scr, vce_scr, sx_scr, vxe_scr, m_scr, acc_scr = refs[-7:]
    else:
        q_ref, kc_ref, vc_ref = refs[:3]
        o_ref, sc_scr, vce_scr, m_scr, acc_scr = refs[-5:]

    @pl.when(pl.program_id(2) == 0)
    def _():
        vce_scr[:, :HEAD_DIM] = vc_ref[...]
        vce_scr[:, HEAD_DIM:] = jnp.ones((vce_scr.shape[0], HEAD_DIM), BF16)
        if n_blocks:
            vxe_scr[:, :HEAD_DIM] = vx_ref[...]
            vxe_scr[:, HEAD_DIM:] = jnp.ones((vxe_scr.shape[0], HEAD_DIM), BF16)

    q = jnp.concatenate([q_ref[:, HEAD_DIM * g:HEAD_DIM * (g + 1)] for g in range(group)], axis=0)

    def scores(k):
        return lax.dot_general(q, k, (((1,), (1,)), ((), ())), preferred_element_type=F32)

    def slab_max(s):
        m = s[:, :LANES]
        for c in range(1, s.shape[1] // LANES):
            m = jnp.maximum(m, s[:, LANES * c:LANES * (c + 1)])
        return m

    def probs(s, m_rep):
        return jnp.concatenate(
            [jnp.exp2(s[:, LANES * c:LANES * (c + 1)] - m_rep).astype(BF16) for c in range(s.shape[1] // LANES)],
            axis=1)

    s = scores(kc_ref[...])
    sc_scr[...] = s
    m_scr[...] = slab_max(s)
    if n_blocks:
        def first(j, carry):
            s = scores(kx_ref[pl.ds(pl.multiple_of(j * tk, tk), tk), :])
            sx_scr[j] = s
            m_scr[...] = jnp.maximum(m_scr[...], slab_max(s))
            return carry

        lax.fori_loop(0, n_blocks, first, 0)
    m_scr[...] = jnp.broadcast_to(jnp.max(m_scr[...], axis=-1, keepdims=True), m_scr.shape)
    acc_scr[...] = jnp.dot(probs(sc_scr[...], m_scr[...]), vce_scr[...], preferred_element_type=F32)
    if n_blocks:
        def second(j, carry):
            acc_scr[...] += jnp.dot(probs(sx_scr[j], m_scr[...]),
                                    vxe_scr[pl.ds(pl.multiple_of(j * tk, tk), tk), :],
                                    preferred_element_type=F32)
            return carry

        lax.fori_loop(0, n_blocks, second, 0)
    acc = acc_scr[...]
    o = acc[:, :HEAD_DIM] / acc[:, HEAD_DIM:]
    for g in range(group):
        o_ref[:, HEAD_DIM * g:HEAD_DIM * (g + 1)] = o[tq * g:tq * (g + 1)].astype(o_ref.dtype)


def _attention(q, k, main, prev, dims):
    nt = q.shape[0]
    attn_w, seq, ctx, batch = dims["attn_w"], dims["seq"], dims["ctx"], dims["batch"]
    latent = prev is None
    group = attn_w // HEAD_DIM // N_KV_HEADS
    gw = group * HEAD_DIM
    v_blk = dims["o_v"] // HEAD_DIM
    lq = seq if latent else ctx
    tq = _pick([lq], [128, 64, 32, 16])
    tk = _pick([seq], [2048, 1024, 512, 256, 128])
    row0 = 0 if latent else dims["n_lat"] // tq
    ctx0 = dims["n_lat"] // ctx
    rows = group * tq
    in_specs = [
        pl.BlockSpec((tq, gw), lambda b, h, i: (row0 + b * (lq // tq) + i, h)),
        pl.BlockSpec((ctx, HEAD_DIM), lambda b, h, i: (ctx0 + b, h)),
        pl.BlockSpec((ctx, HEAD_DIM), lambda b, h, i: (ctx0 + b, v_blk + h)),
    ]
    args = [q, k, main]
    scratch = [pltpu.VMEM((rows, ctx), F32), pltpu.VMEM((ctx, 2 * HEAD_DIM), BF16)]
    aliases = {}
    if latent:
        in_specs += [
            pl.BlockSpec((seq, HEAD_DIM), lambda b, h, i: (b, h)),
            pl.BlockSpec((seq, HEAD_DIM), lambda b, h, i: (b, v_blk + h)),
        ]
        args += [k, main]
        scratch += [pltpu.VMEM((seq // tk, rows, tk), F32), pltpu.VMEM((seq, 2 * HEAD_DIM), BF16)]
    else:
        in_specs.append(pl.BlockSpec(memory_space=pl.ANY))
        args.append(prev)
        aliases = {len(args) - 1: 0}
    scratch += [pltpu.VMEM((rows, LANES), F32), pltpu.VMEM((rows, 2 * HEAD_DIM), F32)]
    return pl.pallas_call(
        functools.partial(_attn_kernel, tq=tq, tk=tk, n_blocks=seq // tk if latent else 0, group=group),
        grid=(batch, N_KV_HEADS, lq // tq),
        in_specs=in_specs,
        out_specs=pl.BlockSpec((tq, gw), lambda b, h, i: (row0 + b * (lq // tq) + i, h)),
        out_shape=jax.ShapeDtypeStruct((nt, attn_w), BF16),
        scratch_shapes=scratch,
        input_output_aliases=aliases,
        compiler_params=_params(("parallel", "parallel", "arbitrary")),
        name="attention_latent" if latent else "attention_context",
    )(*args)


def _ssm_params_kernel(lrr_ref, lir_ref, ldr_ref, lrc_ref, lic_ref, ldc_ref, btr_ref, bti_ref,
                       ctr_ref, cti_ref, r1_ref, v_ref, a_ref):
    t = SSM_CHUNK
    pw = SSM_GROUP * t
    r1_ref[...] = jnp.zeros(r1_ref.shape, r1_ref.dtype)
    v_ref[...] = jnp.zeros(v_ref.shape, v_ref.dtype)
    lane = lax.broadcasted_iota(jnp.int32, (SSM_GROUP, LANES), 1)
    col = lax.broadcasted_iota(jnp.int32, (SSM_GROUP, pw), 1)
    tau_col = (lax.broadcasted_iota(jnp.int32, (LANES, pw), 1) // SSM_GROUP).astype(F32)
    tau_row = lax.broadcasted_iota(jnp.int32, (t, LANES), 0).astype(F32)
    masks = (lane < SSM_STATE, lane >= SSM_STATE)
    toep = []
    for d in range(2):
        lr = jnp.minimum(lrr_ref[d], SSM_RE_MAX)
        li = lir_ref[d]
        dt = jnp.exp(ldr_ref[d])
        mag = jnp.exp(lr * dt)
        ab_re = mag * jnp.cos(li * dt)
        ab_im = mag * jnp.sin(li * dt)
        nr = ab_re - 1.0
        den = lr * lr + li * li
        f_re = (nr * lr + ab_im * li) / den
        f_im = (ab_im * lr - nr * li) / den
        bb_re = f_re * btr_ref[d] - f_im * bti_ref[d]
        bb_im = f_re * bti_ref[d] + f_im * btr_ref[d]
        e16 = jnp.exp(lr * dt * float(t))
        a_ref[2 * d:2 * d + 1, :] = e16 * jnp.cos(li * dt * float(t))
        a_ref[2 * d + 1:2 * d + 2, :] = e16 * jnp.sin(li * dt * float(t))
        tau_s = (t - 1.0) - tau_row if d == 0 else tau_row
        es = jnp.exp(lr * dt * tau_s)
        pw_re = es * jnp.cos(li * dt * tau_s)
        pw_im = es * jnp.sin(li * dt * tau_s)
        for s in range(t):
            ar = pw_re[s:s + 1]
            ai = pw_im[s:s + 1]
            w_re = ar * bb_re - ai * bb_im
            w_im = ar * bb_im + ai * bb_re
            for gi in range(2):
                r0 = gi * pw + SSM_GROUP * s
                c0 = 2 * pw + 2 * LANES * d
                r1_ref[r0:r0 + SSM_GROUP, c0:c0 + LANES] = jnp.where(masks[gi], w_re, 0.0).astype(r1_ref.dtype)
                r1_ref[r0:r0 + SSM_GROUP, c0 + LANES:c0 + 2 * LANES] = (
                    jnp.where(masks[gi], w_im, 0.0).astype(r1_ref.dtype))
        lrc = jnp.minimum(lrc_ref[d], SSM_RE_MAX)
        lic = lic_ref[d]
        dtc = jnp.exp(ldc_ref[d])
        tau_q = tau_col if d == 0 else (t - 1.0) - tau_col
        eq = jnp.exp(lrc * dtc * tau_q)
        p_re = eq * jnp.cos(lic * dtc * tau_q)
        p_im = eq * jnp.sin(lic * dtc * tau_q)
        c_re = ctr_ref[d]
        c_im = cti_ref[d]
        q_re = p_re * c_re - p_im * c_im
        q_im = p_re * c_im + p_im * c_re
        per_group = []
        for gi in range(2):
            k_mat = (jnp.dot(jnp.where(masks[gi], bb_re, 0.0), q_re, precision=lax.Precision.HIGHEST,
                             preferred_element_type=F32)
                     - jnp.dot(jnp.where(masks[gi], bb_im, 0.0), q_im, precision=lax.Precision.HIGHEST,
                               preferred_element_type=F32))
            per_group.append(k_mat)
        toep.append(per_group)
        magc = jnp.exp(lrc * dtc)
        abc_re = magc * jnp.cos(lic * dtc)
        abc_im = magc * jnp.sin(lic * dtc)
        q1_re = abc_re * q_re - abc_im * q_im
        q1_im = abc_re * q_im + abc_im * q_re
        for gi in range(2):
            rows = slice(SSM_STATE * gi, SSM_STATE * (gi + 1))
            r0 = 2 * LANES * d + SSM_STATE * gi
            v_ref[r0:r0 + SSM_STATE, pw * gi:pw * (gi + 1)] = q1_re[rows].astype(v_ref.dtype)
            v_ref[r0 + LANES:r0 + LANES + SSM_STATE, pw * gi:pw * (gi + 1)] = (-q1_im[rows]).astype(v_ref.dtype)
    for gi in range(2):
        k_f, k_b = toep[0][gi], toep[1][gi]
        for s in range(t):
            sh = SSM_GROUP * s
            f = k_f if sh == 0 else jnp.where(col >= sh, pltpu.roll(k_f, sh, 1), 0.0)
            back = SSM_GROUP * (t - 1 - s)
            b = k_b if back == 0 else jnp.where(col < pw - back, pltpu.roll(k_b, pw - back, 1), 0.0)
            r0 = gi * pw + sh
            r1_ref[r0:r0 + SSM_GROUP, pw * gi:pw * (gi + 1)] = (f + b).astype(r1_ref.dtype)


def _pair_rows(a):
    depth, _, groups, n = a.shape
    return a.reshape(depth, 2, groups // 2, 2 * n).transpose(0, 2, 1, 3)


def _ssm_parameters(lam_re, lam_im, log_dt, b_re, b_im, c_re, c_im):
    depth, _, groups, n = lam_re.shape
    pairs = groups // 2
    t = SSM_CHUNK
    pw = SSM_GROUP * t
    ld = jnp.broadcast_to(log_dt[..., None], lam_re.shape)
    rows = [_pair_rows(a.astype(F32))[:, :, :, None, :] for a in (lam_re, lam_im, ld)]
    cols = [_pair_rows(a.astype(F32))[:, :, :, :, None] for a in (lam_re, lam_im, ld)]

    def bt(b):
        return (b.astype(F32).reshape(depth, 2, pairs, 2, n, SSM_GROUP).transpose(0, 2, 1, 5, 3, 4)
                .reshape(depth, pairs, 2, SSM_GROUP, 2 * n))

    def ct(c):
        c = (c.astype(F32).reshape(depth, 2, pairs, 2, SSM_GROUP, n).transpose(0, 2, 1, 3, 5, 4)
             .reshape(depth, pairs, 2, 2 * n, SSM_GROUP))
        return jnp.tile(c, (1, 1, 1, 1, t))

    row_spec = pl.BlockSpec((None, None, 2, 1, 2 * n), lambda l, j: (l, j, 0, 0, 0))
    col_spec = pl.BlockSpec((None, None, 2, 2 * n, 1), lambda l, j: (l, j, 0, 0, 0))
    bt_spec = pl.BlockSpec((None, None, 2, SSM_GROUP, 2 * n), lambda l, j: (l, j, 0, 0, 0))
    ct_spec = pl.BlockSpec((None, None, 2, 2 * n, pw), lambda l, j: (l, j, 0, 0, 0))
    return pl.pallas_call(
        _ssm_params_kernel,
        grid=(depth, pairs),
        in_specs=[row_spec] * 3 + [col_spec] * 3 + [bt_spec] * 2 + [ct_spec] * 2,
        out_specs=[
            pl.BlockSpec((None, None, 2 * pw, 4 * pw), lambda l, j: (l, j, 0, 0)),
            pl.BlockSpec((None, None, 8 * n, 2 * pw), lambda l, j: (l, j, 0, 0)),
            pl.BlockSpec((None, 4, 2 * n), lambda l, j: (l, 0, j)),
        ],
        out_shape=[
            jax.ShapeDtypeStruct((depth, pairs, 2 * pw, 4 * pw), BF16),
            jax.ShapeDtypeStruct((depth, pairs, 8 * n, 2 * pw), BF16),
            jax.ShapeDtypeStruct((depth, 4, groups * n), F32),
        ],
        compiler_params=_params(("parallel", "parallel")),
        name="ssm_parameters",
    )(*rows, *cols, bt(b_re), bt(b_im), ct(c_re), ct(c_im))


def _chunk_rows(u_ref):
    return jnp.concatenate([u_ref[t] for t in range(u_ref.shape[0])], axis=-1)


def _ssm_state_kernel(u_ref, w_ref, s_ref):
    r = jnp.dot(_chunk_rows(u_ref), w_ref[...], preferred_element_type=F32)
    for k in range(4):
        s_ref[k] = r[:, LANES * k:LANES * (k + 1)]


def _ssm_chunk_states(u, r1, l):
    pairs, n_t, nc, _ = u.shape
    kw = n_t * LANES
    tc = _pick([nc], [2112, 1056, 704, 512, 256, 128, 64, 32, 16, 8])
    return pl.pallas_call(
        _ssm_state_kernel,
        grid=(pairs, nc // tc),
        in_specs=[
            pl.BlockSpec((None, n_t, tc, LANES), lambda j, i: (j, 0, i, 0)),
            pl.BlockSpec((None, None, kw, kw), lambda j, i: (l, j, 0, 1)),
        ],
        out_specs=pl.BlockSpec((4, tc, LANES), lambda j, i: (0, i, j)),
        out_shape=jax.ShapeDtypeStruct((4, nc, pairs * LANES), F32),
        compiler_params=_params(("parallel", "parallel")),
        name="ssm_chunk_states",
    )(u, r1)


def _ssm_scan_kernel(s_ref, a_ref, h_ref, *, batch, lat_tiles, ctx_tiles):
    w = s_ref.shape[-1]
    per_tile = SUBLANES // batch
    row_grp = lax.broadcasted_iota(jnp.int32, (SUBLANES, w), 0) // batch
    zero = jnp.zeros((SUBLANES, w), F32)

    def make_tile(k, order, shift):
        ar = a_ref[k:k + 1, :]
        ai = a_ref[k + 1:k + 2, :]

        def tile(ti, carry):
            cur_re, cur_im = carry
            rows = pl.ds(pl.multiple_of(ti * SUBLANES, SUBLANES), SUBLANES)
            s_re = s_ref[k, rows, :]
            s_im = s_ref[k + 1, rows, :]
            h_re, h_im = zero, zero
            for pos in order:
                sel = row_grp == pos
                h_re = jnp.where(sel, cur_re, h_re)
                h_im = jnp.where(sel, cur_im, h_im)
                nxt_re = ar * cur_re - ai * cur_im + s_re
                nxt_im = ar * cur_im + ai * cur_re + s_im
                if batch == SUBLANES:
                    cur_re, cur_im = nxt_re, nxt_im
                else:
                    cur_re = pltpu.roll(nxt_re, shift, 0)
                    cur_im = pltpu.roll(nxt_im, shift, 0)
            h_ref[k, rows, :] = h_re
            h_ref[k + 1, rows, :] = h_im
            return cur_re, cur_im

        return tile

    fwd = make_tile(0, range(per_tile), batch)
    carry = lax.fori_loop(0, ctx_tiles, lambda i, c: fwd(lat_tiles + i, c), (zero, zero))
    lax.fori_loop(0, lat_tiles, fwd, carry)
    bwd = make_tile(2, range(per_tile - 1, -1, -1), SUBLANES - batch)
    carry = lax.fori_loop(0, ctx_tiles, lambda i, c: bwd(lat_tiles + ctx_tiles - 1 - i, c), (zero, zero))
    lax.fori_loop(0, lat_tiles, lambda i, c: bwd(lat_tiles - 1 - i, c), carry)


def _ssm_scan(s, a16, l, dims):
    _, nc, width = s.shape
    batch = dims["batch"]
    lat_tiles = dims["seq"] // SSM_CHUNK * batch // SUBLANES
    ctx_tiles = dims["ctx"] // SSM_CHUNK * batch // SUBLANES
    return pl.pallas_call(
        functools.partial(_ssm_scan_kernel, batch=batch, lat_tiles=lat_tiles, ctx_tiles=ctx_tiles),
        grid=(width // LANES,),
        in_specs=[
            pl.BlockSpec((4, nc, LANES), lambda j: (0, 0, j)),
            pl.BlockSpec((None, 4, LANES), lambda j: (l, 0, j)),
        ],
        out_specs=pl.BlockSpec((4, nc, LANES), lambda j: (0, 0, j)),
        out_shape=jax.ShapeDtypeStruct((4, nc, width), F32),
        compiler_params=_params(("parallel",)),
        name="ssm_scan",
    )(s, a16)


def _ssm_out_kernel(u_ref, m_ref, h_ref, v_ref, d_ref, y_ref):
    u = _chunk_rows(u_ref)
    h = jnp.concatenate([h_ref[k] for k in range(4)], axis=-1).astype(BF16)
    y = jnp.dot(u, m_ref[...], preferred_element_type=F32)
    y = y + jnp.dot(h, v_ref[...], preferred_element_type=F32)
    y = y + d_ref[...] * u.astype(F32)
    for t in range(y_ref.shape[0]):
        y_ref[t] = y[:, LANES * t:LANES * (t + 1)]


def _ssm_outputs(u, r1, h, v, dsk, l):
    pairs, n_t, nc, _ = u.shape
    kw = n_t * LANES
    tc = _pick([nc], [2112, 1056, 704, 512, 256, 128, 64, 32, 16, 8])
    return pl.pallas_call(
        _ssm_out_kernel,
        grid=(pairs, nc // tc),
        in_specs=[
            pl.BlockSpec((None, n_t, tc, LANES), lambda j, i: (j, 0, i, 0)),
            pl.BlockSpec((None, None, kw, kw), lambda j, i: (l, j, 0, 0)),
            pl.BlockSpec((4, tc, LANES), lambda j, i: (0, i, j)),
            pl.BlockSpec((None, None, 4 * LANES, kw), lambda j, i: (l, j, 0, 0)),
            pl.BlockSpec((None, None, 1, kw), lambda j, i: (l, j, 0, 0)),
        ],
        out_specs=pl.BlockSpec((None, n_t, tc, LANES), lambda j, i: (j, 0, i, 0)),
        out_shape=jax.ShapeDtypeStruct((pairs, n_t, nc, LANES), F32),
        compiler_params=_params(("parallel", "parallel")),
        name="ssm_outputs",
    )(u, r1, h, v, dsk)


GROUPS_PER_TILE = LANES // SSM_GROUP


def _lane_slot(rows):
    return lax.broadcasted_iota(jnp.int32, (rows, LANES), 1) // SSM_GROUP


def _chunk_tiling(dims):
    ncl, ncc = dims["seq"] // SSM_CHUNK, dims["ctx"] // SSM_CHUNK
    ct = _pick([ncl, ncc], [16, 8, 4, 2, 1])
    nlt, nct = ncl // ct, ncc // ct

    def token_block(k, b):
        return jnp.where(k < nlt, b * nlt + k, dims["batch"] * nlt + b * nct + (k - nlt))

    return ct, nlt + nct, token_block


def _ssm_pack_kernel(*refs, batch, ct):
    x_refs, o_ref, scr = refs[:batch], refs[batch], refs[batch + 1]
    slot = _lane_slot(ct)
    halves = SSM_CHUNK // GROUPS_PER_TILE
    for b in range(batch):
        for j in range(x_refs[b].shape[0]):
            for half in range(halves):
                xs = [x_refs[b][j, pl.ds(GROUPS_PER_TILE * half + k, ct, stride=SSM_CHUNK), :]
                      for k in range(GROUPS_PER_TILE)]
                for gg in range(GROUPS_PER_TILE):
                    out = None
                    for k in range(GROUPS_PER_TILE):
                        sh = ((k - gg) % GROUPS_PER_TILE) * SSM_GROUP
                        r = xs[k] if sh == 0 else pltpu.roll(xs[k], sh, 1)
                        out = r if out is None else jnp.where(slot == k, r, out)
                    g = GROUPS_PER_TILE * j + gg
                    scr[g // 2, (g % 2) * halves + half, pl.ds(b, ct, stride=batch), :] = out
    o_ref[...] = scr[...].astype(o_ref.dtype)


def _ssm_pack(u32, dims):
    batch, pairs = dims["batch"], dims["pairs"]
    ct, n_tiles, token_block = _chunk_tiling(dims)
    n_in, nt, _ = u32.shape
    n_t = 2 * SSM_CHUNK * SSM_GROUP // LANES
    return pl.pallas_call(
        functools.partial(_ssm_pack_kernel, batch=batch, ct=ct),
        grid=(n_tiles,),
        in_specs=[pl.BlockSpec((n_in, ct * SSM_CHUNK, LANES),
                               functools.partial(lambda k, b: (0, token_block(k, b), 0), b=b))
                  for b in range(batch)],
        out_specs=pl.BlockSpec((pairs, n_t, ct * batch, LANES), lambda k: (0, 0, k, 0)),
        out_shape=jax.ShapeDtypeStruct((pairs, n_t, nt // SSM_CHUNK, LANES), BF16),
        scratch_shapes=[pltpu.VMEM((pairs, n_t, ct * batch, LANES), F32)],
        compiler_params=_params(("parallel",)),
        name="ssm_pack",
    )(*([u32] * batch))


def _ssm_unpack_kernel(y_ref, o_ref, *, batch, ct):
    b = pl.program_id(1)
    slot = _lane_slot(ct)
    halves = SSM_CHUNK // GROUPS_PER_TILE
    for j in range(o_ref.shape[0]):
        for half in range(halves):
            xs = []
            for gg in range(GROUPS_PER_TILE):
                g = GROUPS_PER_TILE * j + gg
                xs.append(y_ref[g // 2, (g % 2) * halves + half, pl.ds(b, ct, stride=batch), :])
            for k in range(GROUPS_PER_TILE):
                out = None
                for gg in range(GROUPS_PER_TILE):
                    sh = ((gg - k) % GROUPS_PER_TILE) * SSM_GROUP
                    r = xs[gg] if sh == 0 else pltpu.roll(xs[gg], sh, 1)
                    out = r if out is None else jnp.where(slot == gg, r, out)
                o_ref[j, pl.ds(GROUPS_PER_TILE * half + k, ct, stride=SSM_CHUNK), :] = out


def _ssm_unpack(y, dims):
    batch = dims["batch"]
    pairs, n_t, nc, _ = y.shape
    ct, n_tiles, token_block = _chunk_tiling(dims)
    n_out = dims["ssm_w"] // LANES
    return pl.pallas_call(
        functools.partial(_ssm_unpack_kernel, batch=batch, ct=ct),
        grid=(n_tiles, batch),
        in_specs=[pl.BlockSpec((pairs, n_t, ct * batch, LANES), lambda k, b: (0, 0, k, 0))],
        out_specs=pl.BlockSpec((n_out, ct * SSM_CHUNK, LANES), lambda k, b: (0, token_block(k, b), 0)),
        out_shape=jax.ShapeDtypeStruct((n_out, nc * SSM_CHUNK, LANES), F32),
        compiler_params=_params(("parallel", "arbitrary")),
        name="ssm_unpack",
    )(y)


def _merge_kernel(pb_ref, pc_ref, pv_ref, hpc_ref, hpv_ref, hnc_ref, hnv_ref, cw_ref, ys_ref, at_ref,
                  g_ref, wc_ref, wg_ref, wa_ref, o_ref, *, tm, tn, seq, ctx, n_lat):
    d = o_ref.shape[1]
    r0 = pl.program_id(0) * tm
    period = jnp.where(r0 >= n_lat, ctx, seq)
    starts = r0 % period == 0
    ends = (r0 + tm) % period == 0
    w = pc_ref[...].astype(F32) * pv_ref[...].astype(F32)
    prev = (hpc_ref[...].astype(F32) * hpv_ref[...].astype(F32))[HALO_ROWS - 1:HALO_ROWS]
    nxt = (hnc_ref[...].astype(F32) * hnv_ref[...].astype(F32))[0:1]
    prev = jnp.where(starts, 0.0, prev)
    nxt = jnp.where(ends, 0.0, nxt)
    row = lax.broadcasted_iota(jnp.int32, w.shape, 0)
    up = jnp.where(row == 0, prev, pltpu.roll(w, 1, 0))
    dn = jnp.where(row == tm - 1, nxt, pltpu.roll(w, tm - 1, 0))
    cw = cw_ref[...]
    cf = (pb_ref[...].astype(F32) * (cw[0:1] * up + cw[1:2] * w + cw[2:3] * dn)).astype(BF16)
    yg = jax.nn.gelu(_chunk_rows(ys_ref)).astype(BF16)
    at = at_ref[...]
    for c in range(d // tn):
        cols = slice(tn * c, tn * (c + 1))
        gate = lambda k: jax.nn.sigmoid(g_ref[:, k * d + tn * c:k * d + tn * (c + 1)].astype(F32))
        ya = jnp.dot(cf, wc_ref[:, cols], preferred_element_type=F32)
        yc = jnp.dot(at, wa_ref[:, cols], preferred_element_type=F32)
        ys = jnp.dot(yg, wg_ref[:, cols], preferred_element_type=F32) * jax.nn.sigmoid(
            jnp.dot(yg, wg_ref[:, d + tn * c:d + tn * (c + 1)], preferred_element_type=F32))
        o_ref[:, cols] = (gate(0) * ya + gate(1) * ys + gate(2) * yc).astype(o_ref.dtype)


def _merge(main, gates, ys, attn, conv_w, w_conv_out, w_glu, w_attn_out, l, dims, rows):
    d = dims["d"]
    cw_, sw, aw = dims["conv_w"], dims["ssm_w"], dims["attn_w"]
    tm = _pick([dims["seq"], dims["ctx"]], [512, 256, 128, 64, 32])
    tn = _pick([d], [512, 256, 128])
    n_halo = main.shape[0] // HALO_ROWS
    per = tm // HALO_ROWS
    prev_blk = lambda i: jnp.maximum(i * per - 1, 0)
    next_blk = lambda i: jnp.minimum((i + 1) * per, n_halo - 1)
    resident = lambda shape: pl.BlockSpec((None,) + shape, lambda i: (l, 0, 0), pipeline_mode=pl.Buffered(1))
    return pl.pallas_call(
        functools.partial(_merge_kernel, tm=tm, tn=tn, seq=dims["seq"], ctx=dims["ctx"], n_lat=dims["n_lat"]),
        grid=(rows // tm,),
        in_specs=[
            pl.BlockSpec((tm, cw_), lambda i: (i, 0)),
            pl.BlockSpec((tm, cw_), lambda i: (i, 1)),
            pl.BlockSpec((tm, cw_), lambda i: (i, 2)),
            pl.BlockSpec((HALO_ROWS, cw_), lambda i: (prev_blk(i), 1)),
            pl.BlockSpec((HALO_ROWS, cw_), lambda i: (prev_blk(i), 2)),
            pl.BlockSpec((HALO_ROWS, cw_), lambda i: (next_blk(i), 1)),
            pl.BlockSpec((HALO_ROWS, cw_), lambda i: (next_blk(i), 2)),
            resident((CONV_K, cw_)),
            pl.BlockSpec((sw // LANES, tm, LANES), lambda i: (0, i, 0)),
            pl.BlockSpec((tm, aw), lambda i: (i, 0)),
            pl.BlockSpec((tm, 3 * d), lambda i: (i, 0)),
            resident((cw_, d)),
            resident((sw, 2 * d)),
            resident((aw, d)),
        ],
        out_specs=pl.BlockSpec((tm, d), lambda i: (i, 0)),
        out_shape=jax.ShapeDtypeStruct((rows, d), BF16),
        compiler_params=_params(("parallel",)),
        name="branch_merge",
    )(main, main, main, main, main, main, main, conv_w, ys, attn, gates, w_conv_out, w_glu, w_attn_out)


def _outproj_kernel(m_ref, w_ref, x_ref, gate_ref, g_ref, o_ref):
    r = jnp.dot(m_ref[...], w_ref[...], preferred_element_type=F32)
    o_ref[...] = x_ref[...] + gate_ref[...] * _rms(r, g_ref[...])


def _out_projection(m, w_out, xt, modr, g_post, l, dims):
    rows, d = m.shape
    tm = _pick([dims["seq"], dims["n_ctx"]], [512, 256, 128])
    grp = lambda i: jnp.minimum((i * tm) // dims["seq"], dims["batch"])
    return pl.pallas_call(
        _outproj_kernel,
        grid=(rows // tm,),
        in_specs=[
            pl.BlockSpec((tm, d), lambda i: (i, 0)),
            pl.BlockSpec((None, d, d), lambda i: (l, 0, 0)),
            pl.BlockSpec((tm, d), lambda i: (i, 0)),
            pl.BlockSpec((None, None, 1, d), lambda i: (l, grp(i), 0, 2)),
            pl.BlockSpec((None, 1, d), lambda i: (l, 0, 0)),
        ],
        out_specs=pl.BlockSpec((tm, d), lambda i: (i, 0)),
        out_shape=jax.ShapeDtypeStruct(xt.shape, F32),
        input_output_aliases={2: 0},
        compiler_params=_params(("parallel",)),
        name="out_projection",
    )(m, w_out, xt, modr, g_post)


def _mlp_kernel(x_ref, gpre_ref, sh_ref, sc_ref, wu_ref, wd_ref, gate_ref, gpost_ref, o_ref, h_scr, acc_scr):
    j = pl.program_id(1)

    @pl.when(j == 0)
    def _():
        h = _rms(x_ref[...], gpre_ref[...]) * (1.0 + sc_ref[...]) + sh_ref[...]
        h_scr[...] = h.astype(BF16)
        acc_scr[...] = jnp.zeros(acc_scr.shape, F32)

    u = jnp.maximum(jnp.dot(h_scr[...], wu_ref[...], preferred_element_type=F32), 0.0)
    acc_scr[...] += jnp.dot((u * u).astype(BF16), wd_ref[...], preferred_element_type=F32)

    @pl.when(j == pl.num_programs(1) - 1)
    def _():
        o_ref[...] = x_ref[...] + gate_ref[...] * _rms(acc_scr[...], gpost_ref[...])


def _mlp(xt, g_pre, g_post, modr, w_up, w_down, l, dims, rows):
    d = xt.shape[1]
    d_ff = w_up.shape[-1]
    tm = _pick([dims["seq"], dims["n_ctx"]], [512, 256, 128])
    tf = _pick([d_ff], [512, 256, 128])
    grp = lambda i: jnp.minimum((i * tm) // dims["seq"], dims["batch"])
    mod = lambda k: pl.BlockSpec((None, None, 1, d), lambda i, j: (l, grp(i), 0, k))
    vec = pl.BlockSpec((None, 1, d), lambda i, j: (l, 0, 0))
    return pl.pallas_call(
        _mlp_kernel,
        grid=(rows // tm, d_ff // tf),
        in_specs=[
            pl.BlockSpec((tm, d), lambda i, j: (i, 0)),
            vec, mod(3), mod(4),
            pl.BlockSpec((None, d, tf), lambda i, j: (l, 0, j)),
            pl.BlockSpec((None, tf, d), lambda i, j: (l, j, 0)),
            mod(5), vec,
        ],
        out_specs=pl.BlockSpec((tm, d), lambda i, j: (i, 0)),
        out_shape=jax.ShapeDtypeStruct(xt.shape, F32),
        scratch_shapes=[pltpu.VMEM((tm, d), BF16), pltpu.VMEM((tm, d), F32)],
        input_output_aliases={0: 0},
        compiler_params=_params(("parallel", "arbitrary")),
        name="mlp",
    )(xt, g_pre, modr, modr, w_up, w_down, modr, g_post)


def kernel(x, c, ctx, c_ctx, w_mod, b_mod, g_pre_mix, g_post_mix, g_pre_mlp, g_post_mlp, w_in, conv_w,
           w_conv_out, ssm_lam_re, ssm_lam_im, ssm_log_dt, ssm_b_re, ssm_b_im, ssm_c_re, ssm_c_im, ssm_d,
           w_glu, q_gain, k_gain, w_attn_out, w_out, w_up, w_down):
    batch, seq, d = x.shape
    ctx_len = ctx.shape[1]
    depth = w_in.shape[0]
    in_w = w_in.shape[-1]
    conv_width = conv_w.shape[-1]
    ssm_w = ssm_d.shape[-1]
    attn_w = w_attn_out.shape[1]
    gate_w = 3 * d
    kv_w = (in_w - 3 * conv_width - ssm_w - attn_w - gate_w) // 2
    groups = ssm_lam_re.shape[2]
    assert batch + 1 <= MOD_ROWS and SUBLANES % batch == 0
    assert kv_w == N_KV_HEADS * HEAD_DIM and ssm_lam_re.shape[3] == SSM_STATE
    assert ssm_w == groups * SSM_GROUP and groups % 2 == 0
    assert seq % GRID_W == 0 and (batch * seq) % ctx_len == 0 and ctx_len % LANES == 0
    dims = dict(
        batch=batch, seq=seq, ctx=ctx_len, d=d, n_lat=batch * seq, n_ctx=batch * ctx_len,
        conv_w=conv_width, ssm_w=ssm_w, attn_w=attn_w, kv_w=kv_w, gate_w=gate_w, pairs=groups // 2,
        o_u=3 * conv_width, o_q=3 * conv_width + ssm_w, o_k=3 * conv_width + ssm_w + attn_w,
        o_v=3 * conv_width + ssm_w + attn_w + kv_w, main_w=in_w - gate_w,
    )
    n_lat, nt = dims["n_lat"], dims["n_lat"] + dims["n_ctx"]

    xt = jnp.concatenate([x.reshape(n_lat, d), ctx.reshape(dims["n_ctx"], d)], axis=0)
    cc = jnp.concatenate([c, c_ctx[None, :], jnp.zeros((MOD_ROWS - batch - 1, d), F32)], axis=0)
    mod = _modulation(cc, w_mod, b_mod)
    modr = mod.reshape(depth, MOD_ROWS, 1, N_MOD * d)

    row_vec = lambda a: a.reshape(depth, 1, a.shape[-1])
    g_pre_mix, g_post_mix, g_pre_mlp, g_post_mlp = map(row_vec, (g_pre_mix, g_post_mix, g_pre_mlp, g_post_mlp))
    q_gain, k_gain = row_vec(q_gain), row_vec(k_gain)
    w_in, w_conv_out, w_glu, w_attn_out, w_out, w_up, w_down = (
        w.astype(BF16) for w in (w_in, w_conv_out, w_glu, w_attn_out, w_out, w_up, w_down))

    qk_tile = _pick([seq, ctx_len], [256, 128, 64, 32])
    cos_t, sin_t = _rope_tables(seq, qk_tile)
    r1, v_mat, a16 = _ssm_parameters(ssm_lam_re, ssm_lam_im, ssm_log_dt, ssm_b_re, ssm_b_im, ssm_c_re, ssm_c_im)
    dsk = jnp.broadcast_to(
        ssm_d.astype(F32).reshape(depth, groups // 2, 2, 1, SSM_GROUP),
        (depth, groups // 2, 2, SSM_CHUNK, SSM_GROUP)).reshape(depth, groups // 2, 1, 2 * SSM_CHUNK * SSM_GROUP)

    for l in range(depth):
        need_ctx = l < depth - 1
        rows = nt if need_ctx else n_lat
        main, gates, u32 = _in_projection(xt, g_pre_mix, modr, w_in, l, dims)
        u = _ssm_pack(u32, dims)
        s = _ssm_chunk_states(u, r1, l)
        h = _ssm_scan(s, a16, l, dims)
        ys = _ssm_unpack(_ssm_outputs(u, r1, h, v_mat, dsk, l), dims)
        q, k = _qk_prepare(main, cos_t, sin_t, q_gain, k_gain, l, dims)
        attn = _attention(q, k, main, None, dims)
        if need_ctx:
            attn = _attention(q, k, main, attn, dims)
        m = _merge(main, gates, ys, attn, conv_w, w_conv_out, w_glu, w_attn_out, l, dims, rows)
        xt = _out_projection(m, w_out, xt, modr, g_post_mix, l, dims)
        xt = _mlp(xt, g_pre_mlp, g_post_mlp, modr, w_up, w_down, l, dims, rows)
    return xt[:n_lat].reshape(batch, seq, d)
```

```python
import functools
import math

import jax
import jax.numpy as jnp
from jax import lax
from jax.experimental import pallas as pl
from jax.experimental.pallas import tpu as pltpu

F32 = jnp.float32
BF16 = jnp.bfloat16

NORM_EPS = 1e-6
N_MOD = 6
HEAD_DIM = 128
N_KV_HEADS = 2
GRID_W = 64
ROPE_BASE = 10000.0
CONV_K = 3
SSM_GROUP = 16
SSM_STATE = 64
SSM_RE_MAX = -1e-4
SSM_CHUNK = 16
SUBLANES = 8
LANES = 128
HALO_ROWS = 16
MAX_KEY_CHUNK = 4224
MOD_ROWS = 8
VMEM_LIMIT = 56 * 2**20


def _params(sem, vmem=VMEM_LIMIT):
    return pltpu.CompilerParams(dimension_semantics=sem, vmem_limit_bytes=vmem)


def _pick(dims, cands):
    for c in cands:
        if all(d % c == 0 for d in dims):
            return c
    raise ValueError(f"no tile in {cands} divides {dims}")


def _rms(x, g):
    ms = jnp.mean(x * x, axis=-1, keepdims=True)
    return x * lax.rsqrt(ms + NORM_EPS) * g


def _mod_kernel(c_ref, w_ref, b_ref, o_ref):
    c = c_ref[...]
    s = (c * jax.nn.sigmoid(c)).astype(BF16)
    o_ref[...] = jnp.dot(s, w_ref[...].astype(BF16), preferred_element_type=F32) + b_ref[...]


def _modulation(cc, w_mod, b_mod):
    depth, d, nout = w_mod.shape
    tn = _pick([nout], [1024, 512, 256, 128])
    return pl.pallas_call(
        _mod_kernel,
        grid=(depth, nout // tn),
        in_specs=[
            pl.BlockSpec((MOD_ROWS, d), lambda l, j: (0, 0)),
            pl.BlockSpec((None, d, tn), lambda l, j: (l, 0, j)),
            pl.BlockSpec((None, 1, tn), lambda l, j: (l, 0, j)),
        ],
        out_specs=pl.BlockSpec((None, MOD_ROWS, tn), lambda l, j: (l, 0, j)),
        out_shape=jax.ShapeDtypeStruct((depth, MOD_ROWS, nout), F32),
        compiler_params=_params(("parallel", "parallel")),
        name="modulation",
    )(cc, w_mod, b_mod.reshape(depth, 1, nout))


def _inproj_kernel(x_ref, g_ref, sh_ref, sc_ref, w_ref, om_ref, og_ref, ou_ref, h_scr, *, n_main, j_ssm):
    j = pl.program_id(1)

    @pl.when(j == 0)
    def _():
        h = _rms(x_ref[...], g_ref[...]) * (1.0 + sc_ref[...]) + sh_ref[...]
        h_scr[...] = h.astype(BF16)

    r = jnp.dot(h_scr[...], w_ref[...], preferred_element_type=F32)

    @pl.when(j < n_main)
    def _():
        om_ref[...] = r.astype(om_ref.dtype)

    @pl.when(j >= n_main)
    def _():
        og_ref[...] = r.astype(og_ref.dtype)

    @pl.when(j == j_ssm)
    def _():
        for t in range(ou_ref.shape[0]):
            ou_ref[t] = r[:, LANES * t:LANES * (t + 1)]


def _in_projection(xt, gain, modr, w_in, l, dims):
    nt, d = xt.shape
    main_w, gate_w, tn = dims["main_w"], dims["gate_w"], dims["ssm_w"]
    tm = _pick([dims["seq"], dims["n_ctx"]], [1024, 512, 256, 128])
    assert main_w % tn == 0 and gate_w % tn == 0 and dims["o_u"] % tn == 0
    n_main = main_w // tn
    grp = lambda i: jnp.minimum((i * tm) // dims["seq"], dims["batch"])
    return pl.pallas_call(
        functools.partial(_inproj_kernel, n_main=n_main, j_ssm=dims["o_u"] // tn),
        grid=(nt // tm, (main_w + gate_w) // tn),
        in_specs=[
            pl.BlockSpec((tm, d), lambda i, j: (i, 0)),
            pl.BlockSpec((None, 1, d), lambda i, j: (l, 0, 0)),
            pl.BlockSpec((None, None, 1, d), lambda i, j: (l, grp(i), 0, 0)),
            pl.BlockSpec((None, None, 1, d), lambda i, j: (l, grp(i), 0, 1)),
            pl.BlockSpec((None, d, tn), lambda i, j: (l, 0, j)),
        ],
        out_specs=[
            pl.BlockSpec((tm, tn), lambda i, j: (i, jnp.minimum(j, n_main - 1))),
            pl.BlockSpec((tm, tn), lambda i, j: (i, jnp.maximum(j - n_main, 0))),
            pl.BlockSpec((tn // LANES, tm, LANES), lambda i, j: (0, i, 0)),
        ],
        out_shape=[
            jax.ShapeDtypeStruct((nt, main_w), BF16),
            jax.ShapeDtypeStruct((nt, gate_w), BF16),
            jax.ShapeDtypeStruct((tn // LANES, nt, LANES), F32),
        ],
        scratch_shapes=[pltpu.VMEM((tm, d), BF16)],
        compiler_params=_params(("parallel", "arbitrary")),
        name="in_projection",
    )(xt, gain, modr, modr, w_in)


def _qk_kernel(x_ref, cos_ref, sin_ref, qg_ref, kg_ref, q_ref, k_ref, *, n_q, n_kv, scale):
    cos = cos_ref[...]
    sin = sin_ref[...]
    lane = lax.broadcasted_iota(jnp.int32, cos.shape, 1)
    first_half = (lane % (HEAD_DIM // 2)) < (HEAD_DIM // 4)
    for h in range(n_q + n_kv):
        xh = x_ref[:, HEAD_DIM * h:HEAD_DIM * (h + 1)].astype(F32)
        gain = qg_ref[...] * scale if h < n_q else kg_ref[...]
        y = _rms(xh, gain)
        partner = jnp.where(first_half, pltpu.roll(y, HEAD_DIM - HEAD_DIM // 4, 1),
                            pltpu.roll(y, HEAD_DIM // 4, 1))
        r = (y * cos + partner * sin).astype(BF16)
        if h < n_q:
            q_ref[:, HEAD_DIM * h:HEAD_DIM * (h + 1)] = r
        else:
            k_ref[:, HEAD_DIM * (h - n_q):HEAD_DIM * (h - n_q + 1)] = r


def _qk_prepare(main, cos_t, sin_t, q_gain, k_gain, l, dims):
    nt = main.shape[0]
    attn_w, kv_w = dims["attn_w"], dims["kv_w"]
    tm = _pick([dims["seq"], dims["ctx"]], [256, 128, 64, 32])
    qk_w = attn_w + kv_w
    assert dims["o_q"] % qk_w == 0
    n_lat = dims["n_lat"] // tm
    n_seq = dims["seq"] // tm
    tab = lambda i: (jnp.where(i < n_lat, i % n_seq, n_seq), 0)
    return pl.pallas_call(
        functools.partial(_qk_kernel, n_q=attn_w // HEAD_DIM, n_kv=kv_w // HEAD_DIM,
                          scale=HEAD_DIM ** -0.5 * math.log2(math.e)),
        grid=(nt // tm,),
        in_specs=[
            pl.BlockSpec((tm, qk_w), lambda i: (i, dims["o_q"] // qk_w)),
            pl.BlockSpec((tm, HEAD_DIM), tab),
            pl.BlockSpec((tm, HEAD_DIM), tab),
            pl.BlockSpec((None, 1, HEAD_DIM), lambda i: (l, 0, 0)),
            pl.BlockSpec((None, 1, HEAD_DIM), lambda i: (l, 0, 0)),
        ],
        out_specs=[
            pl.BlockSpec((tm, attn_w), lambda i: (i, 0)),
            pl.BlockSpec((tm, kv_w), lambda i: (i, 0)),
        ],
        out_shape=[
            jax.ShapeDtypeStruct((nt, attn_w), BF16),
            jax.ShapeDtypeStruct((nt, kv_w), BF16),
        ],
        compiler_params=_params(("parallel",)),
        name="qk_prepare",
    )(main, cos_t, sin_t, q_gain, k_gain)


def _rope_tables(seq, pad_rows):
    rows = seq // GRID_W
    row = jnp.repeat(jnp.arange(rows), GRID_W)
    col = jnp.tile(jnp.arange(GRID_W), rows)
    half = HEAD_DIM // 4
    inv_freq = ROPE_BASE ** (-jnp.arange(half, dtype=F32) / half)
    ang_r = row.astype(F32)[:, None] * inv_freq
    ang_c = col.astype(F32)[:, None] * inv_freq
    cr, sr, cc, sc = jnp.cos(ang_r), jnp.sin(ang_r), jnp.cos(ang_c), jnp.sin(ang_c)
    cos_t = jnp.concatenate([cr, cr, cc, cc], axis=-1)
    sin_t = jnp.concatenate([-sr, sr, -sc, sc], axis=-1)
    cos_t = jnp.concatenate([cos_t, jnp.ones((pad_rows, HEAD_DIM), F32)], axis=0)
    sin_t = jnp.concatenate([sin_t, jnp.zeros((pad_rows, HEAD_DIM), F32)], axis=0)
    return cos_t, sin_t


def _attn_kernel(*refs, tq, tk, group, n_src):
    q_ref = refs[0]
    k_refs, v_refs = refs[1:1 + 2 * n_src:2], refs[2:2 + 2 * n_src:2]
    o_ref, ke_scr, ve_scr, s_scr, m_scr, acc_scr = refs[-6:]

    @pl.when(pl.program_id(2) == 0)
    def _():
        r0 = 0
        for k_ref, v_ref in zip(k_refs, v_refs):
            n = k_ref.shape[0]
            ke_scr[r0:r0 + n, :] = k_ref[...]
            ve_scr[r0:r0 + n, :HEAD_DIM] = v_ref[...]
            r0 += n
        ve_scr[:, HEAD_DIM:] = jnp.ones((ve_scr.shape[0], HEAD_DIM), BF16)

    q = jnp.concatenate([q_ref[:, HEAD_DIM * g:HEAD_DIM * (g + 1)] for g in range(group)], axis=0)
    n_chunks = s_scr.shape[0]

    def first(c, carry):
        keys = ke_scr[pl.ds(pl.multiple_of(c * tk, tk), tk), :]
        s = lax.dot_general(q, keys, (((1,), (1,)), ((), ())), preferred_element_type=F32)
        s_scr[c] = s
        m = m_scr[...]
        for t in range(tk // LANES):
            m = jnp.maximum(m, s[:, LANES * t:LANES * (t + 1)])
        m_scr[...] = m
        return carry

    m_scr[...] = jnp.full(m_scr.shape, -jnp.inf, F32)
    lax.fori_loop(0, n_chunks, first, 0)
    m_scr[...] = jnp.broadcast_to(jnp.max(m_scr[...], axis=-1, keepdims=True), m_scr.shape)

    def second(c, carry):
        s = s_scr[c]
        m_rep = m_scr[...]
        p = jnp.concatenate(
            [jnp.exp2(s[:, LANES * t:LANES * (t + 1)] - m_rep).astype(BF16) for t in range(tk // LANES)], axis=1)
        acc_scr[...] += jnp.dot(p, ve_scr[pl.ds(pl.multiple_of(c * tk, tk), tk), :], preferred_element_type=F32)
        return carry

    acc_scr[...] = jnp.zeros(acc_scr.shape, F32)
    lax.fori_loop(0, n_chunks, second, 0)
    acc = acc_scr[...]
    o = acc[:, :HEAD_DIM] / acc[:, HEAD_DIM:]
    for g in range(group):
        o_ref[:, HEAD_DIM * g:HEAD_DIM * (g + 1)] = o[tq * g:tq * (g + 1)].astype(o_ref.dtype)


def _attention(q, k, main, prev, dims):
    nt = q.shape[0]
    attn_w, seq, ctx, batch = dims["attn_w"], dims["seq"], dims["ctx"], dims["batch"]
    latent = prev is None
    group = attn_w // HEAD_DIM // N_KV_HEADS
    gw = group * HEAD_DIM
    v_blk = dims["o_v"] // HEAD_DIM
    lq = seq if latent else ctx
    tq = _pick([lq], [128, 64, 32, 16])
    n_keys = ctx + seq if latent else ctx
    tk = max(c for c in range(LANES, min(n_keys, MAX_KEY_CHUNK) + 1, LANES) if n_keys % c == 0)
    row0 = 0 if latent else dims["n_lat"] // tq
    ctx0 = dims["n_lat"] // ctx
    rows = group * tq
    in_specs = [
        pl.BlockSpec((tq, gw), lambda b, h, i: (row0 + b * (lq // tq) + i, h)),
        pl.BlockSpec((ctx, HEAD_DIM), lambda b, h, i: (ctx0 + b, h)),
        pl.BlockSpec((ctx, HEAD_DIM), lambda b, h, i: (ctx0 + b, v_blk + h)),
    ]
    args = [q, k, main]
    aliases = {}
    if latent:
        in_specs += [
            pl.BlockSpec((seq, HEAD_DIM), lambda b, h, i: (b, h)),
            pl.BlockSpec((seq, HEAD_DIM), lambda b, h, i: (b, v_blk + h)),
        ]
        args += [k, main]
    n_src = (len(args) - 1) // 2
    if not latent:
        in_specs.append(pl.BlockSpec(memory_space=pl.ANY))
        args.append(prev)
        aliases = {len(args) - 1: 0}
    scratch = [pltpu.VMEM((n_keys, HEAD_DIM), BF16), pltpu.VMEM((n_keys, 2 * HEAD_DIM), BF16),
               pltpu.VMEM((n_keys // tk, rows, tk), F32), pltpu.VMEM((rows, LANES), F32),
               pltpu.VMEM((rows, 2 * HEAD_DIM), F32)]
    return pl.pallas_call(
        functools.partial(_attn_kernel, tq=tq, tk=tk, group=group, n_src=n_src),
        grid=(batch, N_KV_HEADS, lq // tq),
        in_specs=in_specs,
        out_specs=pl.BlockSpec((tq, gw), lambda b, h, i: (row0 + b * (lq // tq) + i, h)),
        out_shape=jax.ShapeDtypeStruct((nt, attn_w), BF16),
        scratch_shapes=scratch,
        input_output_aliases=aliases,
        compiler_params=_params(("parallel", "parallel", "arbitrary")),
        name="attention_latent" if latent else "attention_context",
    )(*args)


def _ssm_params_kernel(lrr_ref, lir_ref, ldr_ref, lrc_ref, lic_ref, ldc_ref, btr_ref, bti_ref,
                       ctr_ref, cti_ref, r1_ref, v_ref, a_ref):
    t = SSM_CHUNK
    pw = SSM_GROUP * t
    r1_ref[...] = jnp.zeros(r1_ref.shape, r1_ref.dtype)
    v_ref[...] = jnp.zeros(v_ref.shape, v_ref.dtype)
    lane = lax.broadcasted_iota(jnp.int32, (SSM_GROUP, LANES), 1)
    col = lax.broadcasted_iota(jnp.int32, (SSM_GROUP, pw), 1)
    tau_col = (lax.broadcasted_iota(jnp.int32, (LANES, pw), 1) // SSM_GROUP).astype(F32)
    tau_row = lax.broadcasted_iota(jnp.int32, (t, LANES), 0).astype(F32)
    masks = (lane < SSM_STATE, lane >= SSM_STATE)
    toep = []
    for d in range(2):
        lr = jnp.minimum(lrr_ref[d], SSM_RE_MAX)
        li = lir_ref[d]
        dt = jnp.exp(ldr_ref[d])
        mag = jnp.exp(lr * dt)
        ab_re = mag * jnp.cos(li * dt)
        ab_im = mag * jnp.sin(li * dt)
        nr = ab_re - 1.0
        den = lr * lr + li * li
        f_re = (nr * lr + ab_im * li) / den
        f_im = (ab_im * lr - nr * li) / den
        bb_re = f_re * btr_ref[d] - f_im * bti_ref[d]
        bb_im = f_re * bti_ref[d] + f_im * btr_ref[d]
        e16 = jnp.exp(lr * dt * float(t))
        a_ref[2 * d:2 * d + 1, :] = e16 * jnp.cos(li * dt * float(t))
        a_ref[2 * d + 1:2 * d + 2, :] = e16 * jnp.sin(li * dt * float(t))
        tau_s = (t - 1.0) - tau_row if d == 0 else tau_row
        es = jnp.exp(lr * dt * tau_s)
        pw_re = es * jnp.cos(li * dt * tau_s)
        pw_im = es * jnp.sin(li * dt * tau_s)
        for s in range(t):
            ar = pw_re[s:s + 1]
            ai = pw_im[s:s + 1]
            w_re = ar * bb_re - ai * bb_im
            w_im = ar * bb_im + ai * bb_re
            for gi in range(2):
                r0 = gi * pw + SSM_GROUP * s
                c0 = 2 * pw + 2 * LANES * d
                r1_ref[r0:r0 + SSM_GROUP, c0:c0 + LANES] = jnp.where(masks[gi], w_re, 0.0).astype(r1_ref.dtype)
                r1_ref[r0:r0 + SSM_GROUP, c0 + LANES:c0 + 2 * LANES] = (
                    jnp.where(masks[gi], w_im, 0.0).astype(r1_ref.dtype))
        lrc = jnp.minimum(lrc_ref[d], SSM_RE_MAX)
        lic = lic_ref[d]
        dtc = jnp.exp(ldc_ref[d])
        tau_q = tau_col if d == 0 else (t - 1.0) - tau_col
        eq = jnp.exp(lrc * dtc * tau_q)
        p_re = eq * jnp.cos(lic * dtc * tau_q)
        p_im = eq * jnp.sin(lic * dtc * tau_q)
        c_re = ctr_ref[d]
        c_im = cti_ref[d]
        q_re = p_re * c_re - p_im * c_im
        q_im = p_re * c_im + p_im * c_re
        per_group = []
        for gi in range(2):
            k_mat = (jnp.dot(jnp.where(masks[gi], bb_re, 0.0), q_re, precision=lax.Precision.HIGHEST,
                             preferred_element_type=F32)
                     - jnp.dot(jnp.where(masks[gi], bb_im, 0.0), q_im, precision=lax.Precision.HIGHEST,
                               preferred_element_type=F32))
            per_group.append(k_mat)
        toep.append(per_group)
        magc = jnp.exp(lrc * dtc)
        abc_re = magc * jnp.cos(lic * dtc)
        abc_im = magc * jnp.sin(lic * dtc)
        q1_re = abc_re * q_re - abc_im * q_im
        q1_im = abc_re * q_im + abc_im * q_re
        for gi in range(2):
            rows = slice(SSM_STATE * gi, SSM_STATE * (gi + 1))
            r0 = 2 * LANES * d + SSM_STATE * gi
            v_ref[r0:r0 + SSM_STATE, pw * gi:pw * (gi + 1)] = q1_re[rows].astype(v_ref.dtype)
            v_ref[r0 + LANES:r0 + LANES + SSM_STATE, pw * gi:pw * (gi + 1)] = (-q1_im[rows]).astype(v_ref.dtype)
    for gi in range(2):
        k_f, k_b = toep[0][gi], toep[1][gi]
        for s in range(t):
            sh = SSM_GROUP * s
            f = k_f if sh == 0 else jnp.where(col >= sh, pltpu.roll(k_f, sh, 1), 0.0)
            back = SSM_GROUP * (t - 1 - s)
            b = k_b if back == 0 else jnp.where(col < pw - back, pltpu.roll(k_b, pw - back, 1), 0.0)
            r0 = gi * pw + sh
            r1_ref[r0:r0 + SSM_GROUP, pw * gi:pw * (gi + 1)] = (f + b).astype(r1_ref.dtype)


def _pair_rows(a):
    depth, _, groups, n = a.shape
    return a.reshape(depth, 2, groups // 2, 2 * n).transpose(0, 2, 1, 3)


def _ssm_parameters(lam_re, lam_im, log_dt, b_re, b_im, c_re, c_im):
    depth, _, groups, n = lam_re.shape
    pairs = groups // 2
    t = SSM_CHUNK
    pw = SSM_GROUP * t
    ld = jnp.broadcast_to(log_dt[..., None], lam_re.shape)
    rows = [_pair_rows(a.astype(F32))[:, :, :, None, :] for a in (lam_re, lam_im, ld)]
    cols = [_pair_rows(a.astype(F32))[:, :, :, :, None] for a in (lam_re, lam_im, ld)]

    def bt(b):
        return (b.astype(F32).reshape(depth, 2, pairs, 2, n, SSM_GROUP).transpose(0, 2, 1, 5, 3, 4)
                .reshape(depth, pairs, 2, SSM_GROUP, 2 * n))

    def ct(c):
        c = (c.astype(F32).reshape(depth, 2, pairs, 2, SSM_GROUP, n).transpose(0, 2, 1, 3, 5, 4)
             .reshape(depth, pairs, 2, 2 * n, SSM_GROUP))
        return jnp.tile(c, (1, 1, 1, 1, t))

    row_spec = pl.BlockSpec((None, None, 2, 1, 2 * n), lambda l, j: (l, j, 0, 0, 0))
    col_spec = pl.BlockSpec((None, None, 2, 2 * n, 1), lambda l, j: (l, j, 0, 0, 0))
    bt_spec = pl.BlockSpec((None, None, 2, SSM_GROUP, 2 * n), lambda l, j: (l, j, 0, 0, 0))
    ct_spec = pl.BlockSpec((None, None, 2, 2 * n, pw), lambda l, j: (l, j, 0, 0, 0))
    return pl.pallas_call(
        _ssm_params_kernel,
        grid=(depth, pairs),
        in_specs=[row_spec] * 3 + [col_spec] * 3 + [bt_spec] * 2 + [ct_spec] * 2,
        out_specs=[
            pl.BlockSpec((None, None, 2 * pw, 4 * pw), lambda l, j: (l, j, 0, 0)),
            pl.BlockSpec((None, None, 8 * n, 2 * pw), lambda l, j: (l, j, 0, 0)),
            pl.BlockSpec((None, 4, 2 * n), lambda l, j: (l, 0, j)),
        ],
        out_shape=[
            jax.ShapeDtypeStruct((depth, pairs, 2 * pw, 4 * pw), BF16),
            jax.ShapeDtypeStruct((depth, pairs, 8 * n, 2 * pw), BF16),
            jax.ShapeDtypeStruct((depth, 4, groups * n), F32),
        ],
        compiler_params=_params(("parallel", "parallel")),
        name="ssm_parameters",
    )(*rows, *cols, bt(b_re), bt(b_im), ct(c_re), ct(c_im))


def _chunk_rows(u_ref):
    return jnp.concatenate([u_ref[t] for t in range(u_ref.shape[0])], axis=-1)


def _scan_chunk_states(s_ref, a_ref, h_ref, *, batch, lat_tiles, ctx_tiles):
    w = s_ref.shape[-1]
    per_tile = SUBLANES // batch
    row_grp = lax.broadcasted_iota(jnp.int32, (SUBLANES, w), 0) // batch
    zero = jnp.zeros((SUBLANES, w), F32)

    def make_tile(k, order, shift):
        ar = a_ref[k:k + 1, :]
        ai = a_ref[k + 1:k + 2, :]

        def tile(ti, carry):
            cur_re, cur_im = carry
            rows = pl.ds(pl.multiple_of(ti * SUBLANES, SUBLANES), SUBLANES)
            s_re = s_ref[k, rows, :]
            s_im = s_ref[k + 1, rows, :]
            h_re, h_im = zero, zero
            for pos in order:
                sel = row_grp == pos
                h_re = jnp.where(sel, cur_re, h_re)
                h_im = jnp.where(sel, cur_im, h_im)
                nxt_re = ar * cur_re - ai * cur_im + s_re
                nxt_im = ar * cur_im + ai * cur_re + s_im
                if batch == SUBLANES:
                    cur_re, cur_im = nxt_re, nxt_im
                else:
                    cur_re = pltpu.roll(nxt_re, shift, 0)
                    cur_im = pltpu.roll(nxt_im, shift, 0)
            h_ref[k, rows, :] = h_re
            h_ref[k + 1, rows, :] = h_im
            return cur_re, cur_im

        return tile

    fwd = make_tile(0, range(per_tile), batch)
    carry = lax.fori_loop(0, ctx_tiles, lambda i, c: fwd(lat_tiles + i, c), (zero, zero))
    lax.fori_loop(0, lat_tiles, fwd, carry)
    bwd = make_tile(2, range(per_tile - 1, -1, -1), SUBLANES - batch)
    carry = lax.fori_loop(0, ctx_tiles, lambda i, c: bwd(lat_tiles + ctx_tiles - 1 - i, c), (zero, zero))
    lax.fori_loop(0, lat_tiles, lambda i, c: bwd(lat_tiles - 1 - i, c), carry)


def _ssm_core_kernel(u_ref, r1_ref, v_ref, a_ref, d_ref, y_ref, s_scr, h_scr, *, batch, lat_tiles, ctx_tiles):
    n_t = u_ref.shape[0]
    kw = n_t * LANES
    u = _chunk_rows(u_ref)
    r = jnp.dot(u, r1_ref[...], preferred_element_type=F32)
    for t in range(n_t):
        y_ref[t] = r[:, LANES * t:LANES * (t + 1)]
    for k in range(4):
        s_scr[k] = r[:, kw + LANES * k:kw + LANES * (k + 1)]
    _scan_chunk_states(s_scr, a_ref, h_scr, batch=batch, lat_tiles=lat_tiles, ctx_tiles=ctx_tiles)
    h = jnp.concatenate([h_scr[k] for k in range(4)], axis=-1).astype(BF16)
    y = jnp.dot(h, v_ref[...], preferred_element_type=F32) + d_ref[...] * u.astype(F32)
    for t in range(n_t):
        y_ref[t] += y[:, LANES * t:LANES * (t + 1)]


def _ssm_core(u, r1, v, a16, dsk, l, dims):
    pairs, n_t, nc, _ = u.shape
    kw = n_t * LANES
    batch = dims["batch"]
    lat_tiles = dims["seq"] // SSM_CHUNK * batch // SUBLANES
    ctx_tiles = dims["ctx"] // SSM_CHUNK * batch // SUBLANES
    return pl.pallas_call(
        functools.partial(_ssm_core_kernel, batch=batch, lat_tiles=lat_tiles, ctx_tiles=ctx_tiles),
        grid=(pairs,),
        in_specs=[
            pl.BlockSpec((None, n_t, nc, LANES), lambda j: (j, 0, 0, 0)),
            pl.BlockSpec((None, None, kw, 2 * kw), lambda j: (l, j, 0, 0)),
            pl.BlockSpec((None, None, 4 * LANES, kw), lambda j: (l, j, 0, 0)),
            pl.BlockSpec((None, 4, LANES), lambda j: (l, 0, j)),
            pl.BlockSpec((None, None, 1, kw), lambda j: (l, j, 0, 0)),
        ],
        out_specs=pl.BlockSpec((None, n_t, nc, LANES), lambda j: (j, 0, 0, 0)),
        out_shape=jax.ShapeDtypeStruct((pairs, n_t, nc, LANES), F32),
        scratch_shapes=[pltpu.VMEM((4, nc, LANES), F32), pltpu.VMEM((4, nc, LANES), F32)],
        compiler_params=_params(("parallel",)),
        name="ssm_core",
    )(u, r1, v, a16, dsk)


GROUPS_PER_TILE = LANES // SSM_GROUP


def _lane_slot(rows):
    return lax.broadcasted_iota(jnp.int32, (rows, LANES), 1) // SSM_GROUP


def _block_transposes(sets, slot):
    sets = [list(t) for t in sets]
    d = GROUPS_PER_TILE // 2
    while d:
        low = (slot & d) == 0
        for t in sets:
            for a in range(GROUPS_PER_TILE):
                if not a & d:
                    lo, hi = t[a], t[a + d]
                    t[a] = jnp.where(low, lo, pltpu.roll(hi, d * SSM_GROUP, 1))
                    t[a + d] = jnp.where(low, pltpu.roll(lo, LANES - d * SSM_GROUP, 1), hi)
        d //= 2
    return sets


def _chunk_tiling(dims):
    ncl, ncc = dims["seq"] // SSM_CHUNK, dims["ctx"] // SSM_CHUNK
    ct = _pick([ncl, ncc], [16, 8, 4, 2, 1])
    nlt, nct = ncl // ct, ncc // ct

    def token_block(k, b):
        return jnp.where(k < nlt, b * nlt + k, dims["batch"] * nlt + b * nct + (k - nlt))

    return ct, nlt + nct, token_block


def _ssm_pack_kernel(*refs, batch, ct):
    x_refs, o_ref, scr = refs[:batch], refs[batch], refs[batch + 1]
    slot = _lane_slot(ct)
    halves = SSM_CHUNK // GROUPS_PER_TILE
    for b in range(batch):
        where = [(j, half) for j in range(x_refs[b].shape[0]) for half in range(halves)]
        sets = [[x_refs[b][j, pl.ds(GROUPS_PER_TILE * half + k, ct, stride=SSM_CHUNK), :]
                 for k in range(GROUPS_PER_TILE)] for j, half in where]
        for (j, half), outs in zip(where, _block_transposes(sets, slot)):
            for gg, out in enumerate(outs):
                g = GROUPS_PER_TILE * j + gg
                scr[g // 2, (g % 2) * halves + half, pl.ds(b, ct, stride=batch), :] = out
    o_ref[...] = scr[...].astype(o_ref.dtype)


def _ssm_pack(u32, dims):
    batch, pairs = dims["batch"], dims["pairs"]
    ct, n_tiles, token_block = _chunk_tiling(dims)
    n_in, nt, _ = u32.shape
    n_t = 2 * SSM_CHUNK * SSM_GROUP // LANES
    return pl.pallas_call(
        functools.partial(_ssm_pack_kernel, batch=batch, ct=ct),
        grid=(n_tiles,),
        in_specs=[pl.BlockSpec((n_in, ct * SSM_CHUNK, LANES),
                               functools.partial(lambda k, b: (0, token_block(k, b), 0), b=b))
                  for b in range(batch)],
        out_specs=pl.BlockSpec((pairs, n_t, ct * batch, LANES), lambda k: (0, 0, k, 0)),
        out_shape=jax.ShapeDtypeStruct((pairs, n_t, nt // SSM_CHUNK, LANES), BF16),
        scratch_shapes=[pltpu.VMEM((pairs, n_t, ct * batch, LANES), F32)],
        compiler_params=_params(("parallel",)),
        name="ssm_pack",
    )(*([u32] * batch))


def _ssm_unpack_kernel(y_ref, o_ref, *, batch, ct):
    b = pl.program_id(1)
    slot = _lane_slot(ct)
    halves = SSM_CHUNK // GROUPS_PER_TILE
    where = [(j, half) for j in range(o_ref.shape[0]) for half in range(halves)]
    sets = []
    for j, half in where:
        groups = [GROUPS_PER_TILE * j + gg for gg in range(GROUPS_PER_TILE)]
        sets.append([y_ref[g // 2, (g % 2) * halves + half, pl.ds(b, ct, stride=batch), :] for g in groups])
    for (j, half), outs in zip(where, _block_transposes(sets, slot)):
        for k, out in enumerate(outs):
            o_ref[j, pl.ds(GROUPS_PER_TILE * half + k, ct, stride=SSM_CHUNK), :] = out


def _ssm_unpack(y, dims):
    batch = dims["batch"]
    pairs, n_t, nc, _ = y.shape
    ct, n_tiles, token_block = _chunk_tiling(dims)
    n_out = dims["ssm_w"] // LANES
    return pl.pallas_call(
        functools.partial(_ssm_unpack_kernel, batch=batch, ct=ct),
        grid=(n_tiles, batch),
        in_specs=[pl.BlockSpec((pairs, n_t, ct * batch, LANES), lambda k, b: (0, 0, k, 0))],
        out_specs=pl.BlockSpec((n_out, ct * SSM_CHUNK, LANES), lambda k, b: (0, token_block(k, b), 0)),
        out_shape=jax.ShapeDtypeStruct((n_out, nc * SSM_CHUNK, LANES), F32),
        compiler_params=_params(("parallel", "arbitrary")),
        name="ssm_unpack",
    )(y)


def _merge_kernel(pb_ref, pc_ref, pv_ref, hpc_ref, hpv_ref, hnc_ref, hnv_ref, cw_ref, ys_ref, at_ref,
                  g_ref, wc_ref, wg_ref, wa_ref, o_ref, *, tm, tn, seq, ctx, n_lat):
    d = o_ref.shape[1]
    r0 = pl.program_id(0) * tm
    period = jnp.where(r0 >= n_lat, ctx, seq)
    starts = r0 % period == 0
    ends = (r0 + tm) % period == 0
    w = pc_ref[...].astype(F32) * pv_ref[...].astype(F32)
    prev = (hpc_ref[...].astype(F32) * hpv_ref[...].astype(F32))[HALO_ROWS - 1:HALO_ROWS]
    nxt = (hnc_ref[...].astype(F32) * hnv_ref[...].astype(F32))[0:1]
    prev = jnp.where(starts, 0.0, prev)
    nxt = jnp.where(ends, 0.0, nxt)
    row = lax.broadcasted_iota(jnp.int32, w.shape, 0)
    up = jnp.where(row == 0, prev, pltpu.roll(w, 1, 0))
    dn = jnp.where(row == tm - 1, nxt, pltpu.roll(w, tm - 1, 0))
    cw = cw_ref[...]
    cf = (pb_ref[...].astype(F32) * (cw[0:1] * up + cw[1:2] * w + cw[2:3] * dn)).astype(BF16)
    yg = jax.nn.gelu(_chunk_rows(ys_ref)).astype(BF16)
    at = at_ref[...]
    for c in range(d // tn):
        cols = slice(tn * c, tn * (c + 1))
        gate = lambda k: jax.nn.sigmoid(g_ref[:, k * d + tn * c:k * d + tn * (c + 1)].astype(F32))
        ya = jnp.dot(cf, wc_ref[:, cols], preferred_element_type=F32)
        yc = jnp.dot(at, wa_ref[:, cols], preferred_element_type=F32)
        ys = jnp.dot(yg, wg_ref[:, cols], preferred_element_type=F32) * jax.nn.sigmoid(
            jnp.dot(yg, wg_ref[:, d + tn * c:d + tn * (c + 1)], preferred_element_type=F32))
        o_ref[:, cols] = (gate(0) * ya + gate(1) * ys + gate(2) * yc).astype(o_ref.dtype)


def _merge(main, gates, ys, attn, conv_w, w_conv_out, w_glu, w_attn_out, l, dims, rows):
    d = dims["d"]
    cw_, sw, aw = dims["conv_w"], dims["ssm_w"], dims["attn_w"]
    tm = _pick([dims["seq"], dims["ctx"]], [512, 256, 128, 64, 32])
    tn = _pick([d], [512, 256, 128])
    n_halo = main.shape[0] // HALO_ROWS
    per = tm // HALO_ROWS
    prev_blk = lambda i: jnp.maximum(i * per - 1, 0)
    next_blk = lambda i: jnp.minimum((i + 1) * per, n_halo - 1)
    resident = lambda shape: pl.BlockSpec((None,) + shape, lambda i: (l, 0, 0), pipeline_mode=pl.Buffered(1))
    return pl.pallas_call(
        functools.partial(_merge_kernel, tm=tm, tn=tn, seq=dims["seq"], ctx=dims["ctx"], n_lat=dims["n_lat"]),
        grid=(rows // tm,),
        in_specs=[
            pl.BlockSpec((tm, cw_), lambda i: (i, 0)),
            pl.BlockSpec((tm, cw_), lambda i: (i, 1)),
            pl.BlockSpec((tm, cw_), lambda i: (i, 2)),
            pl.BlockSpec((HALO_ROWS, cw_), lambda i: (prev_blk(i), 1)),
            pl.BlockSpec((HALO_ROWS, cw_), lambda i: (prev_blk(i), 2)),
            pl.BlockSpec((HALO_ROWS, cw_), lambda i: (next_blk(i), 1)),
            pl.BlockSpec((HALO_ROWS, cw_), lambda i: (next_blk(i), 2)),
            resident((CONV_K, cw_)),
            pl.BlockSpec((sw // LANES, tm, LANES), lambda i: (0, i, 0)),
            pl.BlockSpec((tm, aw), lambda i: (i, 0)),
            pl.BlockSpec((tm, 3 * d), lambda i: (i, 0)),
            resident((cw_, d)),
            resident((sw, 2 * d)),
            resident((aw, d)),
        ],
        out_specs=pl.BlockSpec((tm, d), lambda i: (i, 0)),
        out_shape=jax.ShapeDtypeStruct((rows, d), BF16),
        compiler_params=_params(("parallel",)),
        name="branch_merge",
    )(main, main, main, main, main, main, main, conv_w, ys, attn, gates, w_conv_out, w_glu, w_attn_out)


def _outproj_kernel(m_ref, w_ref, x_ref, gate_ref, g_ref, o_ref):
    r = jnp.dot(m_ref[...], w_ref[...], preferred_element_type=F32)
    o_ref[...] = x_ref[...] + gate_ref[...] * _rms(r, g_ref[...])


def _out_projection(m, w_out, xt, modr, g_post, l, dims):
    rows, d = m.shape
    tm = _pick([dims["seq"], dims["n_ctx"]], [512, 256, 128])
    grp = lambda i: jnp.minimum((i * tm) // dims["seq"], dims["batch"])
    return pl.pallas_call(
        _outproj_kernel,
        grid=(rows // tm,),
        in_specs=[
            pl.BlockSpec((tm, d), lambda i: (i, 0)),
            pl.BlockSpec((None, d, d), lambda i: (l, 0, 0)),
            pl.BlockSpec((tm, d), lambda i: (i, 0)),
            pl.BlockSpec((None, None, 1, d), lambda i: (l, grp(i), 0, 2)),
            pl.BlockSpec((None, 1, d), lambda i: (l, 0, 0)),
        ],
        out_specs=pl.BlockSpec((tm, d), lambda i: (i, 0)),
        out_shape=jax.ShapeDtypeStruct(xt.shape, F32),
        input_output_aliases={2: 0},
        compiler_params=_params(("parallel",)),
        name="out_projection",
    )(m, w_out, xt, modr, g_post)


def _mlp_kernel(x_ref, gpre_ref, sh_ref, sc_ref, wu_ref, wd_ref, gate_ref, gpost_ref, o_ref, h_scr, acc_scr):
    j = pl.program_id(1)

    @pl.when(j == 0)
    def _():
        h = _rms(x_ref[...], gpre_ref[...]) * (1.0 + sc_ref[...]) + sh_ref[...]
        h_scr[...] = h.astype(BF16)
        acc_scr[...] = jnp.zeros(acc_scr.shape, F32)

    u = jnp.maximum(jnp.dot(h_scr[...], wu_ref[...], preferred_element_type=F32), 0.0)
    acc_scr[...] += jnp.dot((u * u).astype(BF16), wd_ref[...], preferred_element_type=F32)

    @pl.when(j == pl.num_programs(1) - 1)
    def _():
        o_ref[...] = x_ref[...] + gate_ref[...] * _rms(acc_scr[...], gpost_ref[...])


def _mlp(xt, g_pre, g_post, modr, w_up, w_down, l, dims, rows):
    d = xt.shape[1]
    d_ff = w_up.shape[-1]
    tm = _pick([dims["seq"], dims["n_ctx"]], [512, 256, 128])
    tf = _pick([d_ff], [1024, 512, 256, 128])
    grp = lambda i: jnp.minimum((i * tm) // dims["seq"], dims["batch"])
    mod = lambda k: pl.BlockSpec((None, None, 1, d), lambda i, j: (l, grp(i), 0, k))
    vec = pl.BlockSpec((None, 1, d), lambda i, j: (l, 0, 0))
    in_place = rows == xt.shape[0]
    return pl.pallas_call(
        _mlp_kernel,
        grid=(rows // tm, d_ff // tf),
        in_specs=[
            pl.BlockSpec((tm, d), lambda i, j: (i, 0)),
            vec, mod(3), mod(4),
            pl.BlockSpec((None, d, tf), lambda i, j: (l, 0, j)),
            pl.BlockSpec((None, tf, d), lambda i, j: (l, j, 0)),
            mod(5), vec,
        ],
        out_specs=pl.BlockSpec((tm, d), lambda i, j: (i, 0)),
        out_shape=jax.ShapeDtypeStruct((rows, d), F32),
        scratch_shapes=[pltpu.VMEM((tm, d), BF16), pltpu.VMEM((tm, d), F32)],
        input_output_aliases={0: 0} if in_place else {},
        compiler_params=_params(("parallel", "arbitrary")),
        name="mlp",
    )(xt, g_pre, modr, modr, w_up, w_down, modr, g_post)


def kernel(x, c, ctx, c_ctx, w_mod, b_mod, g_pre_mix, g_post_mix, g_pre_mlp, g_post_mlp, w_in, conv_w,
           w_conv_out, ssm_lam_re, ssm_lam_im, ssm_log_dt, ssm_b_re, ssm_b_im, ssm_c_re, ssm_c_im, ssm_d,
           w_glu, q_gain, k_gain, w_attn_out, w_out, w_up, w_down):
    batch, seq, d = x.shape
    ctx_len = ctx.shape[1]
    depth = w_in.shape[0]
    in_w = w_in.shape[-1]
    conv_width = conv_w.shape[-1]
    ssm_w = ssm_d.shape[-1]
    attn_w = w_attn_out.shape[1]
    gate_w = 3 * d
    kv_w = (in_w - 3 * conv_width - ssm_w - attn_w - gate_w) // 2
    groups = ssm_lam_re.shape[2]
    assert batch + 1 <= MOD_ROWS and SUBLANES % batch == 0
    assert kv_w == N_KV_HEADS * HEAD_DIM and ssm_lam_re.shape[3] == SSM_STATE
    assert ssm_w == groups * SSM_GROUP and groups % 2 == 0
    assert seq % GRID_W == 0 and (batch * seq) % ctx_len == 0 and ctx_len % LANES == 0
    dims = dict(
        batch=batch, seq=seq, ctx=ctx_len, d=d, n_lat=batch * seq, n_ctx=batch * ctx_len,
        conv_w=conv_width, ssm_w=ssm_w, attn_w=attn_w, kv_w=kv_w, gate_w=gate_w, pairs=groups // 2,
        o_u=3 * conv_width, o_q=3 * conv_width + ssm_w, o_k=3 * conv_width + ssm_w + attn_w,
        o_v=3 * conv_width + ssm_w + attn_w + kv_w, main_w=in_w - gate_w,
    )
    n_lat, nt = dims["n_lat"], dims["n_lat"] + dims["n_ctx"]

    xt = jnp.concatenate([x.reshape(n_lat, d), ctx.reshape(dims["n_ctx"], d)], axis=0)
    cc = jnp.concatenate([c, c_ctx[None, :], jnp.zeros((MOD_ROWS - batch - 1, d), F32)], axis=0)
    mod = _modulation(cc, w_mod, b_mod)
    modr = mod.reshape(depth, MOD_ROWS, 1, N_MOD * d)

    row_vec = lambda a: a.reshape(depth, 1, a.shape[-1])
    g_pre_mix, g_post_mix, g_pre_mlp, g_post_mlp = map(row_vec, (g_pre_mix, g_post_mix, g_pre_mlp, g_post_mlp))
    q_gain, k_gain = row_vec(q_gain), row_vec(k_gain)
    w_in, w_conv_out, w_glu, w_attn_out, w_out, w_up, w_down = (
        w.astype(BF16) for w in (w_in, w_conv_out, w_glu, w_attn_out, w_out, w_up, w_down))

    qk_tile = _pick([seq, ctx_len], [256, 128, 64, 32])
    cos_t, sin_t = _rope_tables(seq, qk_tile)
    r1, v_mat, a16 = _ssm_parameters(ssm_lam_re, ssm_lam_im, ssm_log_dt, ssm_b_re, ssm_b_im, ssm_c_re, ssm_c_im)
    dsk = jnp.broadcast_to(
        ssm_d.astype(F32).reshape(depth, groups // 2, 2, 1, SSM_GROUP),
        (depth, groups // 2, 2, SSM_CHUNK, SSM_GROUP)).reshape(depth, groups // 2, 1, 2 * SSM_CHUNK * SSM_GROUP)

    for l in range(depth):
        need_ctx = l < depth - 1
        rows = nt if need_ctx else n_lat
        main, gates, u32 = _in_projection(xt, g_pre_mix, modr, w_in, l, dims)
        u = _ssm_pack(u32, dims)
        ys = _ssm_unpack(_ssm_core(u, r1, v_mat, a16, dsk, l, dims), dims)
        q, k = _qk_prepare(main, cos_t, sin_t, q_gain, k_gain, l, dims)
        attn = _attention(q, k, main, None, dims)
        if need_ctx:
            attn = _attention(q, k, main, attn, dims)
        m = _merge(main, gates, ys, attn, conv_w, w_conv_out, w_glu, w_attn_out, l, dims, rows)
        xt = _out_projection(m, w_out, xt, modr, g_post_mix, l, dims)
        xt = _mlp(xt, g_pre_mlp, g_post_mlp, modr, w_up, w_down, l, dims, rows)
    return xt[:n_lat].reshape(batch, seq, d)
```

```python
import functools
import math

import jax
import jax.numpy as jnp
from jax import lax
from jax.experimental import pallas as pl
from jax.experimental.pallas import tpu as pltpu

F32 = jnp.float32
BF16 = jnp.bfloat16

NORM_EPS = 1e-6
N_MOD = 6
HEAD_DIM = 128
N_KV_HEADS = 2
GRID_W = 64
ROPE_BASE = 10000.0
CONV_K = 3
SSM_GROUP = 16
SSM_STATE = 64
SSM_RE_MAX = -1e-4
SSM_CHUNK = 16
SUBLANES = 8
LANES = 128
HALO_ROWS = 16
MAX_KEY_CHUNK = 4224
MOD_ROWS = 8
VMEM_LIMIT = 56 * 2**20


def _params(sem, vmem=VMEM_LIMIT):
    return pltpu.CompilerParams(dimension_semantics=sem, vmem_limit_bytes=vmem)


def _pick(dims, cands):
    for c in cands:
        if all(d % c == 0 for d in dims):
            return c
    raise ValueError(f"no tile in {cands} divides {dims}")


def _rms(x, g):
    ms = jnp.mean(x * x, axis=-1, keepdims=True)
    return x * lax.rsqrt(ms + NORM_EPS) * g


def _mod_kernel(c_ref, w_ref, b_ref, o_ref):
    c = c_ref[...]
    s = (c * jax.nn.sigmoid(c)).astype(BF16)
    o_ref[...] = jnp.dot(s, w_ref[...].astype(BF16), preferred_element_type=F32) + b_ref[...]


def _modulation(cc, w_mod, b_mod):
    depth, d, nout = w_mod.shape
    tn = _pick([nout], [1024, 512, 256, 128])
    return pl.pallas_call(
        _mod_kernel,
        grid=(depth, nout // tn),
        in_specs=[
            pl.BlockSpec((MOD_ROWS, d), lambda l, j: (0, 0)),
            pl.BlockSpec((None, d, tn), lambda l, j: (l, 0, j)),
            pl.BlockSpec((None, 1, tn), lambda l, j: (l, 0, j)),
        ],
        out_specs=pl.BlockSpec((None, MOD_ROWS, tn), lambda l, j: (l, 0, j)),
        out_shape=jax.ShapeDtypeStruct((depth, MOD_ROWS, nout), F32),
        compiler_params=_params(("parallel", "parallel")),
        name="modulation",
    )(cc, w_mod, b_mod.reshape(depth, 1, nout))


def _norm_mod(x, g, shift, scale):
    return (_rms(x, g) * (1.0 + scale) + shift).astype(BF16)


def _prenorm_kernel(x_ref, g_ref, sh_ref, sc_ref, h_ref):
    h_ref[...] = _norm_mod(x_ref[...], g_ref[...], sh_ref[...], sc_ref[...])


def _row_group(dims, tm):
    return lambda i: jnp.minimum((i * tm) // dims["seq"], dims["batch"])


def _prenorm(xt, gain, modr, l, dims):
    nt, d = xt.shape
    tm = _pick([dims["seq"], dims["n_ctx"]], [512, 256, 128])
    grp = _row_group(dims, tm)
    return pl.pallas_call(
        _prenorm_kernel,
        grid=(nt // tm,),
        in_specs=[
            pl.BlockSpec((tm, d), lambda i: (i, 0)),
            pl.BlockSpec((None, 1, d), lambda i: (l, 0, 0)),
            pl.BlockSpec((None, None, 1, d), lambda i: (l, grp(i), 0, 0)),
            pl.BlockSpec((None, None, 1, d), lambda i: (l, grp(i), 0, 1)),
        ],
        out_specs=pl.BlockSpec((tm, d), lambda i: (i, 0)),
        out_shape=jax.ShapeDtypeStruct((nt, d), BF16),
        compiler_params=_params(("parallel",)),
        name="prenorm",
    )(xt, gain, modr, modr)


def _proj_kernel(h_ref, w_ref, o_ref, *u_refs, u_tile, u_off):
    r = jnp.dot(h_ref[...], w_ref[...], preferred_element_type=F32)
    o_ref[...] = r.astype(o_ref.dtype)
    if u_refs:
        (ou_ref,) = u_refs

        @pl.when(pl.program_id(1) == u_tile)
        def _():
            for t in range(ou_ref.shape[0]):
                ou_ref[t] = r[:, u_off + LANES * t:u_off + LANES * (t + 1)]


def _projection(h, w, l, dims, tn, f32_cols=None):
    nt, d = h.shape
    n_out = w.shape[-1]
    tm = _pick([dims["seq"], dims["n_ctx"]], [1024, 512, 256, 128])
    assert n_out % tn == 0
    out_specs = [pl.BlockSpec((tm, tn), lambda i, j: (i, j))]
    out_shape = [jax.ShapeDtypeStruct((nt, n_out), BF16)]
    u_tile = u_off = 0
    if f32_cols is not None:
        start, width = f32_cols
        u_tile, u_off = start // tn, start % tn
        assert u_off + width <= tn and u_off % LANES == 0 and width % LANES == 0
        out_specs.append(pl.BlockSpec((width // LANES, tm, LANES), lambda i, j: (0, i, 0)))
        out_shape.append(jax.ShapeDtypeStruct((width // LANES, nt, LANES), F32))
    return pl.pallas_call(
        functools.partial(_proj_kernel, u_tile=u_tile, u_off=u_off),
        grid=(nt // tm, n_out // tn),
        in_specs=[
            pl.BlockSpec((tm, d), lambda i, j: (i, 0)),
            pl.BlockSpec((None, d, tn), lambda i, j: (l, 0, j)),
        ],
        out_specs=out_specs,
        out_shape=out_shape,
        compiler_params=_params(("parallel", "arbitrary")),
        name="in_projection",
    )(h, w)


def _qk_kernel(xq_ref, xk_ref, cos_ref, sin_ref, qg_ref, kg_ref, q_ref, k_ref, *, n_q, n_kv, scale):
    cos = cos_ref[...]
    sin = sin_ref[...]
    lane = lax.broadcasted_iota(jnp.int32, cos.shape, 1)
    first_half = (lane % (HEAD_DIM // 2)) < (HEAD_DIM // 4)
    for h in range(n_q + n_kv):
        if h < n_q:
            xh = xq_ref[:, HEAD_DIM * h:HEAD_DIM * (h + 1)].astype(F32)
        else:
            xh = xk_ref[:, HEAD_DIM * (h - n_q):HEAD_DIM * (h - n_q + 1)].astype(F32)
        gain = qg_ref[...] * scale if h < n_q else kg_ref[...]
        y = _rms(xh, gain)
        partner = jnp.where(first_half, pltpu.roll(y, HEAD_DIM - HEAD_DIM // 4, 1),
                            pltpu.roll(y, HEAD_DIM // 4, 1))
        r = (y * cos + partner * sin).astype(BF16)
        if h < n_q:
            q_ref[:, HEAD_DIM * h:HEAD_DIM * (h + 1)] = r
        else:
            k_ref[:, HEAD_DIM * (h - n_q):HEAD_DIM * (h - n_q + 1)] = r


def _qk_prepare(proj, cos_t, sin_t, q_gain, k_gain, l, dims):
    nt = proj.shape[0]
    attn_w, kv_w = dims["attn_w"], dims["kv_w"]
    tm = _pick([dims["seq"], dims["ctx"]], [256, 128, 64, 32])
    assert dims["o_q"] % attn_w == 0 and dims["o_k"] % kv_w == 0
    n_lat = dims["n_lat"] // tm
    n_seq = dims["seq"] // tm
    tab = lambda i: (jnp.where(i < n_lat, i % n_seq, n_seq), 0)
    return pl.pallas_call(
        functools.partial(_qk_kernel, n_q=attn_w // HEAD_DIM, n_kv=kv_w // HEAD_DIM,
                          scale=HEAD_DIM ** -0.5 * math.log2(math.e)),
        grid=(nt // tm,),
        in_specs=[
            pl.BlockSpec((tm, attn_w), lambda i: (i, dims["o_q"] // attn_w)),
            pl.BlockSpec((tm, kv_w), lambda i: (i, dims["o_k"] // kv_w)),
            pl.BlockSpec((tm, HEAD_DIM), tab),
            pl.BlockSpec((tm, HEAD_DIM), tab),
            pl.BlockSpec((None, 1, HEAD_DIM), lambda i: (l, 0, 0)),
            pl.BlockSpec((None, 1, HEAD_DIM), lambda i: (l, 0, 0)),
        ],
        out_specs=[
            pl.BlockSpec((tm, attn_w), lambda i: (i, 0)),
            pl.BlockSpec((tm, kv_w), lambda i: (i, 0)),
        ],
        out_shape=[
            jax.ShapeDtypeStruct((nt, attn_w), BF16),
            jax.ShapeDtypeStruct((nt, kv_w), BF16),
        ],
        compiler_params=_params(("parallel",)),
        name="qk_prepare",
    )(proj, proj, cos_t, sin_t, q_gain, k_gain)


def _rope_tables(seq, pad_rows):
    rows = seq // GRID_W
    row = jnp.repeat(jnp.arange(rows), GRID_W)
    col = jnp.tile(jnp.arange(GRID_W), rows)
    half = HEAD_DIM // 4
    inv_freq = ROPE_BASE ** (-jnp.arange(half, dtype=F32) / half)
    ang_r = row.astype(F32)[:, None] * inv_freq
    ang_c = col.astype(F32)[:, None] * inv_freq
    cr, sr, cc, sc = jnp.cos(ang_r), jnp.sin(ang_r), jnp.cos(ang_c), jnp.sin(ang_c)
    cos_t = jnp.concatenate([cr, cr, cc, cc], axis=-1)
    sin_t = jnp.concatenate([-sr, sr, -sc, sc], axis=-1)
    cos_t = jnp.concatenate([cos_t, jnp.ones((pad_rows, HEAD_DIM), F32)], axis=0)
    sin_t = jnp.concatenate([sin_t, jnp.zeros((pad_rows, HEAD_DIM), F32)], axis=0)
    return cos_t, sin_t


def _attn_kernel(*refs, tq, tk, group, n_src):
    q_ref = refs[0]
    k_refs, v_refs = refs[1:1 + 2 * n_src:2], refs[2:2 + 2 * n_src:2]
    o_ref, ke_scr, ve_scr, s_scr, m_scr, acc_scr = refs[-6:]

    @pl.when(pl.program_id(2) == 0)
    def _():
        r0 = 0
        for k_ref, v_ref in zip(k_refs, v_refs):
            n = k_ref.shape[0]
            ke_scr[r0:r0 + n, :] = k_ref[...]
            ve_scr[r0:r0 + n, :HEAD_DIM] = v_ref[...]
            r0 += n
        ve_scr[:, HEAD_DIM:] = jnp.ones((ve_scr.shape[0], HEAD_DIM), BF16)

    q = jnp.concatenate([q_ref[:, HEAD_DIM * g:HEAD_DIM * (g + 1)] for g in range(group)], axis=0)
    n_chunks = s_scr.shape[0]

    def first(c, carry):
        keys = ke_scr[pl.ds(pl.multiple_of(c * tk, tk), tk), :]
        s = lax.dot_general(q, keys, (((1,), (1,)), ((), ())), preferred_element_type=F32)
        s_scr[c] = s
        m = m_scr[...]
        for t in range(tk // LANES):
            m = jnp.maximum(m, s[:, LANES * t:LANES * (t + 1)])
        m_scr[...] = m
        return carry

    m_scr[...] = jnp.full(m_scr.shape, -jnp.inf, F32)
    lax.fori_loop(0, n_chunks, first, 0)
    m_scr[...] = jnp.broadcast_to(jnp.max(m_scr[...], axis=-1, keepdims=True), m_scr.shape)

    def second(c, carry):
        s = s_scr[c]
        m_rep = m_scr[...]
        p = jnp.concatenate(
            [jnp.exp2(s[:, LANES * t:LANES * (t + 1)] - m_rep).astype(BF16) for t in range(tk // LANES)], axis=1)
        acc_scr[...] += jnp.dot(p, ve_scr[pl.ds(pl.multiple_of(c * tk, tk), tk), :], preferred_element_type=F32)
        return carry

    acc_scr[...] = jnp.zeros(acc_scr.shape, F32)
    lax.fori_loop(0, n_chunks, second, 0)
    acc = acc_scr[...]
    o = acc[:, :HEAD_DIM] / acc[:, HEAD_DIM:]
    for g in range(group):
        o_ref[:, HEAD_DIM * g:HEAD_DIM * (g + 1)] = o[tq * g:tq * (g + 1)].astype(o_ref.dtype)


def _attention(q, k, main, prev, dims):
    nt = q.shape[0]
    attn_w, seq, ctx, batch = dims["attn_w"], dims["seq"], dims["ctx"], dims["batch"]
    latent = prev is None
    group = attn_w // HEAD_DIM // N_KV_HEADS
    gw = group * HEAD_DIM
    v_blk = dims["o_v"] // HEAD_DIM
    lq = seq if latent else ctx
    tq = _pick([lq], [128, 64, 32, 16])
    n_keys = ctx + seq if latent else ctx
    tk = max(c for c in range(LANES, min(n_keys, MAX_KEY_CHUNK) + 1, LANES) if n_keys % c == 0)
    row0 = 0 if latent else dims["n_lat"] // tq
    ctx0 = dims["n_lat"] // ctx
    rows = group * tq
    in_specs = [
        pl.BlockSpec((tq, gw), lambda b, h, i: (row0 + b * (lq // tq) + i, h)),
        pl.BlockSpec((ctx, HEAD_DIM), lambda b, h, i: (ctx0 + b, h)),
        pl.BlockSpec((ctx, HEAD_DIM), lambda b, h, i: (ctx0 + b, v_blk + h)),
    ]
    args = [q, k, main]
    aliases = {}
    if latent:
        in_specs += [
            pl.BlockSpec((seq, HEAD_DIM), lambda b, h, i: (b, h)),
            pl.BlockSpec((seq, HEAD_DIM), lambda b, h, i: (b, v_blk + h)),
        ]
        args += [k, main]
    n_src = (len(args) - 1) // 2
    if not latent:
        in_specs.append(pl.BlockSpec(memory_space=pl.ANY))
        args.append(prev)
        aliases = {len(args) - 1: 0}
    scratch = [pltpu.VMEM((n_keys, HEAD_DIM), BF16), pltpu.VMEM((n_keys, 2 * HEAD_DIM), BF16),
               pltpu.VMEM((n_keys // tk, rows, tk), F32), pltpu.VMEM((rows, LANES), F32),
               pltpu.VMEM((rows, 2 * HEAD_DIM), F32)]
    return pl.pallas_call(
        functools.partial(_attn_kernel, tq=tq, tk=tk, group=group, n_src=n_src),
        grid=(batch, N_KV_HEADS, lq // tq),
        in_specs=in_specs,
        out_specs=pl.BlockSpec((tq, gw), lambda b, h, i: (row0 + b * (lq // tq) + i, h)),
        out_shape=jax.ShapeDtypeStruct((nt, attn_w), BF16),
        scratch_shapes=scratch,
        input_output_aliases=aliases,
        compiler_params=_params(("parallel", "parallel", "arbitrary")),
        name="attention_latent" if latent else "attention_context",
    )(*args)


def _ssm_params_kernel(lrr_ref, lir_ref, ldr_ref, lrc_ref, lic_ref, ldc_ref, btr_ref, bti_ref,
                       ctr_ref, cti_ref, r1_ref, v_ref, a_ref):
    t = SSM_CHUNK
    pw = SSM_GROUP * t
    r1_ref[...] = jnp.zeros(r1_ref.shape, r1_ref.dtype)
    v_ref[...] = jnp.zeros(v_ref.shape, v_ref.dtype)
    lane = lax.broadcasted_iota(jnp.int32, (SSM_GROUP, LANES), 1)
    col = lax.broadcasted_iota(jnp.int32, (SSM_GROUP, pw), 1)
    tau_col = (lax.broadcasted_iota(jnp.int32, (LANES, pw), 1) // SSM_GROUP).astype(F32)
    tau_row = lax.broadcasted_iota(jnp.int32, (t, LANES), 0).astype(F32)
    masks = (lane < SSM_STATE, lane >= SSM_STATE)
    toep = []
    for d in range(2):
        lr = jnp.minimum(lrr_ref[d], SSM_RE_MAX)
        li = lir_ref[d]
        dt = jnp.exp(ldr_ref[d])
        mag = jnp.exp(lr * dt)
        ab_re = mag * jnp.cos(li * dt)
        ab_im = mag * jnp.sin(li * dt)
        nr = ab_re - 1.0
        den = lr * lr + li * li
        f_re = (nr * lr + ab_im * li) / den
        f_im = (ab_im * lr - nr * li) / den
        bb_re = f_re * btr_ref[d] - f_im * bti_ref[d]
        bb_im = f_re * bti_ref[d] + f_im * btr_ref[d]
        e16 = jnp.exp(lr * dt * float(t))
        a_ref[2 * d:2 * d + 1, :] = e16 * jnp.cos(li * dt * float(t))
        a_ref[2 * d + 1:2 * d + 2, :] = e16 * jnp.sin(li * dt * float(t))
        tau_s = (t - 1.0) - tau_row if d == 0 else tau_row
        es = jnp.exp(lr * dt * tau_s)
        pw_re = es * jnp.cos(li * dt * tau_s)
        pw_im = es * jnp.sin(li * dt * tau_s)
        for s in range(t):
            ar = pw_re[s:s + 1]
            ai = pw_im[s:s + 1]
            w_re = ar * bb_re - ai * bb_im
            w_im = ar * bb_im + ai * bb_re
            for gi in range(2):
                r0 = gi * pw + SSM_GROUP * s
                c0 = 2 * pw + 2 * LANES * d
                r1_ref[r0:r0 + SSM_GROUP, c0:c0 + LANES] = jnp.where(masks[gi], w_re, 0.0).astype(r1_ref.dtype)
                r1_ref[r0:r0 + SSM_GROUP, c0 + LANES:c0 + 2 * LANES] = (
                    jnp.where(masks[gi], w_im, 0.0).astype(r1_ref.dtype))
        lrc = jnp.minimum(lrc_ref[d], SSM_RE_MAX)
        lic = lic_ref[d]
        dtc = jnp.exp(ldc_ref[d])
        tau_q = tau_col if d == 0 else (t - 1.0) - tau_col
        eq = jnp.exp(lrc * dtc * tau_q)
        p_re = eq * jnp.cos(lic * dtc * tau_q)
        p_im = eq * jnp.sin(lic * dtc * tau_q)
        c_re = ctr_ref[d]
        c_im = cti_ref[d]
        q_re = p_re * c_re - p_im * c_im
        q_im = p_re * c_im + p_im * c_re
        per_group = []
        for gi in range(2):
            k_mat = (jnp.dot(jnp.where(masks[gi], bb_re, 0.0), q_re, precision=lax.Precision.HIGHEST,
                             preferred_element_type=F32)
                     - jnp.dot(jnp.where(masks[gi], bb_im, 0.0), q_im, precision=lax.Precision.HIGHEST,
                               preferred_element_type=F32))
            per_group.append(k_mat)
        toep.append(per_group)
        magc = jnp.exp(lrc * dtc)
        abc_re = magc * jnp.cos(lic * dtc)
        abc_im = magc * jnp.sin(lic * dtc)
        q1_re = abc_re * q_re - abc_im * q_im
        q1_im = abc_re * q_im + abc_im * q_re
        for gi in range(2):
            rows = slice(SSM_STATE * gi, SSM_STATE * (gi + 1))
            r0 = 2 * LANES * d + SSM_STATE * gi
            v_ref[r0:r0 + SSM_STATE, pw * gi:pw * (gi + 1)] = q1_re[rows].astype(v_ref.dtype)
            v_ref[r0 + LANES:r0 + LANES + SSM_STATE, pw * gi:pw * (gi + 1)] = (-q1_im[rows]).astype(v_ref.dtype)
    for gi in range(2):
        k_f, k_b = toep[0][gi], toep[1][gi]
        for s in range(t):
            sh = SSM_GROUP * s
            f = k_f if sh == 0 else jnp.where(col >= sh, pltpu.roll(k_f, sh, 1), 0.0)
            back = SSM_GROUP * (t - 1 - s)
            b = k_b if back == 0 else jnp.where(col < pw - back, pltpu.roll(k_b, pw - back, 1), 0.0)
            r0 = gi * pw + sh
            r1_ref[r0:r0 + SSM_GROUP, pw * gi:pw * (gi + 1)] = (f + b).astype(r1_ref.dtype)


def _pair_rows(a):
    depth, _, groups, n = a.shape
    return a.reshape(depth, 2, groups // 2, 2 * n).transpose(0, 2, 1, 3)


def _ssm_parameters(lam_re, lam_im, log_dt, b_re, b_im, c_re, c_im):
    depth, _, groups, n = lam_re.shape
    pairs = groups // 2
    t = SSM_CHUNK
    pw = SSM_GROUP * t
    ld = jnp.broadcast_to(log_dt[..., None], lam_re.shape)
    rows = [_pair_rows(a.astype(F32))[:, :, :, None, :] for a in (lam_re, lam_im, ld)]
    cols = [_pair_rows(a.astype(F32))[:, :, :, :, None] for a in (lam_re, lam_im, ld)]

    def bt(b):
        return (b.astype(F32).reshape(depth, 2, pairs, 2, n, SSM_GROUP).transpose(0, 2, 1, 5, 3, 4)
                .reshape(depth, pairs, 2, SSM_GROUP, 2 * n))

    def ct(c):
        c = (c.astype(F32).reshape(depth, 2, pairs, 2, SSM_GROUP, n).transpose(0, 2, 1, 3, 5, 4)
             .reshape(depth, pairs, 2, 2 * n, SSM_GROUP))
        return jnp.tile(c, (1, 1, 1, 1, t))

    row_spec = pl.BlockSpec((None, None, 2, 1, 2 * n), lambda l, j: (l, j, 0, 0, 0))
    col_spec = pl.BlockSpec((None, None, 2, 2 * n, 1), lambda l, j: (l, j, 0, 0, 0))
    bt_spec = pl.BlockSpec((None, None, 2, SSM_GROUP, 2 * n), lambda l, j: (l, j, 0, 0, 0))
    ct_spec = pl.BlockSpec((None, None, 2, 2 * n, pw), lambda l, j: (l, j, 0, 0, 0))
    return pl.pallas_call(
        _ssm_params_kernel,
        grid=(depth, pairs),
        in_specs=[row_spec] * 3 + [col_spec] * 3 + [bt_spec] * 2 + [ct_spec] * 2,
        out_specs=[
            pl.BlockSpec((None, None, 2 * pw, 4 * pw), lambda l, j: (l, j, 0, 0)),
            pl.BlockSpec((None, None, 8 * n, 2 * pw), lambda l, j: (l, j, 0, 0)),
            pl.BlockSpec((None, 4, 2 * n), lambda l, j: (l, 0, j)),
        ],
        out_shape=[
            jax.ShapeDtypeStruct((depth, pairs, 2 * pw, 4 * pw), BF16),
            jax.ShapeDtypeStruct((depth, pairs, 8 * n, 2 * pw), BF16),
            jax.ShapeDtypeStruct((depth, 4, groups * n), F32),
        ],
        compiler_params=_params(("parallel", "parallel")),
        name="ssm_parameters",
    )(*rows, *cols, bt(b_re), bt(b_im), ct(c_re), ct(c_im))


def _chunk_rows(u_ref):
    return jnp.concatenate([u_ref[t] for t in range(u_ref.shape[0])], axis=-1)


def _scan_chunk_states(s_ref, a_ref, h_ref, *, batch, lat_tiles, ctx_tiles):
    w = s_ref.shape[-1]
    per_tile = SUBLANES // batch
    row_grp = lax.broadcasted_iota(jnp.int32, (SUBLANES, w), 0) // batch
    zero = jnp.zeros((SUBLANES, w), F32)

    def make_tile(k, order, shift):
        ar = a_ref[k:k + 1, :]
        ai = a_ref[k + 1:k + 2, :]

        def tile(ti, carry):
            cur_re, cur_im = carry
            rows = pl.ds(pl.multiple_of(ti * SUBLANES, SUBLANES), SUBLANES)
            s_re = s_ref[k, rows, :]
            s_im = s_ref[k + 1, rows, :]
            h_re, h_im = zero, zero
            for pos in order:
                sel = row_grp == pos
                h_re = jnp.where(sel, cur_re, h_re)
                h_im = jnp.where(sel, cur_im, h_im)
                nxt_re = ar * cur_re - ai * cur_im + s_re
                nxt_im = ar * cur_im + ai * cur_re + s_im
                if batch == SUBLANES:
                    cur_re, cur_im = nxt_re, nxt_im
                else:
                    cur_re = pltpu.roll(nxt_re, shift, 0)
                    cur_im = pltpu.roll(nxt_im, shift, 0)
            h_ref[k, rows, :] = h_re
            h_ref[k + 1, rows, :] = h_im
            return cur_re, cur_im

        return tile

    fwd = make_tile(0, range(per_tile), batch)
    carry = lax.fori_loop(0, ctx_tiles, lambda i, c: fwd(lat_tiles + i, c), (zero, zero))
    lax.fori_loop(0, lat_tiles, fwd, carry)
    bwd = make_tile(2, range(per_tile - 1, -1, -1), SUBLANES - batch)
    carry = lax.fori_loop(0, ctx_tiles, lambda i, c: bwd(lat_tiles + ctx_tiles - 1 - i, c), (zero, zero))
    lax.fori_loop(0, lat_tiles, lambda i, c: bwd(lat_tiles - 1 - i, c), carry)


def _ssm_core_kernel(u_ref, r1_ref, v_ref, a_ref, d_ref, y_ref, s_scr, h_scr, *, batch, lat_tiles, ctx_tiles):
    n_t = u_ref.shape[0]
    kw = n_t * LANES
    u = _chunk_rows(u_ref)
    r = jnp.dot(u, r1_ref[...], preferred_element_type=F32)
    for t in range(n_t):
        y_ref[t] = r[:, LANES * t:LANES * (t + 1)]
    for k in range(4):
        s_scr[k] = r[:, kw + LANES * k:kw + LANES * (k + 1)]
    _scan_chunk_states(s_scr, a_ref, h_scr, batch=batch, lat_tiles=lat_tiles, ctx_tiles=ctx_tiles)
    h = jnp.concatenate([h_scr[k] for k in range(4)], axis=-1).astype(BF16)
    y = jnp.dot(h, v_ref[...], preferred_element_type=F32) + d_ref[...] * u.astype(F32)
    for t in range(n_t):
        y_ref[t] += y[:, LANES * t:LANES * (t + 1)]


def _ssm_core(u, r1, v, a16, dsk, l, dims):
    pairs, n_t, nc, _ = u.shape
    kw = n_t * LANES
    batch = dims["batch"]
    lat_tiles = dims["seq"] // SSM_CHUNK * batch // SUBLANES
    ctx_tiles = dims["ctx"] // SSM_CHUNK * batch // SUBLANES
    return pl.pallas_call(
        functools.partial(_ssm_core_kernel, batch=batch, lat_tiles=lat_tiles, ctx_tiles=ctx_tiles),
        grid=(pairs,),
        in_specs=[
            pl.BlockSpec((None, n_t, nc, LANES), lambda j: (j, 0, 0, 0)),
            pl.BlockSpec((None, None, kw, 2 * kw), lambda j: (l, j, 0, 0)),
            pl.BlockSpec((None, None, 4 * LANES, kw), lambda j: (l, j, 0, 0)),
            pl.BlockSpec((None, 4, LANES), lambda j: (l, 0, j)),
            pl.BlockSpec((None, None, 1, kw), lambda j: (l, j, 0, 0)),
        ],
        out_specs=pl.BlockSpec((None, n_t, nc, LANES), lambda j: (j, 0, 0, 0)),
        out_shape=jax.ShapeDtypeStruct((pairs, n_t, nc, LANES), F32),
        scratch_shapes=[pltpu.VMEM((4, nc, LANES), F32), pltpu.VMEM((4, nc, LANES), F32)],
        compiler_params=_params(("parallel",)),
        name="ssm_core",
    )(u, r1, v, a16, dsk)


GROUPS_PER_TILE = LANES // SSM_GROUP


def _lane_slot(rows):
    return lax.broadcasted_iota(jnp.int32, (rows, LANES), 1) // SSM_GROUP


def _block_transposes(sets, slot):
    sets = [list(t) for t in sets]
    d = GROUPS_PER_TILE // 2
    while d:
        low = (slot & d) == 0
        for t in sets:
            for a in range(GROUPS_PER_TILE):
                if not a & d:
                    lo, hi = t[a], t[a + d]
                    t[a] = jnp.where(low, lo, pltpu.roll(hi, d * SSM_GROUP, 1))
                    t[a + d] = jnp.where(low, pltpu.roll(lo, LANES - d * SSM_GROUP, 1), hi)
        d //= 2
    return sets


def _chunk_tiling(dims):
    ncl, ncc = dims["seq"] // SSM_CHUNK, dims["ctx"] // SSM_CHUNK
    ct = _pick([ncl, ncc], [16, 8, 4, 2, 1])
    nlt, nct = ncl // ct, ncc // ct

    def token_block(k, b):
        return jnp.where(k < nlt, b * nlt + k, dims["batch"] * nlt + b * nct + (k - nlt))

    return ct, nlt + nct, token_block


def _ssm_pack_kernel(*refs, batch, ct):
    x_refs, o_ref, scr = refs[:batch], refs[batch], refs[batch + 1]
    slot = _lane_slot(ct)
    halves = SSM_CHUNK // GROUPS_PER_TILE
    for b in range(batch):
        where = [(j, half) for j in range(x_refs[b].shape[0]) for half in range(halves)]
        sets = [[x_refs[b][j, pl.ds(GROUPS_PER_TILE * half + k, ct, stride=SSM_CHUNK), :]
                 for k in range(GROUPS_PER_TILE)] for j, half in where]
        for (j, half), outs in zip(where, _block_transposes(sets, slot)):
            for gg, out in enumerate(outs):
                g = GROUPS_PER_TILE * j + gg
                scr[g // 2, (g % 2) * halves + half, pl.ds(b, ct, stride=batch), :] = out
    o_ref[...] = scr[...].astype(o_ref.dtype)


def _ssm_pack(u32, dims):
    batch, pairs = dims["batch"], dims["pairs"]
    ct, n_tiles, token_block = _chunk_tiling(dims)
    n_in, nt, _ = u32.shape
    n_t = 2 * SSM_CHUNK * SSM_GROUP // LANES
    return pl.pallas_call(
        functools.partial(_ssm_pack_kernel, batch=batch, ct=ct),
        grid=(n_tiles,),
        in_specs=[pl.BlockSpec((n_in, ct * SSM_CHUNK, LANES),
                               functools.partial(lambda k, b: (0, token_block(k, b), 0), b=b))
                  for b in range(batch)],
        out_specs=pl.BlockSpec((pairs, n_t, ct * batch, LANES), lambda k: (0, 0, k, 0)),
        out_shape=jax.ShapeDtypeStruct((pairs, n_t, nt // SSM_CHUNK, LANES), BF16),
        scratch_shapes=[pltpu.VMEM((pairs, n_t, ct * batch, LANES), F32)],
        compiler_params=_params(("parallel",)),
        name="ssm_pack",
    )(*([u32] * batch))


def _ssm_unpack_kernel(y_ref, o_ref, *, batch, ct):
    b = pl.program_id(1)
    slot = _lane_slot(ct)
    halves = SSM_CHUNK // GROUPS_PER_TILE
    where = [(j, half) for j in range(o_ref.shape[0]) for half in range(halves)]
    sets = []
    for j, half in where:
        groups = [GROUPS_PER_TILE * j + gg for gg in range(GROUPS_PER_TILE)]
        sets.append([y_ref[g // 2, (g % 2) * halves + half, pl.ds(b, ct, stride=batch), :] for g in groups])
    for (j, half), outs in zip(where, _block_transposes(sets, slot)):
        for k, out in enumerate(outs):
            o_ref[j, pl.ds(GROUPS_PER_TILE * half + k, ct, stride=SSM_CHUNK), :] = out


def _ssm_unpack(y, dims):
    batch = dims["batch"]
    pairs, n_t, nc, _ = y.shape
    ct, n_tiles, token_block = _chunk_tiling(dims)
    n_out = dims["ssm_w"] // LANES
    return pl.pallas_call(
        functools.partial(_ssm_unpack_kernel, batch=batch, ct=ct),
        grid=(n_tiles, batch),
        in_specs=[pl.BlockSpec((pairs, n_t, ct * batch, LANES), lambda k, b: (0, 0, k, 0))],
        out_specs=pl.BlockSpec((n_out, ct * SSM_CHUNK, LANES), lambda k, b: (0, token_block(k, b), 0)),
        out_shape=jax.ShapeDtypeStruct((n_out, nc * SSM_CHUNK, LANES), F32),
        compiler_params=_params(("parallel", "arbitrary")),
        name="ssm_unpack",
    )(y)


def _merge_kernel(pb_ref, pc_ref, pv_ref, hpc_ref, hpv_ref, hnc_ref, hnv_ref, cw_ref, ys_ref, at_ref,
                  g_ref, wc_ref, wg_ref, wa_ref, o_ref, *, tm, tn, seq, ctx, n_lat):
    d = o_ref.shape[1]
    r0 = pl.program_id(0) * tm
    period = jnp.where(r0 >= n_lat, ctx, seq)
    starts = r0 % period == 0
    ends = (r0 + tm) % period == 0
    w = pc_ref[...].astype(F32) * pv_ref[...].astype(F32)
    prev = (hpc_ref[...].astype(F32) * hpv_ref[...].astype(F32))[HALO_ROWS - 1:HALO_ROWS]
    nxt = (hnc_ref[...].astype(F32) * hnv_ref[...].astype(F32))[0:1]
    prev = jnp.where(starts, 0.0, prev)
    nxt = jnp.where(ends, 0.0, nxt)
    row = lax.broadcasted_iota(jnp.int32, w.shape, 0)
    up = jnp.where(row == 0, prev, pltpu.roll(w, 1, 0))
    dn = jnp.where(row == tm - 1, nxt, pltpu.roll(w, tm - 1, 0))
    cw = cw_ref[...]
    cf = (pb_ref[...].astype(F32) * (cw[0:1] * up + cw[1:2] * w + cw[2:3] * dn)).astype(BF16)
    yg = jax.nn.gelu(_chunk_rows(ys_ref)).astype(BF16)
    at = at_ref[...]
    for c in range(d // tn):
        cols = slice(tn * c, tn * (c + 1))
        gate = lambda k: jax.nn.sigmoid(g_ref[:, k * d + tn * c:k * d + tn * (c + 1)].astype(F32))
        ya = jnp.dot(cf, wc_ref[:, cols], preferred_element_type=F32)
        yc = jnp.dot(at, wa_ref[:, cols], preferred_element_type=F32)
        ys = jnp.dot(yg, wg_ref[:, cols], preferred_element_type=F32) * jax.nn.sigmoid(
            jnp.dot(yg, wg_ref[:, d + tn * c:d + tn * (c + 1)], preferred_element_type=F32))
        o_ref[:, cols] = (gate(0) * ya + gate(1) * ys + gate(2) * yc).astype(o_ref.dtype)


def _merge(main, gates, ys, attn, conv_w, w_conv_out, w_glu, w_attn_out, l, dims, rows):
    d = dims["d"]
    cw_, sw, aw = dims["conv_w"], dims["ssm_w"], dims["attn_w"]
    tm = _pick([dims["seq"], dims["ctx"]], [512, 256, 128, 64, 32])
    tn = _pick([d], [512, 256, 128])
    n_halo = main.shape[0] // HALO_ROWS
    per = tm // HALO_ROWS
    prev_blk = lambda i: jnp.maximum(i * per - 1, 0)
    next_blk = lambda i: jnp.minimum((i + 1) * per, n_halo - 1)
    resident = lambda shape: pl.BlockSpec((None,) + shape, lambda i: (l, 0, 0), pipeline_mode=pl.Buffered(1))
    return pl.pallas_call(
        functools.partial(_merge_kernel, tm=tm, tn=tn, seq=dims["seq"], ctx=dims["ctx"], n_lat=dims["n_lat"]),
        grid=(rows // tm,),
        in_specs=[
            pl.BlockSpec((tm, cw_), lambda i: (i, 0)),
            pl.BlockSpec((tm, cw_), lambda i: (i, 1)),
            pl.BlockSpec((tm, cw_), lambda i: (i, 2)),
            pl.BlockSpec((HALO_ROWS, cw_), lambda i: (prev_blk(i), 1)),
            pl.BlockSpec((HALO_ROWS, cw_), lambda i: (prev_blk(i), 2)),
            pl.BlockSpec((HALO_ROWS, cw_), lambda i: (next_blk(i), 1)),
            pl.BlockSpec((HALO_ROWS, cw_), lambda i: (next_blk(i), 2)),
            resident((CONV_K, cw_)),
            pl.BlockSpec((sw // LANES, tm, LANES), lambda i: (0, i, 0)),
            pl.BlockSpec((tm, aw), lambda i: (i, 0)),
            pl.BlockSpec((tm, 3 * d), lambda i: (i, 0)),
            resident((cw_, d)),
            resident((sw, 2 * d)),
            resident((aw, d)),
        ],
        out_specs=pl.BlockSpec((tm, d), lambda i: (i, 0)),
        out_shape=jax.ShapeDtypeStruct((rows, d), BF16),
        compiler_params=_params(("parallel",)),
        name="branch_merge",
    )(main, main, main, main, main, main, main, conv_w, ys, attn, gates, w_conv_out, w_glu, w_attn_out)


def _outproj_kernel(m_ref, w_ref, x_ref, gate_ref, g_ref, gn_ref, shn_ref, scn_ref, o_ref, h_ref):
    r = jnp.dot(m_ref[...], w_ref[...], preferred_element_type=F32)
    x = x_ref[...] + gate_ref[...] * _rms(r, g_ref[...])
    o_ref[...] = x
    h_ref[...] = _norm_mod(x, gn_ref[...], shn_ref[...], scn_ref[...])


def _out_projection(m, w_out, xt, modr, g_post, g_pre_mlp, l, dims):
    rows, d = m.shape
    tm = _pick([dims["seq"], dims["n_ctx"]], [512, 256, 128])
    grp = _row_group(dims, tm)
    mod = lambda k: pl.BlockSpec((None, None, 1, d), lambda i: (l, grp(i), 0, k))
    vec = pl.BlockSpec((None, 1, d), lambda i: (l, 0, 0))
    return pl.pallas_call(
        _outproj_kernel,
        grid=(rows // tm,),
        in_specs=[
            pl.BlockSpec((tm, d), lambda i: (i, 0)),
            pl.BlockSpec((None, d, d), lambda i: (l, 0, 0)),
            pl.BlockSpec((tm, d), lambda i: (i, 0)),
            mod(2), vec, vec, mod(3), mod(4),
        ],
        out_specs=[pl.BlockSpec((tm, d), lambda i: (i, 0)), pl.BlockSpec((tm, d), lambda i: (i, 0))],
        out_shape=[jax.ShapeDtypeStruct(xt.shape, F32), jax.ShapeDtypeStruct((rows, d), BF16)],
        input_output_aliases={2: 0},
        compiler_params=_params(("parallel",)),
        name="out_projection",
    )(m, w_out, xt, modr, g_post, g_pre_mlp, modr, modr)


def _mlp_kernel(x_ref, h_ref, wu_ref, wd_ref, gate_ref, gpost_ref, *rest, with_next):
    j = pl.program_id(1)
    if with_next:
        gn_ref, shn_ref, scn_ref, o_ref, hn_ref, acc_scr = rest
    else:
        o_ref, acc_scr = rest

    @pl.when(j == 0)
    def _():
        acc_scr[...] = jnp.zeros(acc_scr.shape, F32)

    u = jnp.maximum(jnp.dot(h_ref[...], wu_ref[...], preferred_element_type=F32), 0.0)
    acc_scr[...] += jnp.dot((u * u).astype(BF16), wd_ref[...], preferred_element_type=F32)

    @pl.when(j == pl.num_programs(1) - 1)
    def _():
        x = x_ref[...] + gate_ref[...] * _rms(acc_scr[...], gpost_ref[...])
        o_ref[...] = x
        if with_next:
            hn_ref[...] = _norm_mod(x, gn_ref[...], shn_ref[...], scn_ref[...])


def _mlp(xt, h, g_post, modr, w_up, w_down, l, dims, rows, g_next=None):
    d = xt.shape[1]
    d_ff = w_up.shape[-1]
    tm = _pick([dims["seq"], dims["n_ctx"]], [512, 256, 128])
    tf = _pick([d_ff], [1024, 512, 256, 128])
    grp = _row_group(dims, tm)
    mod = lambda layer, k: pl.BlockSpec((None, None, 1, d), lambda i, j: (layer, grp(i), 0, k))
    vec = lambda layer: pl.BlockSpec((None, 1, d), lambda i, j: (layer, 0, 0))
    row_tile = pl.BlockSpec((tm, d), lambda i, j: (i, 0))
    in_specs = [
        row_tile, row_tile,
        pl.BlockSpec((None, d, tf), lambda i, j: (l, 0, j)),
        pl.BlockSpec((None, tf, d), lambda i, j: (l, j, 0)),
        mod(l, 5), vec(l),
    ]
    args = [xt, h, w_up, w_down, modr, g_post]
    out_specs, out_shape = [row_tile], [jax.ShapeDtypeStruct((rows, d), F32)]
    if g_next is not None:
        assert rows == xt.shape[0]
        in_specs += [vec(l + 1), mod(l + 1, 0), mod(l + 1, 1)]
        args += [g_next, modr, modr]
        out_specs.append(row_tile)
        out_shape.append(jax.ShapeDtypeStruct((rows, d), BF16))
    return pl.pallas_call(
        functools.partial(_mlp_kernel, with_next=g_next is not None),
        grid=(rows // tm, d_ff // tf),
        in_specs=in_specs,
        out_specs=out_specs,
        out_shape=out_shape,
        scratch_shapes=[pltpu.VMEM((tm, d), F32)],
        input_output_aliases={0: 0} if g_next is not None else {},
        compiler_params=_params(("parallel", "arbitrary")),
        name="mlp",
    )(*args)


def kernel(x, c, ctx, c_ctx, w_mod, b_mod, g_pre_mix, g_post_mix, g_pre_mlp, g_post_mlp, w_in, conv_w,
           w_conv_out, ssm_lam_re, ssm_lam_im, ssm_log_dt, ssm_b_re, ssm_b_im, ssm_c_re, ssm_c_im, ssm_d,
           w_glu, q_gain, k_gain, w_attn_out, w_out, w_up, w_down):
    batch, seq, d = x.shape
    ctx_len = ctx.shape[1]
    depth = w_in.shape[0]
    in_w = w_in.shape[-1]
    conv_width = conv_w.shape[-1]
    ssm_w = ssm_d.shape[-1]
    attn_w = w_attn_out.shape[1]
    gate_w = 3 * d
    kv_w = (in_w - 3 * conv_width - ssm_w - attn_w - gate_w) // 2
    groups = ssm_lam_re.shape[2]
    assert batch + 1 <= MOD_ROWS and SUBLANES % batch == 0
    assert kv_w == N_KV_HEADS * HEAD_DIM and ssm_lam_re.shape[3] == SSM_STATE
    assert ssm_w == groups * SSM_GROUP and groups % 2 == 0
    assert seq % GRID_W == 0 and (batch * seq) % ctx_len == 0 and ctx_len % LANES == 0
    a_w = 3 * conv_width + ssm_w
    dims = dict(
        batch=batch, seq=seq, ctx=ctx_len, d=d, n_lat=batch * seq, n_ctx=batch * ctx_len,
        conv_w=conv_width, ssm_w=ssm_w, attn_w=attn_w, kv_w=kv_w, gate_w=gate_w, pairs=groups // 2,
        o_q=gate_w, o_k=gate_w + attn_w, o_v=gate_w + attn_w + kv_w,
    )
    n_lat, nt = dims["n_lat"], dims["n_lat"] + dims["n_ctx"]

    xt = jnp.concatenate([x.reshape(n_lat, d), ctx.reshape(dims["n_ctx"], d)], axis=0)
    cc = jnp.concatenate([c, c_ctx[None, :], jnp.zeros((MOD_ROWS - batch - 1, d), F32)], axis=0)
    mod = _modulation(cc, w_mod, b_mod)
    modr = mod.reshape(depth, MOD_ROWS, 1, N_MOD * d)

    row_vec = lambda a: a.reshape(depth, 1, a.shape[-1])
    g_pre_mix, g_post_mix, g_pre_mlp, g_post_mlp = map(row_vec, (g_pre_mix, g_post_mix, g_pre_mlp, g_post_mlp))
    q_gain, k_gain = row_vec(q_gain), row_vec(k_gain)
    w_in_a = w_in[:, :, :a_w].astype(BF16)
    w_in_b = jnp.concatenate([w_in[:, :, in_w - gate_w:], w_in[:, :, a_w:in_w - gate_w]], axis=-1).astype(BF16)
    w_conv_out, w_glu, w_attn_out, w_out, w_up, w_down = (
        w.astype(BF16) for w in (w_conv_out, w_glu, w_attn_out, w_out, w_up, w_down))
    tn_a = _pick([a_w], [1280, 768, 512, 256, 128])
    tn_b = _pick([w_in_b.shape[-1]], [1536, 768, 512, 256, 128])

    qk_tile = _pick([seq, ctx_len], [256, 128, 64, 32])
    cos_t, sin_t = _rope_tables(seq, qk_tile)
    r1, v_mat, a16 = _ssm_parameters(ssm_lam_re, ssm_lam_im, ssm_log_dt, ssm_b_re, ssm_b_im, ssm_c_re, ssm_c_im)
    dsk = jnp.broadcast_to(
        ssm_d.astype(F32).reshape(depth, groups // 2, 2, 1, SSM_GROUP),
        (depth, groups // 2, 2, SSM_CHUNK, SSM_GROUP)).reshape(depth, groups // 2, 1, 2 * SSM_CHUNK * SSM_GROUP)

    h = _prenorm(xt, g_pre_mix, modr, 0, dims)
    for l in range(depth):
        need_ctx = l < depth - 1
        rows = nt if need_ctx else n_lat
        proj_a, u32 = _projection(h, w_in_a, l, dims, tn_a, f32_cols=(3 * conv_width, ssm_w))
        (proj_b,) = _projection(h, w_in_b, l, dims, tn_b)
        u = _ssm_pack(u32, dims)
        ys = _ssm_unpack(_ssm_core(u, r1, v_mat, a16, dsk, l, dims), dims)
        q, k = _qk_prepare(proj_b, cos_t, sin_t, q_gain, k_gain, l, dims)
        attn = _attention(q, k, proj_b, None, dims)
        if need_ctx:
            attn = _attention(q, k, proj_b, attn, dims)
        m = _merge(proj_a, proj_b, ys, attn, conv_w, w_conv_out, w_glu, w_attn_out, l, dims, rows)
        xt, h = _out_projection(m, w_out, xt, modr, g_post_mix, g_pre_mlp, l, dims)
        if need_ctx:
            xt, h = _mlp(xt, h, g_post_mlp, modr, w_up, w_down, l, dims, rows, g_next=g_pre_mix)
        else:
            (xt,) = _mlp(xt, h, g_post_mlp, modr, w_up, w_down, l, dims, rows)
    return xt.reshape(batch, seq, d)
```

```python
import functools
import math

import jax
import jax.numpy as jnp
from jax import lax
from jax.experimental import pallas as pl
from jax.experimental.pallas import tpu as pltpu

F32 = jnp.float32
BF16 = jnp.bfloat16

NORM_EPS = 1e-6
N_MOD = 6
HEAD_DIM = 128
N_KV_HEADS = 2
GRID_W = 64
ROPE_BASE = 10000.0
CONV_K = 3
SSM_GROUP = 16
SSM_STATE = 64
SSM_RE_MAX = -1e-4
SSM_CHUNK = 16
SUBLANES = 8
LANES = 128
HALO_ROWS = 16
MAX_KEY_CHUNK = 4224
MOD_ROWS = 8
VMEM_LIMIT = 56 * 2**20


def _params(sem, vmem=VMEM_LIMIT):
    return pltpu.CompilerParams(dimension_semantics=sem, vmem_limit_bytes=vmem)


def _pick(dims, cands):
    for c in cands:
        if all(d % c == 0 for d in dims):
            return c
    raise ValueError(f"no tile in {cands} divides {dims}")


def _rms(x, g):
    ms = jnp.mean(x * x, axis=-1, keepdims=True)
    return x * lax.rsqrt(ms + NORM_EPS) * g


def _mod_kernel(c_ref, w_ref, b_ref, o_ref):
    c = c_ref[...]
    s = (c * jax.nn.sigmoid(c)).astype(BF16)
    o_ref[...] = jnp.dot(s, w_ref[...].astype(BF16), preferred_element_type=F32) + b_ref[...]


def _modulation(cc, w_mod, b_mod):
    depth, d, nout = w_mod.shape
    tn = _pick([nout], [1024, 512, 256, 128])
    return pl.pallas_call(
        _mod_kernel,
        grid=(depth, nout // tn),
        in_specs=[
            pl.BlockSpec((MOD_ROWS, d), lambda l, j: (0, 0)),
            pl.BlockSpec((None, d, tn), lambda l, j: (l, 0, j)),
            pl.BlockSpec((None, 1, tn), lambda l, j: (l, 0, j)),
        ],
        out_specs=pl.BlockSpec((None, MOD_ROWS, tn), lambda l, j: (l, 0, j)),
        out_shape=jax.ShapeDtypeStruct((depth, MOD_ROWS, nout), F32),
        compiler_params=_params(("parallel", "parallel")),
        name="modulation",
    )(cc, w_mod, b_mod.reshape(depth, 1, nout))


def _norm_mod(x, g, shift, scale):
    return (_rms(x, g) * (1.0 + scale) + shift).astype(BF16)


def _prenorm_kernel(x_ref, g_ref, sh_ref, sc_ref, h_ref):
    h_ref[...] = _norm_mod(x_ref[...], g_ref[...], sh_ref[...], sc_ref[...])


def _row_group(dims, tm):
    return lambda i: jnp.minimum((i * tm) // dims["seq"], dims["batch"])


def _prenorm(xt, gain, modr, l, dims):
    nt, d = xt.shape
    tm = _pick([dims["seq"], dims["n_ctx"]], [512, 256, 128])
    grp = _row_group(dims, tm)
    return pl.pallas_call(
        _prenorm_kernel,
        grid=(nt // tm,),
        in_specs=[
            pl.BlockSpec((tm, d), lambda i: (i, 0)),
            pl.BlockSpec((None, 1, d), lambda i: (l, 0, 0)),
            pl.BlockSpec((None, None, 1, d), lambda i: (l, grp(i), 0, 0)),
            pl.BlockSpec((None, None, 1, d), lambda i: (l, grp(i), 0, 1)),
        ],
        out_specs=pl.BlockSpec((tm, d), lambda i: (i, 0)),
        out_shape=jax.ShapeDtypeStruct((nt, d), BF16),
        compiler_params=_params(("parallel",)),
        name="prenorm",
    )(xt, gain, modr, modr)


def _proj_kernel(h_ref, w_ref, o_ref, *u_refs, u_tile, u_off):
    r = jnp.dot(h_ref[...], w_ref[...], preferred_element_type=F32)
    o_ref[...] = r.astype(o_ref.dtype)
    if u_refs:
        (ou_ref,) = u_refs

        @pl.when(pl.program_id(1) == u_tile)
        def _():
            for t in range(ou_ref.shape[0]):
                ou_ref[t] = r[:, u_off + LANES * t:u_off + LANES * (t + 1)]


def _projection(h, w, l, dims, tn, f32_cols=None):
    nt, d = h.shape
    n_out = w.shape[-1]
    tm = _pick([dims["seq"], dims["n_ctx"]], [1024, 512, 256, 128])
    assert n_out % tn == 0
    out_specs = [pl.BlockSpec((tm, tn), lambda i, j: (i, j))]
    out_shape = [jax.ShapeDtypeStruct((nt, n_out), BF16)]
    u_tile = u_off = 0
    if f32_cols is not None:
        start, width = f32_cols
        u_tile, u_off = start // tn, start % tn
        assert u_off + width <= tn and u_off % LANES == 0 and width % LANES == 0
        out_specs.append(pl.BlockSpec((width // LANES, tm, LANES), lambda i, j: (0, i, 0)))
        out_shape.append(jax.ShapeDtypeStruct((width // LANES, nt, LANES), F32))
    return pl.pallas_call(
        functools.partial(_proj_kernel, u_tile=u_tile, u_off=u_off),
        grid=(nt // tm, n_out // tn),
        in_specs=[
            pl.BlockSpec((tm, d), lambda i, j: (i, 0)),
            pl.BlockSpec((None, d, tn), lambda i, j: (l, 0, j)),
        ],
        out_specs=out_specs,
        out_shape=out_shape,
        compiler_params=_params(("parallel", "arbitrary")),
        name="in_projection",
    )(h, w)


def _head_permutation():
    quarter = HEAD_DIM // 4
    return jnp.concatenate([jnp.arange(quarter * k, quarter * (k + 1)) for k in (0, 2, 1, 3)])


def _qk_kernel(xq_ref, xk_ref, cos_ref, sin_ref, qg_ref, kg_ref, q_ref, k_ref, *, n_q, n_kv, scale):
    cos = cos_ref[...]
    sin = sin_ref[...]
    mean_cols = jnp.full((HEAD_DIM, HEAD_DIM), 1.0 / HEAD_DIM, BF16)
    for h in range(n_q + n_kv):
        if h < n_q:
            xh = xq_ref[:, HEAD_DIM * h:HEAD_DIM * (h + 1)].astype(F32)
        else:
            xh = xk_ref[:, HEAD_DIM * (h - n_q):HEAD_DIM * (h - n_q + 1)].astype(F32)
        gain = qg_ref[...] * scale if h < n_q else kg_ref[...]
        sq = xh * xh
        hi = sq.astype(BF16)
        lo = (sq - hi.astype(F32)).astype(BF16)
        ms = (jnp.dot(hi, mean_cols, preferred_element_type=F32)
              + jnp.dot(lo, mean_cols, preferred_element_type=F32))
        y = xh * lax.rsqrt(ms + NORM_EPS) * gain
        r = (y * cos + pltpu.roll(y, HEAD_DIM // 2, 1) * sin).astype(BF16)
        if h < n_q:
            q_ref[:, HEAD_DIM * h:HEAD_DIM * (h + 1)] = r
        else:
            k_ref[:, HEAD_DIM * (h - n_q):HEAD_DIM * (h - n_q + 1)] = r


def _qk_prepare(proj, cos_t, sin_t, q_gain, k_gain, l, dims):
    nt = proj.shape[0]
    attn_w, kv_w = dims["attn_w"], dims["kv_w"]
    tm = _pick([dims["seq"], dims["ctx"]], [256, 128, 64, 32])
    assert dims["o_q"] % attn_w == 0 and dims["o_k"] % kv_w == 0
    n_lat = dims["n_lat"] // tm
    n_seq = dims["seq"] // tm
    tab = lambda i: (jnp.where(i < n_lat, i % n_seq, n_seq), 0)
    return pl.pallas_call(
        functools.partial(_qk_kernel, n_q=attn_w // HEAD_DIM, n_kv=kv_w // HEAD_DIM,
                          scale=HEAD_DIM ** -0.5 * math.log2(math.e)),
        grid=(nt // tm,),
        in_specs=[
            pl.BlockSpec((tm, attn_w), lambda i: (i, dims["o_q"] // attn_w)),
            pl.BlockSpec((tm, kv_w), lambda i: (i, dims["o_k"] // kv_w)),
            pl.BlockSpec((tm, HEAD_DIM), tab),
            pl.BlockSpec((tm, HEAD_DIM), tab),
            pl.BlockSpec((None, 1, HEAD_DIM), lambda i: (l, 0, 0)),
            pl.BlockSpec((None, 1, HEAD_DIM), lambda i: (l, 0, 0)),
        ],
        out_specs=[
            pl.BlockSpec((tm, attn_w), lambda i: (i, 0)),
            pl.BlockSpec((tm, kv_w), lambda i: (i, 0)),
        ],
        out_shape=[
            jax.ShapeDtypeStruct((nt, attn_w), BF16),
            jax.ShapeDtypeStruct((nt, kv_w), BF16),
        ],
        compiler_params=_params(("parallel",)),
        name="qk_prepare",
    )(proj, proj, cos_t, sin_t, q_gain, k_gain)


def _rope_tables(seq, pad_rows):
    rows = seq // GRID_W
    row = jnp.repeat(jnp.arange(rows), GRID_W)
    col = jnp.tile(jnp.arange(GRID_W), rows)
    half = HEAD_DIM // 4
    inv_freq = ROPE_BASE ** (-jnp.arange(half, dtype=F32) / half)
    ang_r = row.astype(F32)[:, None] * inv_freq
    ang_c = col.astype(F32)[:, None] * inv_freq
    cr, sr, cc, sc = jnp.cos(ang_r), jnp.sin(ang_r), jnp.cos(ang_c), jnp.sin(ang_c)
    cos_t = jnp.concatenate([cr, cc, cr, cc], axis=-1)
    sin_t = jnp.concatenate([-sr, -sc, sr, sc], axis=-1)
    cos_t = jnp.concatenate([cos_t, jnp.ones((pad_rows, HEAD_DIM), F32)], axis=0)
    sin_t = jnp.concatenate([sin_t, jnp.zeros((pad_rows, HEAD_DIM), F32)], axis=0)
    return cos_t, sin_t


def _attn_kernel(*refs, tq, tk, group, n_src):
    q_ref = refs[0]
    k_refs, v_refs = refs[1:1 + 2 * n_src:2], refs[2:2 + 2 * n_src:2]
    o_ref, ke_scr, ve_scr, s_scr, m_scr, acc_scr = refs[-6:]

    @pl.when(pl.program_id(2) == 0)
    def _():
        r0 = 0
        for k_ref, v_ref in zip(k_refs, v_refs):
            n = k_ref.shape[0]
            ke_scr[r0:r0 + n, :] = k_ref[...]
            ve_scr[r0:r0 + n, :HEAD_DIM] = v_ref[...]
            r0 += n
        ve_scr[:, HEAD_DIM:] = jnp.ones((ve_scr.shape[0], HEAD_DIM), BF16)

    q = jnp.concatenate([q_ref[:, HEAD_DIM * g:HEAD_DIM * (g + 1)] for g in range(group)], axis=0)
    n_chunks = s_scr.shape[0]

    one = jnp.minimum(pl.program_id(2) + 1, 1)

    def first(_, carry):
        m = None
        for c in range(n_chunks):
            s = lax.dot_general(q, ke_scr[tk * c:tk * (c + 1), :], (((1,), (1,)), ((), ())),
                                preferred_element_type=F32)
            s_scr[c] = s
            for t in range(tk // LANES):
                slab = s[:, LANES * t:LANES * (t + 1)]
                m = slab if m is None else jnp.maximum(m, slab)
        m_scr[...] = m
        return carry

    lax.fori_loop(0, one, first, 0)
    m_scr[...] = jnp.broadcast_to(jnp.max(m_scr[...], axis=-1, keepdims=True), m_scr.shape)

    def second(_, carry):
        m_rep = m_scr[...]
        acc = None
        for c in range(n_chunks):
            s = s_scr[c]
            p = jnp.concatenate(
                [jnp.exp2(s[:, LANES * t:LANES * (t + 1)] - m_rep).astype(BF16) for t in range(tk // LANES)],
                axis=1)
            part = jnp.dot(p, ve_scr[tk * c:tk * (c + 1), :], preferred_element_type=F32)
            acc = part if acc is None else acc + part
        acc_scr[...] = acc
        return carry

    lax.fori_loop(0, one, second, 0)
    acc = acc_scr[...]
    o = acc[:, :HEAD_DIM] / acc[:, HEAD_DIM:]
    for g in range(group):
        o_ref[:, HEAD_DIM * g:HEAD_DIM * (g + 1)] = o[tq * g:tq * (g + 1)].astype(o_ref.dtype)


def _attention(q, k, main, prev, dims):
    nt = q.shape[0]
    attn_w, seq, ctx, batch = dims["attn_w"], dims["seq"], dims["ctx"], dims["batch"]
    latent = prev is None
    group = attn_w // HEAD_DIM // N_KV_HEADS
    gw = group * HEAD_DIM
    v_blk = dims["o_v"] // HEAD_DIM
    lq = seq if latent else ctx
    tq = _pick([lq], [128, 64, 32, 16])
    n_keys = ctx + seq if latent else ctx
    tk = max(c for c in range(LANES, min(n_keys, MAX_KEY_CHUNK) + 1, LANES) if n_keys % c == 0)
    row0 = 0 if latent else dims["n_lat"] // tq
    ctx0 = dims["n_lat"] // ctx
    rows = group * tq
    in_specs = [
        pl.BlockSpec((tq, gw), lambda b, h, i: (row0 + b * (lq // tq) + i, h)),
        pl.BlockSpec((ctx, HEAD_DIM), lambda b, h, i: (ctx0 + b, h)),
        pl.BlockSpec((ctx, HEAD_DIM), lambda b, h, i: (ctx0 + b, v_blk + h)),
    ]
    args = [q, k, main]
    aliases = {}
    if latent:
        in_specs += [
            pl.BlockSpec((seq, HEAD_DIM), lambda b, h, i: (b, h)),
            pl.BlockSpec((seq, HEAD_DIM), lambda b, h, i: (b, v_blk + h)),
        ]
        args += [k, main]
    n_src = (len(args) - 1) // 2
    if not latent:
        in_specs.append(pl.BlockSpec(memory_space=pl.ANY))
        args.append(prev)
        aliases = {len(args) - 1: 0}
    scratch = [pltpu.VMEM((n_keys, HEAD_DIM), BF16), pltpu.VMEM((n_keys, 2 * HEAD_DIM), BF16),
               pltpu.VMEM((n_keys // tk, rows, tk), F32), pltpu.VMEM((rows, LANES), F32),
               pltpu.VMEM((rows, 2 * HEAD_DIM), F32)]
    return pl.pallas_call(
        functools.partial(_attn_kernel, tq=tq, tk=tk, group=group, n_src=n_src),
        grid=(batch, N_KV_HEADS, lq // tq),
        in_specs=in_specs,
        out_specs=pl.BlockSpec((tq, gw), lambda b, h, i: (row0 + b * (lq // tq) + i, h)),
        out_shape=jax.ShapeDtypeStruct((nt, attn_w), BF16),
        scratch_shapes=scratch,
        input_output_aliases=aliases,
        compiler_params=_params(("parallel", "parallel", "arbitrary")),
        name="attention_latent" if latent else "attention_context",
    )(*args)


def _ssm_params_kernel(lrr_ref, lir_ref, ldr_ref, lrc_ref, lic_ref, ldc_ref, btr_ref, bti_ref,
                       ctr_ref, cti_ref, r1_ref, v_ref, a_ref):
    t = SSM_CHUNK
    pw = SSM_GROUP * t
    r1_ref[...] = jnp.zeros(r1_ref.shape, r1_ref.dtype)
    v_ref[...] = jnp.zeros(v_ref.shape, v_ref.dtype)
    lane = lax.broadcasted_iota(jnp.int32, (SSM_GROUP, LANES), 1)
    col = lax.broadcasted_iota(jnp.int32, (SSM_GROUP, pw), 1)
    tau_col = (lax.broadcasted_iota(jnp.int32, (LANES, pw), 1) // SSM_GROUP).astype(F32)
    tau_row = lax.broadcasted_iota(jnp.int32, (t, LANES), 0).astype(F32)
    masks = (lane < SSM_STATE, lane >= SSM_STATE)
    toep = []
    for d in range(2):
        lr = jnp.minimum(lrr_ref[d], SSM_RE_MAX)
        li = lir_ref[d]
        dt = jnp.exp(ldr_ref[d])
        mag = jnp.exp(lr * dt)
        ab_re = mag * jnp.cos(li * dt)
        ab_im = mag * jnp.sin(li * dt)
        nr = ab_re - 1.0
        den = lr * lr + li * li
        f_re = (nr * lr + ab_im * li) / den
        f_im = (ab_im * lr - nr * li) / den
        bb_re = f_re * btr_ref[d] - f_im * bti_ref[d]
        bb_im = f_re * bti_ref[d] + f_im * btr_ref[d]
        e16 = jnp.exp(lr * dt * float(t))
        a_ref[2 * d:2 * d + 1, :] = e16 * jnp.cos(li * dt * float(t))
        a_ref[2 * d + 1:2 * d + 2, :] = e16 * jnp.sin(li * dt * float(t))
        tau_s = (t - 1.0) - tau_row if d == 0 else tau_row
        es = jnp.exp(lr * dt * tau_s)
        pw_re = es * jnp.cos(li * dt * tau_s)
        pw_im = es * jnp.sin(li * dt * tau_s)
        for s in range(t):
            ar = pw_re[s:s + 1]
            ai = pw_im[s:s + 1]
            w_re = ar * bb_re - ai * bb_im
            w_im = ar * bb_im + ai * bb_re
            for gi in range(2):
                r0 = gi * pw + SSM_GROUP * s
                c0 = 2 * pw + 2 * LANES * d
                r1_ref[r0:r0 + SSM_GROUP, c0:c0 + LANES] = jnp.where(masks[gi], w_re, 0.0).astype(r1_ref.dtype)
                r1_ref[r0:r0 + SSM_GROUP, c0 + LANES:c0 + 2 * LANES] = (
                    jnp.where(masks[gi], w_im, 0.0).astype(r1_ref.dtype))
        lrc = jnp.minimum(lrc_ref[d], SSM_RE_MAX)
        lic = lic_ref[d]
        dtc = jnp.exp(ldc_ref[d])
        tau_q = tau_col if d == 0 else (t - 1.0) - tau_col
        eq = jnp.exp(lrc * dtc * tau_q)
        p_re = eq * jnp.cos(lic * dtc * tau_q)
        p_im = eq * jnp.sin(lic * dtc * tau_q)
        c_re = ctr_ref[d]
        c_im = cti_ref[d]
        q_re = p_re * c_re - p_im * c_im
        q_im = p_re * c_im + p_im * c_re
        per_group = []
        for gi in range(2):
            k_mat = (jnp.dot(jnp.where(masks[gi], bb_re, 0.0), q_re, precision=lax.Precision.HIGHEST,
                             preferred_element_type=F32)
                     - jnp.dot(jnp.where(masks[gi], bb_im, 0.0), q_im, precision=lax.Precision.HIGHEST,
                               preferred_element_type=F32))
            per_group.append(k_mat)
        toep.append(per_group)
        magc = jnp.exp(lrc * dtc)
        abc_re = magc * jnp.cos(lic * dtc)
        abc_im = magc * jnp.sin(lic * dtc)
        q1_re = abc_re * q_re - abc_im * q_im
        q1_im = abc_re * q_im + abc_im * q_re
        for gi in range(2):
            rows = slice(SSM_STATE * gi, SSM_STATE * (gi + 1))
            r0 = 2 * LANES * d + SSM_STATE * gi
            v_ref[r0:r0 + SSM_STATE, pw * gi:pw * (gi + 1)] = q1_re[rows].astype(v_ref.dtype)
            v_ref[r0 + LANES:r0 + LANES + SSM_STATE, pw * gi:pw * (gi + 1)] = (-q1_im[rows]).astype(v_ref.dtype)
    for gi in range(2):
        k_f, k_b = toep[0][gi], toep[1][gi]
        for s in range(t):
            sh = SSM_GROUP * s
            f = k_f if sh == 0 else jnp.where(col >= sh, pltpu.roll(k_f, sh, 1), 0.0)
            back = SSM_GROUP * (t - 1 - s)
            b = k_b if back == 0 else jnp.where(col < pw - back, pltpu.roll(k_b, pw - back, 1), 0.0)
            r0 = gi * pw + sh
            r1_ref[r0:r0 + SSM_GROUP, pw * gi:pw * (gi + 1)] = (f + b).astype(r1_ref.dtype)


def _pair_rows(a):
    depth, _, groups, n = a.shape
    return a.reshape(depth, 2, groups // 2, 2 * n).transpose(0, 2, 1, 3)


def _ssm_parameters(lam_re, lam_im, log_dt, b_re, b_im, c_re, c_im):
    depth, _, groups, n = lam_re.shape
    pairs = groups // 2
    t = SSM_CHUNK
    pw = SSM_GROUP * t
    ld = jnp.broadcast_to(log_dt[..., None], lam_re.shape)
    rows = [_pair_rows(a.astype(F32))[:, :, :, None, :] for a in (lam_re, lam_im, ld)]
    cols = [_pair_rows(a.astype(F32))[:, :, :, :, None] for a in (lam_re, lam_im, ld)]

    def bt(b):
        return (b.astype(F32).reshape(depth, 2, pairs, 2, n, SSM_GROUP).transpose(0, 2, 1, 5, 3, 4)
                .reshape(depth, pairs, 2, SSM_GROUP, 2 * n))

    def ct(c):
        c = (c.astype(F32).reshape(depth, 2, pairs, 2, SSM_GROUP, n).transpose(0, 2, 1, 3, 5, 4)
             .reshape(depth, pairs, 2, 2 * n, SSM_GROUP))
        return jnp.tile(c, (1, 1, 1, 1, t))

    row_spec = pl.BlockSpec((None, None, 2, 1, 2 * n), lambda l, j: (l, j, 0, 0, 0))
    col_spec = pl.BlockSpec((None, None, 2, 2 * n, 1), lambda l, j: (l, j, 0, 0, 0))
    bt_spec = pl.BlockSpec((None, None, 2, SSM_GROUP, 2 * n), lambda l, j: (l, j, 0, 0, 0))
    ct_spec = pl.BlockSpec((None, None, 2, 2 * n, pw), lambda l, j: (l, j, 0, 0, 0))
    return pl.pallas_call(
        _ssm_params_kernel,
        grid=(depth, pairs),
        in_specs=[row_spec] * 3 + [col_spec] * 3 + [bt_spec] * 2 + [ct_spec] * 2,
        out_specs=[
            pl.BlockSpec((None, None, 2 * pw, 4 * pw), lambda l, j: (l, j, 0, 0)),
            pl.BlockSpec((None, None, 8 * n, 2 * pw), lambda l, j: (l, j, 0, 0)),
            pl.BlockSpec((None, 4, 2 * n), lambda l, j: (l, 0, j)),
        ],
        out_shape=[
            jax.ShapeDtypeStruct((depth, pairs, 2 * pw, 4 * pw), BF16),
            jax.ShapeDtypeStruct((depth, pairs, 8 * n, 2 * pw), BF16),
            jax.ShapeDtypeStruct((depth, 4, groups * n), F32),
        ],
        compiler_params=_params(("parallel", "parallel")),
        name="ssm_parameters",
    )(*rows, *cols, bt(b_re), bt(b_im), ct(c_re), ct(c_im))


def _chunk_rows(u_ref):
    return jnp.concatenate([u_ref[t] for t in range(u_ref.shape[0])], axis=-1)


def _scan_chunk_states(s_ref, a_ref, h_ref, *, batch, lat_tiles, ctx_tiles):
    w = s_ref.shape[-1]
    per_tile = SUBLANES // batch
    row_grp = lax.broadcasted_iota(jnp.int32, (SUBLANES, w), 0) // batch
    zero = jnp.zeros((SUBLANES, w), F32)

    def make_tile(k, order, shift):
        ar = a_ref[k:k + 1, :]
        ai = a_ref[k + 1:k + 2, :]

        def tile(ti, carry):
            cur_re, cur_im = carry
            rows = pl.ds(pl.multiple_of(ti * SUBLANES, SUBLANES), SUBLANES)
            s_re = s_ref[k, rows, :]
            s_im = s_ref[k + 1, rows, :]
            h_re, h_im = zero, zero
            for pos in order:
                sel = row_grp == pos
                h_re = jnp.where(sel, cur_re, h_re)
                h_im = jnp.where(sel, cur_im, h_im)
                nxt_re = ar * cur_re - ai * cur_im + s_re
                nxt_im = ar * cur_im + ai * cur_re + s_im
                if batch == SUBLANES:
                    cur_re, cur_im = nxt_re, nxt_im
                else:
                    cur_re = pltpu.roll(nxt_re, shift, 0)
                    cur_im = pltpu.roll(nxt_im, shift, 0)
            h_ref[k, rows, :] = h_re
            h_ref[k + 1, rows, :] = h_im
            return cur_re, cur_im

        return tile

    fwd = make_tile(0, range(per_tile), batch)
    carry = lax.fori_loop(0, ctx_tiles, lambda i, c: fwd(lat_tiles + i, c), (zero, zero))
    lax.fori_loop(0, lat_tiles, fwd, carry)
    bwd = make_tile(2, range(per_tile - 1, -1, -1), SUBLANES - batch)
    carry = lax.fori_loop(0, ctx_tiles, lambda i, c: bwd(lat_tiles + ctx_tiles - 1 - i, c), (zero, zero))
    lax.fori_loop(0, lat_tiles, lambda i, c: bwd(lat_tiles - 1 - i, c), carry)


def _ssm_core_kernel(u_ref, r1_ref, v_ref, a_ref, d_ref, y_ref, s_scr, h_scr, *, batch, lat_tiles, ctx_tiles):
    n_t = u_ref.shape[0]
    kw = n_t * LANES
    u = _chunk_rows(u_ref)
    r = jnp.dot(u, r1_ref[...], preferred_element_type=F32)
    for t in range(n_t):
        y_ref[t] = r[:, LANES * t:LANES * (t + 1)]
    for k in range(4):
        s_scr[k] = r[:, kw + LANES * k:kw + LANES * (k + 1)]
    _scan_chunk_states(s_scr, a_ref, h_scr, batch=batch, lat_tiles=lat_tiles, ctx_tiles=ctx_tiles)
    h = jnp.concatenate([h_scr[k] for k in range(4)], axis=-1).astype(BF16)
    y = jnp.dot(h, v_ref[...], preferred_element_type=F32) + d_ref[...] * u.astype(F32)
    for t in range(n_t):
        y_ref[t] += y[:, LANES * t:LANES * (t + 1)]


def _ssm_core(u, r1, v, a16, dsk, l, dims):
    pairs, n_t, nc, _ = u.shape
    kw = n_t * LANES
    batch = dims["batch"]
    lat_tiles = dims["seq"] // SSM_CHUNK * batch // SUBLANES
    ctx_tiles = dims["ctx"] // SSM_CHUNK * batch // SUBLANES
    return pl.pallas_call(
        functools.partial(_ssm_core_kernel, batch=batch, lat_tiles=lat_tiles, ctx_tiles=ctx_tiles),
        grid=(pairs,),
        in_specs=[
            pl.BlockSpec((None, n_t, nc, LANES), lambda j: (j, 0, 0, 0)),
            pl.BlockSpec((None, None, kw, 2 * kw), lambda j: (l, j, 0, 0)),
            pl.BlockSpec((None, None, 4 * LANES, kw), lambda j: (l, j, 0, 0)),
            pl.BlockSpec((None, 4, LANES), lambda j: (l, 0, j)),
            pl.BlockSpec((None, None, 1, kw), lambda j: (l, j, 0, 0)),
        ],
        out_specs=pl.BlockSpec((None, n_t, nc, LANES), lambda j: (j, 0, 0, 0)),
        out_shape=jax.ShapeDtypeStruct((pairs, n_t, nc, LANES), F32),
        scratch_shapes=[pltpu.VMEM((4, nc, LANES), F32), pltpu.VMEM((4, nc, LANES), F32)],
        compiler_params=_params(("parallel",)),
        name="ssm_core",
    )(u, r1, v, a16, dsk)


GROUPS_PER_TILE = LANES // SSM_GROUP


def _lane_slot(rows):
    return lax.broadcasted_iota(jnp.int32, (rows, LANES), 1) // SSM_GROUP


def _block_transposes(sets, slot):
    sets = [list(t) for t in sets]
    d = GROUPS_PER_TILE // 2
    while d:
        low = (slot & d) == 0
        for t in sets:
            for a in range(GROUPS_PER_TILE):
                if not a & d:
                    lo, hi = t[a], t[a + d]
                    t[a] = jnp.where(low, lo, pltpu.roll(hi, d * SSM_GROUP, 1))
                    t[a + d] = jnp.where(low, pltpu.roll(lo, LANES - d * SSM_GROUP, 1), hi)
        d //= 2
    return sets


def _chunk_tiling(dims):
    ncl, ncc = dims["seq"] // SSM_CHUNK, dims["ctx"] // SSM_CHUNK
    ct = _pick([ncl, ncc], [16, 8, 4, 2, 1])
    nlt, nct = ncl // ct, ncc // ct

    def token_block(k, b):
        return jnp.where(k < nlt, b * nlt + k, dims["batch"] * nlt + b * nct + (k - nlt))

    return ct, nlt + nct, token_block


def _ssm_pack_kernel(*refs, batch, ct):
    x_refs, o_ref, scr = refs[:batch], refs[batch], refs[batch + 1]
    slot = _lane_slot(ct)
    halves = SSM_CHUNK // GROUPS_PER_TILE
    for b in range(batch):
        where = [(j, half) for j in range(x_refs[b].shape[0]) for half in range(halves)]
        sets = [[x_refs[b][j, pl.ds(GROUPS_PER_TILE * half + k, ct, stride=SSM_CHUNK), :]
                 for k in range(GROUPS_PER_TILE)] for j, half in where]
        for (j, half), outs in zip(where, _block_transposes(sets, slot)):
            for gg, out in enumerate(outs):
                g = GROUPS_PER_TILE * j + gg
                scr[g // 2, (g % 2) * halves + half, pl.ds(b, ct, stride=batch), :] = out
    o_ref[...] = scr[...].astype(o_ref.dtype)


def _ssm_pack(u32, dims):
    batch, pairs = dims["batch"], dims["pairs"]
    ct, n_tiles, token_block = _chunk_tiling(dims)
    n_in, nt, _ = u32.shape
    n_t = 2 * SSM_CHUNK * SSM_GROUP // LANES
    return pl.pallas_call(
        functools.partial(_ssm_pack_kernel, batch=batch, ct=ct),
        grid=(n_tiles,),
        in_specs=[pl.BlockSpec((n_in, ct * SSM_CHUNK, LANES),
                               functools.partial(lambda k, b: (0, token_block(k, b), 0), b=b))
                  for b in range(batch)],
        out_specs=pl.BlockSpec((pairs, n_t, ct * batch, LANES), lambda k: (0, 0, k, 0)),
        out_shape=jax.ShapeDtypeStruct((pairs, n_t, nt // SSM_CHUNK, LANES), BF16),
        scratch_shapes=[pltpu.VMEM((pairs, n_t, ct * batch, LANES), F32)],
        compiler_params=_params(("parallel",)),
        name="ssm_pack",
    )(*([u32] * batch))


def _ssm_unpack_kernel(y_ref, o_ref, *, batch, ct):
    b = pl.program_id(1)
    slot = _lane_slot(ct)
    halves = SSM_CHUNK // GROUPS_PER_TILE
    where = [(j, half) for j in range(o_ref.shape[0]) for half in range(halves)]
    sets = []
    for j, half in where:
        groups = [GROUPS_PER_TILE * j + gg for gg in range(GROUPS_PER_TILE)]
        sets.append([y_ref[g // 2, (g % 2) * halves + half, pl.ds(b, ct, stride=batch), :] for g in groups])
    for (j, half), outs in zip(where, _block_transposes(sets, slot)):
        for k, out in enumerate(outs):
            o_ref[j, pl.ds(GROUPS_PER_TILE * half + k, ct, stride=SSM_CHUNK), :] = out


def _ssm_unpack(y, dims):
    batch = dims["batch"]
    pairs, n_t, nc, _ = y.shape
    ct, n_tiles, token_block = _chunk_tiling(dims)
    n_out = dims["ssm_w"] // LANES
    return pl.pallas_call(
        functools.partial(_ssm_unpack_kernel, batch=batch, ct=ct),
        grid=(n_tiles, batch),
        in_specs=[pl.BlockSpec((pairs, n_t, ct * batch, LANES), lambda k, b: (0, 0, k, 0))],
        out_specs=pl.BlockSpec((n_out, ct * SSM_CHUNK, LANES), lambda k, b: (0, token_block(k, b), 0)),
        out_shape=jax.ShapeDtypeStruct((n_out, nc * SSM_CHUNK, LANES), F32),
        compiler_params=_params(("parallel", "arbitrary")),
        name="ssm_unpack",
    )(y)


def _merge_kernel(pb_ref, pc_ref, pv_ref, hpc_ref, hpv_ref, hnc_ref, hnv_ref, cw_ref, ys_ref, at_ref,
                  g_ref, wc_ref, wg_ref, wa_ref, o_ref, *, tm, tn, seq, ctx, n_lat):
    d = o_ref.shape[1]
    r0 = pl.program_id(0) * tm
    period = jnp.where(r0 >= n_lat, ctx, seq)
    starts = r0 % period == 0
    ends = (r0 + tm) % period == 0
    w = pc_ref[...].astype(F32) * pv_ref[...].astype(F32)
    prev = (hpc_ref[...].astype(F32) * hpv_ref[...].astype(F32))[HALO_ROWS - 1:HALO_ROWS]
    nxt = (hnc_ref[...].astype(F32) * hnv_ref[...].astype(F32))[0:1]
    prev = jnp.where(starts, 0.0, prev)
    nxt = jnp.where(ends, 0.0, nxt)
    row = lax.broadcasted_iota(jnp.int32, w.shape, 0)
    up = jnp.where(row == 0, prev, pltpu.roll(w, 1, 0))
    dn = jnp.where(row == tm - 1, nxt, pltpu.roll(w, tm - 1, 0))
    cw = cw_ref[...]
    cf = (pb_ref[...].astype(F32) * (cw[0:1] * up + cw[1:2] * w + cw[2:3] * dn)).astype(BF16)
    yg = jax.nn.gelu(_chunk_rows(ys_ref)).astype(BF16)
    at = at_ref[...]
    for c in range(d // tn):
        cols = slice(tn * c, tn * (c + 1))
        gate = lambda k: jax.nn.sigmoid(g_ref[:, k * d + tn * c:k * d + tn * (c + 1)].astype(F32))
        ya = jnp.dot(cf, wc_ref[:, cols], preferred_element_type=F32)
        yc = jnp.dot(at, wa_ref[:, cols], preferred_element_type=F32)
        ys = jnp.dot(yg, wg_ref[:, cols], preferred_element_type=F32) * jax.nn.sigmoid(
            jnp.dot(yg, wg_ref[:, d + tn * c:d + tn * (c + 1)], preferred_element_type=F32))
        o_ref[:, cols] = (gate(0) * ya + gate(1) * ys + gate(2) * yc).astype(o_ref.dtype)


def _merge(main, gates, ys, attn, conv_w, w_conv_out, w_glu, w_attn_out, l, dims, rows):
    d = dims["d"]
    cw_, sw, aw = dims["conv_w"], dims["ssm_w"], dims["attn_w"]
    tm = _pick([dims["seq"], dims["ctx"]], [512, 256, 128, 64, 32])
    tn = _pick([d], [512, 256, 128])
    n_halo = main.shape[0] // HALO_ROWS
    per = tm // HALO_ROWS
    prev_blk = lambda i: jnp.maximum(i * per - 1, 0)
    next_blk = lambda i: jnp.minimum((i + 1) * per, n_halo - 1)
    resident = lambda shape: pl.BlockSpec((None,) + shape, lambda i: (l, 0, 0), pipeline_mode=pl.Buffered(1))
    return pl.pallas_call(
        functools.partial(_merge_kernel, tm=tm, tn=tn, seq=dims["seq"], ctx=dims["ctx"], n_lat=dims["n_lat"]),
        grid=(rows // tm,),
        in_specs=[
            pl.BlockSpec((tm, cw_), lambda i: (i, 0)),
            pl.BlockSpec((tm, cw_), lambda i: (i, 1)),
            pl.BlockSpec((tm, cw_), lambda i: (i, 2)),
            pl.BlockSpec((HALO_ROWS, cw_), lambda i: (prev_blk(i), 1)),
            pl.BlockSpec((HALO_ROWS, cw_), lambda i: (prev_blk(i), 2)),
            pl.BlockSpec((HALO_ROWS, cw_), lambda i: (next_blk(i), 1)),
            pl.BlockSpec((HALO_ROWS, cw_), lambda i: (next_blk(i), 2)),
            resident((CONV_K, cw_)),
            pl.BlockSpec((sw // LANES, tm, LANES), lambda i: (0, i, 0)),
            pl.BlockSpec((tm, aw), lambda i: (i, 0)),
            pl.BlockSpec((tm, 3 * d), lambda i: (i, 0)),
            resident((cw_, d)),
            resident((sw, 2 * d)),
            resident((aw, d)),
        ],
        out_specs=pl.BlockSpec((tm, d), lambda i: (i, 0)),
        out_shape=jax.ShapeDtypeStruct((rows, d), BF16),
        compiler_params=_params(("parallel",)),
        name="branch_merge",
    )(main, main, main, main, main, main, main, conv_w, ys, attn, gates, w_conv_out, w_glu, w_attn_out)


def _outproj_kernel(m_ref, w_ref, x_ref, gate_ref, g_ref, gn_ref, shn_ref, scn_ref, o_ref, h_ref):
    r = jnp.dot(m_ref[...], w_ref[...], preferred_element_type=F32)
    x = x_ref[...] + gate_ref[...] * _rms(r, g_ref[...])
    o_ref[...] = x
    h_ref[...] = _norm_mod(x, gn_ref[...], shn_ref[...], scn_ref[...])


def _out_projection(m, w_out, xt, modr, g_post, g_pre_mlp, l, dims):
    rows, d = m.shape
    tm = _pick([dims["seq"], dims["n_ctx"]], [512, 256, 128])
    grp = _row_group(dims, tm)
    mod = lambda k: pl.BlockSpec((None, None, 1, d), lambda i: (l, grp(i), 0, k))
    vec = pl.BlockSpec((None, 1, d), lambda i: (l, 0, 0))
    return pl.pallas_call(
        _outproj_kernel,
        grid=(rows // tm,),
        in_specs=[
            pl.BlockSpec((tm, d), lambda i: (i, 0)),
            pl.BlockSpec((None, d, d), lambda i: (l, 0, 0)),
            pl.BlockSpec((tm, d), lambda i: (i, 0)),
            mod(2), vec, vec, mod(3), mod(4),
        ],
        out_specs=[pl.BlockSpec((tm, d), lambda i: (i, 0)), pl.BlockSpec((tm, d), lambda i: (i, 0))],
        out_shape=[jax.ShapeDtypeStruct(xt.shape, F32), jax.ShapeDtypeStruct((rows, d), BF16)],
        input_output_aliases={2: 0},
        compiler_params=_params(("parallel",)),
        name="out_projection",
    )(m, w_out, xt, modr, g_post, g_pre_mlp, modr, modr)


def _mlp_kernel(x_ref, h_ref, wu_ref, wd_ref, gate_ref, gpost_ref, *rest, with_next):
    j = pl.program_id(1)
    if with_next:
        gn_ref, shn_ref, scn_ref, o_ref, hn_ref, acc_scr = rest
    else:
        o_ref, acc_scr = rest

    @pl.when(j == 0)
    def _():
        acc_scr[...] = jnp.zeros(acc_scr.shape, F32)

    u = jnp.maximum(jnp.dot(h_ref[...], wu_ref[...], preferred_element_type=F32), 0.0)
    acc_scr[...] += jnp.dot((u * u).astype(BF16), wd_ref[...], preferred_element_type=F32)

    @pl.when(j == pl.num_programs(1) - 1)
    def _():
        x = x_ref[...] + gate_ref[...] * _rms(acc_scr[...], gpost_ref[...])
        o_ref[...] = x
        if with_next:
            hn_ref[...] = _norm_mod(x, gn_ref[...], shn_ref[...], scn_ref[...])


def _mlp(xt, h, g_post, modr, w_up, w_down, l, dims, rows, g_next=None):
    d = xt.shape[1]
    d_ff = w_up.shape[-1]
    tm = _pick([dims["seq"], dims["n_ctx"]], [512, 256, 128])
    tf = _pick([d_ff], [1024, 512, 256, 128])
    grp = _row_group(dims, tm)
    mod = lambda layer, k: pl.BlockSpec((None, None, 1, d), lambda i, j: (layer, grp(i), 0, k))
    vec = lambda layer: pl.BlockSpec((None, 1, d), lambda i, j: (layer, 0, 0))
    row_tile = pl.BlockSpec((tm, d), lambda i, j: (i, 0))
    in_specs = [
        row_tile, row_tile,
        pl.BlockSpec((None, d, tf), lambda i, j: (l, 0, j)),
        pl.BlockSpec((None, tf, d), lambda i, j: (l, j, 0)),
        mod(l, 5), vec(l),
    ]
    args = [xt, h, w_up, w_down, modr, g_post]
    out_specs, out_shape = [row_tile], [jax.ShapeDtypeStruct((rows, d), F32)]
    if g_next is not None:
        assert rows == xt.shape[0]
        in_specs += [vec(l + 1), mod(l + 1, 0), mod(l + 1, 1)]
        args += [g_next, modr, modr]
        out_specs.append(row_tile)
        out_shape.append(jax.ShapeDtypeStruct((rows, d), BF16))
    return pl.pallas_call(
        functools.partial(_mlp_kernel, with_next=g_next is not None),
        grid=(rows // tm, d_ff // tf),
        in_specs=in_specs,
        out_specs=out_specs,
        out_shape=out_shape,
        scratch_shapes=[pltpu.VMEM((tm, d), F32)],
        input_output_aliases={0: 0} if g_next is not None else {},
        compiler_params=_params(("parallel", "arbitrary")),
        name="mlp",
    )(*args)


def kernel(x, c, ctx, c_ctx, w_mod, b_mod, g_pre_mix, g_post_mix, g_pre_mlp, g_post_mlp, w_in, conv_w,
           w_conv_out, ssm_lam_re, ssm_lam_im, ssm_log_dt, ssm_b_re, ssm_b_im, ssm_c_re, ssm_c_im, ssm_d,
           w_glu, q_gain, k_gain, w_attn_out, w_out, w_up, w_down):
    batch, seq, d = x.shape
    ctx_len = ctx.shape[1]
    depth = w_in.shape[0]
    in_w = w_in.shape[-1]
    conv_width = conv_w.shape[-1]
    ssm_w = ssm_d.shape[-1]
    attn_w = w_attn_out.shape[1]
    gate_w = 3 * d
    kv_w = (in_w - 3 * conv_width - ssm_w - attn_w - gate_w) // 2
    groups = ssm_lam_re.shape[2]
    assert batch + 1 <= MOD_ROWS and SUBLANES % batch == 0
    assert kv_w == N_KV_HEADS * HEAD_DIM and ssm_lam_re.shape[3] == SSM_STATE
    assert ssm_w == groups * SSM_GROUP and groups % 2 == 0
    assert seq % GRID_W == 0 and (batch * seq) % ctx_len == 0 and ctx_len % LANES == 0
    a_w = 3 * conv_width + ssm_w
    dims = dict(
        batch=batch, seq=seq, ctx=ctx_len, d=d, n_lat=batch * seq, n_ctx=batch * ctx_len,
        conv_w=conv_width, ssm_w=ssm_w, attn_w=attn_w, kv_w=kv_w, gate_w=gate_w, pairs=groups // 2,
        o_q=gate_w, o_k=gate_w + attn_w, o_v=gate_w + attn_w + kv_w,
    )
    n_lat, nt = dims["n_lat"], dims["n_lat"] + dims["n_ctx"]

    xt = jnp.concatenate([x.reshape(n_lat, d), ctx.reshape(dims["n_ctx"], d)], axis=0)
    cc = jnp.concatenate([c, c_ctx[None, :], jnp.zeros((MOD_ROWS - batch - 1, d), F32)], axis=0)
    mod = _modulation(cc, w_mod, b_mod)
    modr = mod.reshape(depth, MOD_ROWS, 1, N_MOD * d)

    row_vec = lambda a: a.reshape(depth, 1, a.shape[-1])
    g_pre_mix, g_post_mix, g_pre_mlp, g_post_mlp = map(row_vec, (g_pre_mix, g_post_mix, g_pre_mlp, g_post_mlp))
    perm = _head_permutation()
    q_gain, k_gain = row_vec(q_gain[:, perm]), row_vec(k_gain[:, perm])
    w_in_a = w_in[:, :, :a_w].astype(BF16)
    w_qk = w_in[:, :, a_w:a_w + attn_w + kv_w].reshape(depth, d, -1, HEAD_DIM)[..., perm]
    w_in_b = jnp.concatenate([w_in[:, :, in_w - gate_w:], w_qk.reshape(depth, d, attn_w + kv_w),
                              w_in[:, :, a_w + attn_w + kv_w:in_w - gate_w]], axis=-1).astype(BF16)
    w_conv_out, w_glu, w_attn_out, w_out, w_up, w_down = (
        w.astype(BF16) for w in (w_conv_out, w_glu, w_attn_out, w_out, w_up, w_down))
    tn_a = _pick([a_w], [1280, 768, 512, 256, 128])
    tn_b = _pick([w_in_b.shape[-1]], [1536, 768, 512, 256, 128])

    qk_tile = _pick([seq, ctx_len], [256, 128, 64, 32])
    cos_t, sin_t = _rope_tables(seq, qk_tile)
    r1, v_mat, a16 = _ssm_parameters(ssm_lam_re, ssm_lam_im, ssm_log_dt, ssm_b_re, ssm_b_im, ssm_c_re, ssm_c_im)
    dsk = jnp.broadcast_to(
        ssm_d.astype(F32).reshape(depth, groups // 2, 2, 1, SSM_GROUP),
        (depth, groups // 2, 2, SSM_CHUNK, SSM_GROUP)).reshape(depth, groups // 2, 1, 2 * SSM_CHUNK * SSM_GROUP)

    h = _prenorm(xt, g_pre_mix, modr, 0, dims)
    for l in range(depth):
        need_ctx = l < depth - 1
        rows = nt if need_ctx else n_lat
        proj_a, u32 = _projection(h, w_in_a, l, dims, tn_a, f32_cols=(3 * conv_width, ssm_w))
        (proj_b,) = _projection(h, w_in_b, l, dims, tn_b)
        u = _ssm_pack(u32, dims)
        ys = _ssm_unpack(_ssm_core(u, r1, v_mat, a16, dsk, l, dims), dims)
        q, k = _qk_prepare(proj_b, cos_t, sin_t, q_gain, k_gain, l, dims)
        attn = _attention(q, k, proj_b, None, dims)
        if need_ctx:
            attn = _attention(q, k, proj_b, attn, dims)
        m = _merge(proj_a, proj_b, ys, attn, conv_w, w_conv_out, w_glu, w_attn_out, l, dims, rows)
        xt, h = _out_projection(m, w_out, xt, modr, g_post_mix, g_pre_mlp, l, dims)
        if need_ctx:
            xt, h = _mlp(xt, h, g_post_mlp, modr, w_up, w_down, l, dims, rows, g_next=g_pre_mix)
        else:
            (xt,) = _mlp(xt, h, g_post_mlp, modr, w_up, w_down, l, dims, rows)
    return xt.reshape(batch, seq, d)
```

```python
import functools
import math

import jax
import jax.numpy as jnp
from jax import lax
from jax.experimental import pallas as pl
from jax.experimental.pallas import tpu as pltpu

F32 = jnp.float32
BF16 = jnp.bfloat16

NORM_EPS = 1e-6
N_MOD = 6
HEAD_DIM = 128
N_KV_HEADS = 2
GRID_W = 64
ROPE_BASE = 10000.0
CONV_K = 3
SSM_GROUP = 16
SSM_STATE = 64
SSM_RE_MAX = -1e-4
SSM_CHUNK = 16
SUBLANES = 8
LANES = 128
HALO_ROWS = 16
MAX_KEY_CHUNK = 4224
MLP_TAIL_PARTS = 2
MOD_ROWS = 8
VMEM_LIMIT = 56 * 2**20


def _params(sem, vmem=VMEM_LIMIT):
    return pltpu.CompilerParams(dimension_semantics=sem, vmem_limit_bytes=vmem)


def _pick(dims, cands):
    for c in cands:
        if all(d % c == 0 for d in dims):
            return c
    raise ValueError(f"no tile in {cands} divides {dims}")


def _rms(x, g):
    ms = jnp.mean(x * x, axis=-1, keepdims=True)
    return x * lax.rsqrt(ms + NORM_EPS) * g


def _mod_kernel(c_ref, w_ref, b_ref, o_ref):
    c = c_ref[...]
    s = (c * jax.nn.sigmoid(c)).astype(BF16)
    o_ref[...] = jnp.dot(s, w_ref[...].astype(BF16), preferred_element_type=F32) + b_ref[...]


def _modulation(cc, w_mod, b_mod):
    depth, d, nout = w_mod.shape
    tn = _pick([nout], [1024, 512, 256, 128])
    return pl.pallas_call(
        _mod_kernel,
        grid=(depth, nout // tn),
        in_specs=[
            pl.BlockSpec((MOD_ROWS, d), lambda l, j: (0, 0)),
            pl.BlockSpec((None, d, tn), lambda l, j: (l, 0, j)),
            pl.BlockSpec((None, 1, tn), lambda l, j: (l, 0, j)),
        ],
        out_specs=pl.BlockSpec((None, MOD_ROWS, tn), lambda l, j: (l, 0, j)),
        out_shape=jax.ShapeDtypeStruct((depth, MOD_ROWS, nout), F32),
        compiler_params=_params(("parallel", "parallel")),
        name="modulation",
    )(cc, w_mod, b_mod.reshape(depth, 1, nout))


def _norm_mod(x, g, shift, scale):
    return (_rms(x, g) * (1.0 + scale) + shift).astype(BF16)


def _prenorm_kernel(x_ref, c_ref, g_ref, sh_ref, sc_ref, xt_ref, h_ref, *, lat_tiles):
    x = jnp.where(pl.program_id(0) < lat_tiles, x_ref[...], c_ref[...])
    xt_ref[...] = x
    h_ref[...] = _norm_mod(x, g_ref[...], sh_ref[...], sc_ref[...])


def _row_group(dims, tm):
    return lambda i: jnp.minimum((i * tm) // dims["seq"], dims["batch"])


def _prenorm(x, ctx, gain, modr, dims):
    d = x.shape[1]
    nt = x.shape[0] + ctx.shape[0]
    tm = _pick([dims["seq"], dims["n_ctx"]], [512, 256, 128])
    lat_tiles = x.shape[0] // tm
    grp = _row_group(dims, tm)
    row_tile = pl.BlockSpec((tm, d), lambda i: (i, 0))
    return pl.pallas_call(
        functools.partial(_prenorm_kernel, lat_tiles=lat_tiles),
        grid=(nt // tm,),
        in_specs=[
            pl.BlockSpec((tm, d), lambda i: (jnp.minimum(i, lat_tiles - 1), 0)),
            pl.BlockSpec((tm, d), lambda i: (jnp.maximum(i - lat_tiles, 0), 0)),
            pl.BlockSpec((None, 1, d), lambda i: (0, 0, 0)),
            pl.BlockSpec((None, None, 1, d), lambda i: (0, grp(i), 0, 0)),
            pl.BlockSpec((None, None, 1, d), lambda i: (0, grp(i), 0, 1)),
        ],
        out_specs=[row_tile, row_tile],
        out_shape=[jax.ShapeDtypeStruct((nt, d), F32), jax.ShapeDtypeStruct((nt, d), BF16)],
        compiler_params=_params(("parallel",)),
        name="prenorm",
    )(x, ctx, gain, modr, modr)


def _proj_kernel(h_ref, w_ref, o_ref, *u_refs, u_tile, u_off):
    r = jnp.dot(h_ref[...], w_ref[...], preferred_element_type=F32)
    o_ref[...] = r.astype(o_ref.dtype)
    if u_refs:
        (ou_ref,) = u_refs

        @pl.when(pl.program_id(1) == u_tile)
        def _():
            for t in range(ou_ref.shape[0]):
                ou_ref[t] = r[:, u_off + LANES * t:u_off + LANES * (t + 1)]


def _projection(h, w, l, dims, tn, f32_cols=None):
    nt, d = h.shape
    n_out = w.shape[-1]
    tm = _pick([dims["seq"], dims["n_ctx"]], [1024, 512, 256, 128])
    assert n_out % tn == 0
    out_specs = [pl.BlockSpec((tm, tn), lambda i, j: (i, j))]
    out_shape = [jax.ShapeDtypeStruct((nt, n_out), BF16)]
    u_tile = u_off = 0
    if f32_cols is not None:
        start, width = f32_cols
        u_tile, u_off = start // tn, start % tn
        assert u_off + width <= tn and u_off % LANES == 0 and width % LANES == 0
        out_specs.append(pl.BlockSpec((width // LANES, tm, LANES), lambda i, j: (0, i, 0)))
        out_shape.append(jax.ShapeDtypeStruct((width // LANES, nt, LANES), F32))
    return pl.pallas_call(
        functools.partial(_proj_kernel, u_tile=u_tile, u_off=u_off),
        grid=(nt // tm, n_out // tn),
        in_specs=[
            pl.BlockSpec((tm, d), lambda i, j: (i, 0)),
            pl.BlockSpec((None, d, tn), lambda i, j: (l, 0, j)),
        ],
        out_specs=out_specs,
        out_shape=out_shape,
        compiler_params=_params(("parallel", "arbitrary")),
        name="in_projection",
    )(h, w)


def _head_permutation():
    quarter = HEAD_DIM // 4
    return jnp.concatenate([jnp.arange(quarter * k, quarter * (k + 1)) for k in (0, 2, 1, 3)])


def _qk_kernel(xq_ref, xk_ref, cos_ref, sin_ref, qg_ref, kg_ref, q_ref, k_ref, *, n_q, n_kv, scale):
    cos = cos_ref[...]
    sin = sin_ref[...]
    mean_cols = jnp.full((HEAD_DIM, HEAD_DIM), 1.0 / HEAD_DIM, BF16)
    for h in range(n_q + n_kv):
        if h < n_q:
            xh = xq_ref[:, HEAD_DIM * h:HEAD_DIM * (h + 1)].astype(F32)
        else:
            xh = xk_ref[:, HEAD_DIM * (h - n_q):HEAD_DIM * (h - n_q + 1)].astype(F32)
        gain = qg_ref[...] * scale if h < n_q else kg_ref[...]
        sq = xh * xh
        hi = sq.astype(BF16)
        lo = (sq - hi.astype(F32)).astype(BF16)
        ms = (jnp.dot(hi, mean_cols, preferred_element_type=F32)
              + jnp.dot(lo, mean_cols, preferred_element_type=F32))
        y = xh * lax.rsqrt(ms + NORM_EPS) * gain
        r = (y * cos + pltpu.roll(y, HEAD_DIM // 2, 1) * sin).astype(BF16)
        if h < n_q:
            q_ref[:, HEAD_DIM * h:HEAD_DIM * (h + 1)] = r
        else:
            k_ref[:, HEAD_DIM * (h - n_q):HEAD_DIM * (h - n_q + 1)] = r


def _qk_tile(dims):
    return _pick([dims["seq"], dims["n_ctx"]], [512, 256, 128, 64, 32])


def _qk_prepare(proj, cos_t, sin_t, q_gain, k_gain, l, dims):
    nt = proj.shape[0]
    attn_w, kv_w = dims["attn_w"], dims["kv_w"]
    tm = _qk_tile(dims)
    assert dims["o_q"] % attn_w == 0 and dims["o_k"] % kv_w == 0
    n_lat = dims["n_lat"] // tm
    n_seq = dims["seq"] // tm
    tab = lambda i: (jnp.where(i < n_lat, i % n_seq, n_seq), 0)
    return pl.pallas_call(
        functools.partial(_qk_kernel, n_q=attn_w // HEAD_DIM, n_kv=kv_w // HEAD_DIM,
                          scale=HEAD_DIM ** -0.5 * math.log2(math.e)),
        grid=(nt // tm,),
        in_specs=[
            pl.BlockSpec((tm, attn_w), lambda i: (i, dims["o_q"] // attn_w)),
            pl.BlockSpec((tm, kv_w), lambda i: (i, dims["o_k"] // kv_w)),
            pl.BlockSpec((tm, HEAD_DIM), tab),
            pl.BlockSpec((tm, HEAD_DIM), tab),
            pl.BlockSpec((None, 1, HEAD_DIM), lambda i: (l, 0, 0)),
            pl.BlockSpec((None, 1, HEAD_DIM), lambda i: (l, 0, 0)),
        ],
        out_specs=[
            pl.BlockSpec((tm, attn_w), lambda i: (i, 0)),
            pl.BlockSpec((tm, kv_w), lambda i: (i, 0)),
        ],
        out_shape=[
            jax.ShapeDtypeStruct((nt, attn_w), BF16),
            jax.ShapeDtypeStruct((nt, kv_w), BF16),
        ],
        compiler_params=_params(("parallel",)),
        name="qk_prepare",
    )(proj, proj, cos_t, sin_t, q_gain, k_gain)


def _rope_tables(seq, pad_rows):
    rows = seq // GRID_W
    row = jnp.repeat(jnp.arange(rows), GRID_W)
    col = jnp.tile(jnp.arange(GRID_W), rows)
    half = HEAD_DIM // 4
    inv_freq = ROPE_BASE ** (-jnp.arange(half, dtype=F32) / half)
    ang_r = row.astype(F32)[:, None] * inv_freq
    ang_c = col.astype(F32)[:, None] * inv_freq
    cr, sr, cc, sc = jnp.cos(ang_r), jnp.sin(ang_r), jnp.cos(ang_c), jnp.sin(ang_c)
    cos_t = jnp.concatenate([cr, cc, cr, cc], axis=-1)
    sin_t = jnp.concatenate([-sr, -sc, sr, sc], axis=-1)
    cos_t = jnp.concatenate([cos_t, jnp.ones((pad_rows, HEAD_DIM), F32)], axis=0)
    sin_t = jnp.concatenate([sin_t, jnp.zeros((pad_rows, HEAD_DIM), F32)], axis=0)
    return cos_t, sin_t


def _attn_kernel(*refs, tq, tk, group, n_src):
    q_ref = refs[0]
    k_refs, v_refs = refs[1:1 + 2 * n_src:2], refs[2:2 + 2 * n_src:2]
    o_ref, ke_scr, ve_scr, s_scr, m_scr, acc_scr = refs[-6:]

    @pl.when(pl.program_id(2) == 0)
    def _():
        r0 = 0
        for k_ref, v_ref in zip(k_refs, v_refs):
            n = k_ref.shape[0]
            ke_scr[r0:r0 + n, :] = k_ref[...]
            ve_scr[r0:r0 + n, :HEAD_DIM] = v_ref[...]
            r0 += n
        ve_scr[:, HEAD_DIM:] = jnp.ones((ve_scr.shape[0], HEAD_DIM), BF16)

    q = jnp.concatenate([q_ref[:, HEAD_DIM * g:HEAD_DIM * (g + 1)] for g in range(group)], axis=0)
    n_chunks = s_scr.shape[0]

    one = jnp.minimum(pl.program_id(2) + 1, 1)

    def first(_, carry):
        m = None
        for c in range(n_chunks):
            s = lax.dot_general(q, ke_scr[tk * c:tk * (c + 1), :], (((1,), (1,)), ((), ())),
                                preferred_element_type=F32)
            s_scr[c] = s
            for t in range(tk // LANES):
                slab = s[:, LANES * t:LANES * (t + 1)]
                m = slab if m is None else jnp.maximum(m, slab)
        m_scr[...] = m
        return carry

    lax.fori_loop(0, one, first, 0)
    m_scr[...] = jnp.broadcast_to(jnp.max(m_scr[...], axis=-1, keepdims=True), m_scr.shape)

    def second(_, carry):
        m_rep = m_scr[...]
        acc = None
        for c in range(n_chunks):
            s = s_scr[c]
            p = jnp.concatenate(
                [jnp.exp2(s[:, LANES * t:LANES * (t + 1)] - m_rep).astype(BF16) for t in range(tk // LANES)],
                axis=1)
            part = jnp.dot(p, ve_scr[tk * c:tk * (c + 1), :], preferred_element_type=F32)
            acc = part if acc is None else acc + part
        acc_scr[...] = acc
        return carry

    lax.fori_loop(0, one, second, 0)
    acc = acc_scr[...]
    o = acc[:, :HEAD_DIM] / acc[:, HEAD_DIM:]
    for g in range(group):
        o_ref[:, HEAD_DIM * g:HEAD_DIM * (g + 1)] = o[tq * g:tq * (g + 1)].astype(o_ref.dtype)


def _attention(q, k, main, prev, dims):
    nt = q.shape[0]
    attn_w, seq, ctx, batch = dims["attn_w"], dims["seq"], dims["ctx"], dims["batch"]
    latent = prev is None
    group = attn_w // HEAD_DIM // N_KV_HEADS
    gw = group * HEAD_DIM
    v_blk = dims["o_v"] // HEAD_DIM
    lq = seq if latent else ctx
    tq = _pick([lq], [128, 64, 32, 16])
    n_keys = ctx + seq if latent else ctx
    tk = max(c for c in range(LANES, min(n_keys, MAX_KEY_CHUNK) + 1, LANES) if n_keys % c == 0)
    row0 = 0 if latent else dims["n_lat"] // tq
    ctx0 = dims["n_lat"] // ctx
    rows = group * tq
    in_specs = [
        pl.BlockSpec((tq, gw), lambda b, h, i: (row0 + b * (lq // tq) + i, h)),
        pl.BlockSpec((ctx, HEAD_DIM), lambda b, h, i: (ctx0 + b, h)),
        pl.BlockSpec((ctx, HEAD_DIM), lambda b, h, i: (ctx0 + b, v_blk + h)),
    ]
    args = [q, k, main]
    aliases = {}
    if latent:
        in_specs += [
            pl.BlockSpec((seq, HEAD_DIM), lambda b, h, i: (b, h)),
            pl.BlockSpec((seq, HEAD_DIM), lambda b, h, i: (b, v_blk + h)),
        ]
        args += [k, main]
    n_src = (len(args) - 1) // 2
    if not latent:
        in_specs.append(pl.BlockSpec(memory_space=pl.ANY))
        args.append(prev)
        aliases = {len(args) - 1: 0}
    scratch = [pltpu.VMEM((n_keys, HEAD_DIM), BF16), pltpu.VMEM((n_keys, 2 * HEAD_DIM), BF16),
               pltpu.VMEM((n_keys // tk, rows, tk), F32), pltpu.VMEM((rows, LANES), F32),
               pltpu.VMEM((rows, 2 * HEAD_DIM), F32)]
    return pl.pallas_call(
        functools.partial(_attn_kernel, tq=tq, tk=tk, group=group, n_src=n_src),
        grid=(batch, N_KV_HEADS, lq // tq),
        in_specs=in_specs,
        out_specs=pl.BlockSpec((tq, gw), lambda b, h, i: (row0 + b * (lq // tq) + i, h)),
        out_shape=jax.ShapeDtypeStruct((nt, attn_w), BF16),
        scratch_shapes=scratch,
        input_output_aliases=aliases,
        compiler_params=_params(("parallel", "parallel", "arbitrary")),
        name="attention_latent" if latent else "attention_context",
    )(*args)


def _ssm_params_kernel(lrr_ref, lir_ref, ldr_ref, lrc_ref, lic_ref, ldc_ref, btr_ref, bti_ref,
                       ctr_ref, cti_ref, r1_ref, v_ref, a_ref):
    t = SSM_CHUNK
    pw = SSM_GROUP * t
    r1_ref[...] = jnp.zeros(r1_ref.shape, r1_ref.dtype)
    v_ref[...] = jnp.zeros(v_ref.shape, v_ref.dtype)
    lane = lax.broadcasted_iota(jnp.int32, (SSM_GROUP, LANES), 1)
    col = lax.broadcasted_iota(jnp.int32, (SSM_GROUP, pw), 1)
    tau_col = (lax.broadcasted_iota(jnp.int32, (LANES, pw), 1) // SSM_GROUP).astype(F32)
    tau_row = lax.broadcasted_iota(jnp.int32, (t, LANES), 0).astype(F32)
    masks = (lane < SSM_STATE, lane >= SSM_STATE)
    toep = []
    for d in range(2):
        lr = jnp.minimum(lrr_ref[d], SSM_RE_MAX)
        li = lir_ref[d]
        dt = jnp.exp(ldr_ref[d])
        mag = jnp.exp(lr * dt)
        ab_re = mag * jnp.cos(li * dt)
        ab_im = mag * jnp.sin(li * dt)
        nr = ab_re - 1.0
        den = lr * lr + li * li
        f_re = (nr * lr + ab_im * li) / den
        f_im = (ab_im * lr - nr * li) / den
        bb_re = f_re * btr_ref[d] - f_im * bti_ref[d]
        bb_im = f_re * bti_ref[d] + f_im * btr_ref[d]
        e16 = jnp.exp(lr * dt * float(t))
        a_ref[2 * d:2 * d + 1, :] = e16 * jnp.cos(li * dt * float(t))
        a_ref[2 * d + 1:2 * d + 2, :] = e16 * jnp.sin(li * dt * float(t))
        tau_s = (t - 1.0) - tau_row if d == 0 else tau_row
        es = jnp.exp(lr * dt * tau_s)
        pw_re = es * jnp.cos(li * dt * tau_s)
        pw_im = es * jnp.sin(li * dt * tau_s)
        for s in range(t):
            ar = pw_re[s:s + 1]
            ai = pw_im[s:s + 1]
            w_re = ar * bb_re - ai * bb_im
            w_im = ar * bb_im + ai * bb_re
            for gi in range(2):
                r0 = gi * pw + SSM_GROUP * s
                c0 = 2 * pw + 2 * LANES * d
                r1_ref[r0:r0 + SSM_GROUP, c0:c0 + LANES] = jnp.where(masks[gi], w_re, 0.0).astype(r1_ref.dtype)
                r1_ref[r0:r0 + SSM_GROUP, c0 + LANES:c0 + 2 * LANES] = (
                    jnp.where(masks[gi], w_im, 0.0).astype(r1_ref.dtype))
        lrc = jnp.minimum(lrc_ref[d], SSM_RE_MAX)
        lic = lic_ref[d]
        dtc = jnp.exp(ldc_ref[d])
        tau_q = tau_col if d == 0 else (t - 1.0) - tau_col
        eq = jnp.exp(lrc * dtc * tau_q)
        p_re = eq * jnp.cos(lic * dtc * tau_q)
        p_im = eq * jnp.sin(lic * dtc * tau_q)
        c_re = ctr_ref[d]
        c_im = cti_ref[d]
        q_re = p_re * c_re - p_im * c_im
        q_im = p_re * c_im + p_im * c_re
        per_group = []
        for gi in range(2):
            k_mat = (jnp.dot(jnp.where(masks[gi], bb_re, 0.0), q_re, precision=lax.Precision.HIGHEST,
                             preferred_element_type=F32)
                     - jnp.dot(jnp.where(masks[gi], bb_im, 0.0), q_im, precision=lax.Precision.HIGHEST,
                               preferred_element_type=F32))
            per_group.append(k_mat)
        toep.append(per_group)
        magc = jnp.exp(lrc * dtc)
        abc_re = magc * jnp.cos(lic * dtc)
        abc_im = magc * jnp.sin(lic * dtc)
        q1_re = abc_re * q_re - abc_im * q_im
        q1_im = abc_re * q_im + abc_im * q_re
        for gi in range(2):
            rows = slice(SSM_STATE * gi, SSM_STATE * (gi + 1))
            r0 = 2 * LANES * d + SSM_STATE * gi
            v_ref[r0:r0 + SSM_STATE, pw * gi:pw * (gi + 1)] = q1_re[rows].astype(v_ref.dtype)
            v_ref[r0 + LANES:r0 + LANES + SSM_STATE, pw * gi:pw * (gi + 1)] = (-q1_im[rows]).astype(v_ref.dtype)
    for gi in range(2):
        k_f, k_b = toep[0][gi], toep[1][gi]
        for s in range(t):
            sh = SSM_GROUP * s
            f = k_f if sh == 0 else jnp.where(col >= sh, pltpu.roll(k_f, sh, 1), 0.0)
            back = SSM_GROUP * (t - 1 - s)
            b = k_b if back == 0 else jnp.where(col < pw - back, pltpu.roll(k_b, pw - back, 1), 0.0)
            r0 = gi * pw + sh
            r1_ref[r0:r0 + SSM_GROUP, pw * gi:pw * (gi + 1)] = (f + b).astype(r1_ref.dtype)


def _pair_rows(a):
    depth, _, groups, n = a.shape
    return a.reshape(depth, 2, groups // 2, 2 * n).transpose(0, 2, 1, 3)


def _ssm_parameters(lam_re, lam_im, log_dt, b_re, b_im, c_re, c_im):
    depth, _, groups, n = lam_re.shape
    pairs = groups // 2
    t = SSM_CHUNK
    pw = SSM_GROUP * t
    ld = jnp.broadcast_to(log_dt[..., None], lam_re.shape)
    rows = [_pair_rows(a.astype(F32))[:, :, :, None, :] for a in (lam_re, lam_im, ld)]
    cols = [_pair_rows(a.astype(F32))[:, :, :, :, None] for a in (lam_re, lam_im, ld)]

    def bt(b):
        return (b.astype(F32).reshape(depth, 2, pairs, 2, n, SSM_GROUP).transpose(0, 2, 1, 5, 3, 4)
                .reshape(depth, pairs, 2, SSM_GROUP, 2 * n))

    def ct(c):
        c = (c.astype(F32).reshape(depth, 2, pairs, 2, SSM_GROUP, n).transpose(0, 2, 1, 3, 5, 4)
             .reshape(depth, pairs, 2, 2 * n, SSM_GROUP))
        return jnp.tile(c, (1, 1, 1, 1, t))

    row_spec = pl.BlockSpec((None, None, 2, 1, 2 * n), lambda l, j: (l, j, 0, 0, 0))
    col_spec = pl.BlockSpec((None, None, 2, 2 * n, 1), lambda l, j: (l, j, 0, 0, 0))
    bt_spec = pl.BlockSpec((None, None, 2, SSM_GROUP, 2 * n), lambda l, j: (l, j, 0, 0, 0))
    ct_spec = pl.BlockSpec((None, None, 2, 2 * n, pw), lambda l, j: (l, j, 0, 0, 0))
    return pl.pallas_call(
        _ssm_params_kernel,
        grid=(depth, pairs),
        in_specs=[row_spec] * 3 + [col_spec] * 3 + [bt_spec] * 2 + [ct_spec] * 2,
        out_specs=[
            pl.BlockSpec((None, None, 2 * pw, 4 * pw), lambda l, j: (l, j, 0, 0)),
            pl.BlockSpec((None, None, 8 * n, 2 * pw), lambda l, j: (l, j, 0, 0)),
            pl.BlockSpec((None, 4, 2 * n), lambda l, j: (l, 0, j)),
        ],
        out_shape=[
            jax.ShapeDtypeStruct((depth, pairs, 2 * pw, 4 * pw), BF16),
            jax.ShapeDtypeStruct((depth, pairs, 8 * n, 2 * pw), BF16),
            jax.ShapeDtypeStruct((depth, 4, groups * n), F32),
        ],
        compiler_params=_params(("parallel", "parallel")),
        name="ssm_parameters",
    )(*rows, *cols, bt(b_re), bt(b_im), ct(c_re), ct(c_im))


def _chunk_rows(u_ref):
    return jnp.concatenate([u_ref[t] for t in range(u_ref.shape[0])], axis=-1)


def _scan_chunk_states(s_ref, a_ref, h_ref, *, batch, lat_tiles, ctx_tiles):
    w = s_ref.shape[-1]
    per_tile = SUBLANES // batch
    row_grp = lax.broadcasted_iota(jnp.int32, (SUBLANES, w), 0) // batch
    zero = jnp.zeros((SUBLANES, w), F32)

    def make_tile(k, order, shift):
        ar = a_ref[k:k + 1, :]
        ai = a_ref[k + 1:k + 2, :]

        def tile(ti, carry):
            cur_re, cur_im = carry
            rows = pl.ds(pl.multiple_of(ti * SUBLANES, SUBLANES), SUBLANES)
            s_re = s_ref[k, rows, :]
            s_im = s_ref[k + 1, rows, :]
            h_re, h_im = zero, zero
            for pos in order:
                sel = row_grp == pos
                h_re = jnp.where(sel, cur_re, h_re)
                h_im = jnp.where(sel, cur_im, h_im)
                nxt_re = ar * cur_re - ai * cur_im + s_re
                nxt_im = ar * cur_im + ai * cur_re + s_im
                if batch == SUBLANES:
                    cur_re, cur_im = nxt_re, nxt_im
                else:
                    cur_re = pltpu.roll(nxt_re, shift, 0)
                    cur_im = pltpu.roll(nxt_im, shift, 0)
            h_ref[k, rows, :] = h_re
            h_ref[k + 1, rows, :] = h_im
            return cur_re, cur_im

        return tile

    fwd = make_tile(0, range(per_tile), batch)
    carry = lax.fori_loop(0, ctx_tiles, lambda i, c: fwd(lat_tiles + i, c), (zero, zero))
    lax.fori_loop(0, lat_tiles, fwd, carry)
    bwd = make_tile(2, range(per_tile - 1, -1, -1), SUBLANES - batch)
    carry = lax.fori_loop(0, ctx_tiles, lambda i, c: bwd(lat_tiles + ctx_tiles - 1 - i, c), (zero, zero))
    lax.fori_loop(0, lat_tiles, lambda i, c: bwd(lat_tiles - 1 - i, c), carry)


def _ssm_core_kernel(u_ref, r1_ref, v_ref, a_ref, d_ref, y_ref, s_scr, h_scr, *, batch, lat_tiles, ctx_tiles):
    n_t = u_ref.shape[0]
    kw = n_t * LANES
    u = _chunk_rows(u_ref)
    r = jnp.dot(u, r1_ref[...], preferred_element_type=F32)
    for t in range(n_t):
        y_ref[t] = r[:, LANES * t:LANES * (t + 1)]
    for k in range(4):
        s_scr[k] = r[:, kw + LANES * k:kw + LANES * (k + 1)]
    _scan_chunk_states(s_scr, a_ref, h_scr, batch=batch, lat_tiles=lat_tiles, ctx_tiles=ctx_tiles)
    h = jnp.concatenate([h_scr[k] for k in range(4)], axis=-1).astype(BF16)
    y = jnp.dot(h, v_ref[...], preferred_element_type=F32) + d_ref[...] * u.astype(F32)
    for t in range(n_t):
        y_ref[t] += y[:, LANES * t:LANES * (t + 1)]


def _ssm_core(u, r1, v, a16, dsk, l, dims):
    pairs, n_t, nc, _ = u.shape
    kw = n_t * LANES
    batch = dims["batch"]
    lat_tiles = dims["seq"] // SSM_CHUNK * batch // SUBLANES
    ctx_tiles = dims["ctx"] // SSM_CHUNK * batch // SUBLANES
    return pl.pallas_call(
        functools.partial(_ssm_core_kernel, batch=batch, lat_tiles=lat_tiles, ctx_tiles=ctx_tiles),
        grid=(pairs,),
        in_specs=[
            pl.BlockSpec((None, n_t, nc, LANES), lambda j: (j, 0, 0, 0)),
            pl.BlockSpec((None, None, kw, 2 * kw), lambda j: (l, j, 0, 0)),
            pl.BlockSpec((None, None, 4 * LANES, kw), lambda j: (l, j, 0, 0)),
            pl.BlockSpec((None, 4, LANES), lambda j: (l, 0, j)),
            pl.BlockSpec((None, None, 1, kw), lambda j: (l, j, 0, 0)),
        ],
        out_specs=pl.BlockSpec((None, n_t, nc, LANES), lambda j: (j, 0, 0, 0)),
        out_shape=jax.ShapeDtypeStruct((pairs, n_t, nc, LANES), F32),
        scratch_shapes=[pltpu.VMEM((4, nc, LANES), F32), pltpu.VMEM((4, nc, LANES), F32)],
        compiler_params=_params(("parallel",)),
        name="ssm_core",
    )(u, r1, v, a16, dsk)


GROUPS_PER_TILE = LANES // SSM_GROUP


def _lane_slot(rows):
    return lax.broadcasted_iota(jnp.int32, (rows, LANES), 1) // SSM_GROUP


def _block_transposes(sets, slot):
    sets = [list(t) for t in sets]
    d = GROUPS_PER_TILE // 2
    while d:
        low = (slot & d) == 0
        for t in sets:
            for a in range(GROUPS_PER_TILE):
                if not a & d:
                    lo, hi = t[a], t[a + d]
                    t[a] = jnp.where(low, lo, pltpu.roll(hi, d * SSM_GROUP, 1))
                    t[a + d] = jnp.where(low, pltpu.roll(lo, LANES - d * SSM_GROUP, 1), hi)
        d //= 2
    return sets


def _chunk_tiling(dims):
    ncl, ncc = dims["seq"] // SSM_CHUNK, dims["ctx"] // SSM_CHUNK
    ct = _pick([ncl, ncc], [16, 8, 4, 2, 1])
    nlt, nct = ncl // ct, ncc // ct

    def token_block(k, b):
        return jnp.where(k < nlt, b * nlt + k, dims["batch"] * nlt + b * nct + (k - nlt))

    return ct, nlt + nct, token_block


def _ssm_pack_kernel(*refs, batch, ct):
    x_refs, o_ref, scr = refs[:batch], refs[batch], refs[batch + 1]
    slot = _lane_slot(ct)
    halves = SSM_CHUNK // GROUPS_PER_TILE
    for b in range(batch):
        where = [(j, half) for j in range(x_refs[b].shape[0]) for half in range(halves)]
        sets = [[x_refs[b][j, pl.ds(GROUPS_PER_TILE * half + k, ct, stride=SSM_CHUNK), :]
                 for k in range(GROUPS_PER_TILE)] for j, half in where]
        for (j, half), outs in zip(where, _block_transposes(sets, slot)):
            for gg, out in enumerate(outs):
                g = GROUPS_PER_TILE * j + gg
                scr[g // 2, (g % 2) * halves + half, pl.ds(b, ct, stride=batch), :] = out
    o_ref[...] = scr[...].astype(o_ref.dtype)


def _ssm_pack(u32, dims):
    batch, pairs = dims["batch"], dims["pairs"]
    ct, n_tiles, token_block = _chunk_tiling(dims)
    n_in, nt, _ = u32.shape
    n_t = 2 * SSM_CHUNK * SSM_GROUP // LANES
    return pl.pallas_call(
        functools.partial(_ssm_pack_kernel, batch=batch, ct=ct),
        grid=(n_tiles,),
        in_specs=[pl.BlockSpec((n_in, ct * SSM_CHUNK, LANES),
                               functools.partial(lambda k, b: (0, token_block(k, b), 0), b=b))
                  for b in range(batch)],
        out_specs=pl.BlockSpec((pairs, n_t, ct * batch, LANES), lambda k: (0, 0, k, 0)),
        out_shape=jax.ShapeDtypeStruct((pairs, n_t, nt // SSM_CHUNK, LANES), BF16),
        scratch_shapes=[pltpu.VMEM((pairs, n_t, ct * batch, LANES), F32)],
        compiler_params=_params(("parallel",)),
        name="ssm_pack",
    )(*([u32] * batch))


def _ssm_unpack_kernel(y_ref, o_ref, *, batch, ct):
    b = pl.program_id(1)
    slot = _lane_slot(ct)
    halves = SSM_CHUNK // GROUPS_PER_TILE
    where = [(j, half) for j in range(o_ref.shape[0]) for half in range(halves)]
    sets = []
    for j, half in where:
        groups = [GROUPS_PER_TILE * j + gg for gg in range(GROUPS_PER_TILE)]
        sets.append([y_ref[g // 2, (g % 2) * halves + half, pl.ds(b, ct, stride=batch), :] for g in groups])
    for (j, half), outs in zip(where, _block_transposes(sets, slot)):
        for k, out in enumerate(outs):
            o_ref[j, pl.ds(GROUPS_PER_TILE * half + k, ct, stride=SSM_CHUNK), :] = out


def _ssm_unpack(y, dims):
    batch = dims["batch"]
    pairs, n_t, nc, _ = y.shape
    ct, n_tiles, token_block = _chunk_tiling(dims)
    n_out = dims["ssm_w"] // LANES
    return pl.pallas_call(
        functools.partial(_ssm_unpack_kernel, batch=batch, ct=ct),
        grid=(n_tiles, batch),
        in_specs=[pl.BlockSpec((pairs, n_t, ct * batch, LANES), lambda k, b: (0, 0, k, 0))],
        out_specs=pl.BlockSpec((n_out, ct * SSM_CHUNK, LANES), lambda k, b: (0, token_block(k, b), 0)),
        out_shape=jax.ShapeDtypeStruct((n_out, nc * SSM_CHUNK, LANES), F32),
        compiler_params=_params(("parallel", "arbitrary")),
        name="ssm_unpack",
    )(y)


def _merge_kernel(pb_ref, pc_ref, pv_ref, hpc_ref, hpv_ref, hnc_ref, hnv_ref, cw_ref, ys_ref, at_ref,
                  g_ref, wc_ref, wg_ref, wa_ref, o_ref, *, tm, tn, seq, ctx, n_lat):
    d = o_ref.shape[1]
    r0 = pl.program_id(0) * tm
    period = jnp.where(r0 >= n_lat, ctx, seq)
    starts = r0 % period == 0
    ends = (r0 + tm) % period == 0
    w = pc_ref[...].astype(F32) * pv_ref[...].astype(F32)
    prev = (hpc_ref[...].astype(F32) * hpv_ref[...].astype(F32))[HALO_ROWS - 1:HALO_ROWS]
    nxt = (hnc_ref[...].astype(F32) * hnv_ref[...].astype(F32))[0:1]
    prev = jnp.where(starts, 0.0, prev)
    nxt = jnp.where(ends, 0.0, nxt)
    row = lax.broadcasted_iota(jnp.int32, w.shape, 0)
    up = jnp.where(row == 0, prev, pltpu.roll(w, 1, 0))
    dn = jnp.where(row == tm - 1, nxt, pltpu.roll(w, tm - 1, 0))
    cw = cw_ref[...]
    cf = (pb_ref[...].astype(F32) * (cw[0:1] * up + cw[1:2] * w + cw[2:3] * dn)).astype(BF16)
    yg = jax.nn.gelu(_chunk_rows(ys_ref)).astype(BF16)
    at = at_ref[...]
    for c in range(d // tn):
        cols = slice(tn * c, tn * (c + 1))
        gate = lambda k: jax.nn.sigmoid(g_ref[:, k * d + tn * c:k * d + tn * (c + 1)].astype(F32))
        ya = jnp.dot(cf, wc_ref[:, cols], preferred_element_type=F32)
        yc = jnp.dot(at, wa_ref[:, cols], preferred_element_type=F32)
        ys = jnp.dot(yg, wg_ref[:, cols], preferred_element_type=F32) * jax.nn.sigmoid(
            jnp.dot(yg, wg_ref[:, d + tn * c:d + tn * (c + 1)], preferred_element_type=F32))
        o_ref[:, cols] = (gate(0) * ya + gate(1) * ys + gate(2) * yc).astype(o_ref.dtype)


def _merge(main, gates, ys, attn, conv_w, w_conv_out, w_glu, w_attn_out, l, dims, rows):
    d = dims["d"]
    cw_, sw, aw = dims["conv_w"], dims["ssm_w"], dims["attn_w"]
    tm = _pick([dims["seq"], dims["ctx"]], [512, 256, 128, 64, 32])
    tn = _pick([d], [512, 256, 128])
    n_halo = main.shape[0] // HALO_ROWS
    per = tm // HALO_ROWS
    prev_blk = lambda i: jnp.maximum(i * per - 1, 0)
    next_blk = lambda i: jnp.minimum((i + 1) * per, n_halo - 1)
    resident = lambda shape: pl.BlockSpec((None,) + shape, lambda i: (l, 0, 0), pipeline_mode=pl.Buffered(1))
    return pl.pallas_call(
        functools.partial(_merge_kernel, tm=tm, tn=tn, seq=dims["seq"], ctx=dims["ctx"], n_lat=dims["n_lat"]),
        grid=(rows // tm,),
        in_specs=[
            pl.BlockSpec((tm, cw_), lambda i: (i, 0)),
            pl.BlockSpec((tm, cw_), lambda i: (i, 1)),
            pl.BlockSpec((tm, cw_), lambda i: (i, 2)),
            pl.BlockSpec((HALO_ROWS, cw_), lambda i: (prev_blk(i), 1)),
            pl.BlockSpec((HALO_ROWS, cw_), lambda i: (prev_blk(i), 2)),
            pl.BlockSpec((HALO_ROWS, cw_), lambda i: (next_blk(i), 1)),
            pl.BlockSpec((HALO_ROWS, cw_), lambda i: (next_blk(i), 2)),
            resident((CONV_K, cw_)),
            pl.BlockSpec((sw // LANES, tm, LANES), lambda i: (0, i, 0)),
            pl.BlockSpec((tm, aw), lambda i: (i, 0)),
            pl.BlockSpec((tm, 3 * d), lambda i: (i, 0)),
            resident((cw_, d)),
            resident((sw, 2 * d)),
            resident((aw, d)),
        ],
        out_specs=pl.BlockSpec((tm, d), lambda i: (i, 0)),
        out_shape=jax.ShapeDtypeStruct((rows, d), BF16),
        compiler_params=_params(("parallel",)),
        name="branch_merge",
    )(main, main, main, main, main, main, main, conv_w, ys, attn, gates, w_conv_out, w_glu, w_attn_out)


def _outproj_kernel(m_ref, w_ref, x_ref, gate_ref, g_ref, gn_ref, shn_ref, scn_ref, o_ref, h_ref):
    half = m_ref.shape[0] // 2
    for rows in (slice(0, half), slice(half, 2 * half)):
        r = jnp.dot(m_ref[rows, :], w_ref[...], preferred_element_type=F32)
        x = x_ref[rows, :] + gate_ref[...] * _rms(r, g_ref[...])
        o_ref[rows, :] = x
        h_ref[rows, :] = _norm_mod(x, gn_ref[...], shn_ref[...], scn_ref[...])


def _out_projection(m, w_out, xt, modr, g_post, g_pre_mlp, l, dims):
    rows, d = m.shape
    tm = _pick([dims["seq"], dims["n_ctx"]], [512, 256, 128])
    grp = _row_group(dims, tm)
    mod = lambda k: pl.BlockSpec((None, None, 1, d), lambda i: (l, grp(i), 0, k))
    vec = pl.BlockSpec((None, 1, d), lambda i: (l, 0, 0))
    return pl.pallas_call(
        _outproj_kernel,
        grid=(rows // tm,),
        in_specs=[
            pl.BlockSpec((tm, d), lambda i: (i, 0)),
            pl.BlockSpec((None, d, d), lambda i: (l, 0, 0)),
            pl.BlockSpec((tm, d), lambda i: (i, 0)),
            mod(2), vec, vec, mod(3), mod(4),
        ],
        out_specs=[pl.BlockSpec((tm, d), lambda i: (i, 0)), pl.BlockSpec((tm, d), lambda i: (i, 0))],
        out_shape=[jax.ShapeDtypeStruct(xt.shape, F32), jax.ShapeDtypeStruct((rows, d), BF16)],
        input_output_aliases={2: 0},
        compiler_params=_params(("parallel",)),
        name="out_projection",
    )(m, w_out, xt, modr, g_post, g_pre_mlp, modr, modr)


def _mlp_kernel(x_ref, h_ref, wu_ref, wd_ref, gate_ref, gpost_ref, *rest, with_next):
    j = pl.program_id(1)
    if with_next:
        gn_ref, shn_ref, scn_ref, o_ref, hn_ref, acc_scr = rest
    else:
        o_ref, acc_scr = rest

    last = pl.num_programs(1) - 1

    def partial_out(rows):
        u = jnp.maximum(jnp.dot(h_ref[rows, :], wu_ref[...], preferred_element_type=F32), 0.0)
        return jnp.dot((u * u).astype(BF16), wd_ref[...], preferred_element_type=F32)

    every = slice(None)

    @pl.when(j == 0)
    def _():
        acc_scr[...] = partial_out(every)

    @pl.when(jnp.logical_and(j > 0, j < last))
    def _():
        acc_scr[...] += partial_out(every)

    @pl.when(j == last)
    def _():
        part = acc_scr.shape[0] // MLP_TAIL_PARTS
        for k in range(MLP_TAIL_PARTS):
            rows = slice(part * k, part * (k + 1))
            r = acc_scr[rows, :] + partial_out(rows)
            x = x_ref[rows, :] + gate_ref[...] * _rms(r, gpost_ref[...])
            o_ref[rows, :] = x
            if with_next:
                hn_ref[rows, :] = _norm_mod(x, gn_ref[...], shn_ref[...], scn_ref[...])


def _mlp(xt, h, g_post, modr, w_up, w_down, l, dims, rows, g_next=None):
    d = xt.shape[1]
    d_ff = w_up.shape[-1]
    tm = _pick([dims["seq"], dims["n_ctx"]], [512, 256, 128])
    tf = _pick([d_ff], [1024, 512, 256, 128])
    grp = _row_group(dims, tm)
    mod = lambda layer, k: pl.BlockSpec((None, None, 1, d), lambda i, j: (layer, grp(i), 0, k))
    vec = lambda layer: pl.BlockSpec((None, 1, d), lambda i, j: (layer, 0, 0))
    row_tile = pl.BlockSpec((tm, d), lambda i, j: (i, 0))
    in_specs = [
        row_tile, row_tile,
        pl.BlockSpec((None, d, tf), lambda i, j: (l, 0, j)),
        pl.BlockSpec((None, tf, d), lambda i, j: (l, j, 0)),
        mod(l, 5), vec(l),
    ]
    args = [xt, h, w_up, w_down, modr, g_post]
    out_specs, out_shape = [row_tile], [jax.ShapeDtypeStruct((rows, d), F32)]
    if g_next is not None:
        assert rows == xt.shape[0]
        in_specs += [vec(l + 1), mod(l + 1, 0), mod(l + 1, 1)]
        args += [g_next, modr, modr]
        out_specs.append(row_tile)
        out_shape.append(jax.ShapeDtypeStruct((rows, d), BF16))
    return pl.pallas_call(
        functools.partial(_mlp_kernel, with_next=g_next is not None),
        grid=(rows // tm, d_ff // tf),
        in_specs=in_specs,
        out_specs=out_specs,
        out_shape=out_shape,
        scratch_shapes=[pltpu.VMEM((tm, d), F32)],
        input_output_aliases={0: 0} if g_next is not None else {},
        compiler_params=_params(("parallel", "arbitrary")),
        name="mlp",
    )(*args)


def kernel(x, c, ctx, c_ctx, w_mod, b_mod, g_pre_mix, g_post_mix, g_pre_mlp, g_post_mlp, w_in, conv_w,
           w_conv_out, ssm_lam_re, ssm_lam_im, ssm_log_dt, ssm_b_re, ssm_b_im, ssm_c_re, ssm_c_im, ssm_d,
           w_glu, q_gain, k_gain, w_attn_out, w_out, w_up, w_down):
    batch, seq, d = x.shape
    ctx_len = ctx.shape[1]
    depth = w_in.shape[0]
    in_w = w_in.shape[-1]
    conv_width = conv_w.shape[-1]
    ssm_w = ssm_d.shape[-1]
    attn_w = w_attn_out.shape[1]
    gate_w = 3 * d
    kv_w = (in_w - 3 * conv_width - ssm_w - attn_w - gate_w) // 2
    groups = ssm_lam_re.shape[2]
    assert batch + 1 <= MOD_ROWS and SUBLANES % batch == 0
    assert kv_w == N_KV_HEADS * HEAD_DIM and ssm_lam_re.shape[3] == SSM_STATE
    assert ssm_w == groups * SSM_GROUP and groups % 2 == 0
    assert seq % GRID_W == 0 and (batch * seq) % ctx_len == 0 and ctx_len % LANES == 0
    a_w = 3 * conv_width + ssm_w
    dims = dict(
        batch=batch, seq=seq, ctx=ctx_len, d=d, n_lat=batch * seq, n_ctx=batch * ctx_len,
        conv_w=conv_width, ssm_w=ssm_w, attn_w=attn_w, kv_w=kv_w, gate_w=gate_w, pairs=groups // 2,
        o_q=gate_w, o_k=gate_w + attn_w, o_v=gate_w + attn_w + kv_w,
    )
    n_lat, nt = dims["n_lat"], dims["n_lat"] + dims["n_ctx"]

    cc = jnp.concatenate([c, c_ctx[None, :], jnp.zeros((MOD_ROWS - batch - 1, d), F32)], axis=0)
    mod = _modulation(cc, w_mod, b_mod)
    modr = mod.reshape(depth, MOD_ROWS, 1, N_MOD * d)

    row_vec = lambda a: a.reshape(depth, 1, a.shape[-1])
    g_pre_mix, g_post_mix, g_pre_mlp, g_post_mlp = map(row_vec, (g_pre_mix, g_post_mix, g_pre_mlp, g_post_mlp))
    perm = _head_permutation()
    q_gain, k_gain = row_vec(q_gain[:, perm]), row_vec(k_gain[:, perm])
    w_in_a = w_in[:, :, :a_w].astype(BF16)
    w_qk = w_in[:, :, a_w:a_w + attn_w + kv_w].reshape(depth, d, -1, HEAD_DIM)[..., perm]
    w_in_b = jnp.concatenate([w_in[:, :, in_w - gate_w:], w_qk.reshape(depth, d, attn_w + kv_w),
                              w_in[:, :, a_w + attn_w + kv_w:in_w - gate_w]], axis=-1).astype(BF16)
    w_conv_out, w_glu, w_attn_out, w_out, w_up, w_down = (
        w.astype(BF16) for w in (w_conv_out, w_glu, w_attn_out, w_out, w_up, w_down))
    tn_a = _pick([a_w], [1280, 768, 512, 256, 128])
    tn_b = _pick([w_in_b.shape[-1]], [1536, 768, 512, 256, 128])

    cos_t, sin_t = _rope_tables(seq, _qk_tile(dims))
    r1, v_mat, a16 = _ssm_parameters(ssm_lam_re, ssm_lam_im, ssm_log_dt, ssm_b_re, ssm_b_im, ssm_c_re, ssm_c_im)
    dsk = jnp.broadcast_to(
        ssm_d.astype(F32).reshape(depth, groups // 2, 2, 1, SSM_GROUP),
        (depth, groups // 2, 2, SSM_CHUNK, SSM_GROUP)).reshape(depth, groups // 2, 1, 2 * SSM_CHUNK * SSM_GROUP)

    xt, h = _prenorm(x.reshape(n_lat, d), ctx.reshape(dims["n_ctx"], d), g_pre_mix, modr, dims)
    for l in range(depth):
        need_ctx = l < depth - 1
        rows = nt if need_ctx else n_lat
        proj_a, u32 = _projection(h, w_in_a, l, dims, tn_a, f32_cols=(3 * conv_width, ssm_w))
        (proj_b,) = _projection(h, w_in_b, l, dims, tn_b)
        u = _ssm_pack(u32, dims)
        ys = _ssm_unpack(_ssm_core(u, r1, v_mat, a16, dsk, l, dims), dims)
        q, k = _qk_prepare(proj_b, cos_t, sin_t, q_gain, k_gain, l, dims)
        attn = _attention(q, k, proj_b, None, dims)
        if need_ctx:
            attn = _attention(q, k, proj_b, attn, dims)
        m = _merge(proj_a, proj_b, ys, attn, conv_w, w_conv_out, w_glu, w_attn_out, l, dims, rows)
        xt, h = _out_projection(m, w_out, xt, modr, g_post_mix, g_pre_mlp, l, dims)
        if need_ctx:
            xt, h = _mlp(xt, h, g_post_mlp, modr, w_up, w_down, l, dims, rows, g_next=g_pre_mix)
        else:
            (xt,) = _mlp(xt, h, g_post_mlp, modr, w_up, w_down, l, dims, rows)
    return xt.reshape(batch, seq, d)
```

```python
import functools
import math

import jax
import jax.numpy as jnp
from jax import lax
from jax.experimental import pallas as pl
from jax.experimental.pallas import tpu as pltpu

F32 = jnp.float32
BF16 = jnp.bfloat16

NORM_EPS = 1e-6
N_MOD = 6
HEAD_DIM = 128
N_KV_HEADS = 2
GRID_W = 64
ROPE_BASE = 10000.0
CONV_K = 3
SSM_GROUP = 16
SSM_STATE = 64
SSM_RE_MAX = -1e-4
SSM_CHUNK = 16
SUBLANES = 8
LANES = 128
HALO_ROWS = 16
MXU_TILE = 256
MAX_KEY_CHUNK = 4352
MLP_TAIL_PARTS = 2
MOD_ROWS = 8
VMEM_LIMIT = 56 * 2**20


def _params(sem, vmem=VMEM_LIMIT):
    return pltpu.CompilerParams(dimension_semantics=sem, vmem_limit_bytes=vmem)


def _pick(dims, cands):
    for c in cands:
        if all(d % c == 0 for d in dims):
            return c
    raise ValueError(f"no tile in {cands} divides {dims}")


def _rms(x, g):
    ms = jnp.mean(x * x, axis=-1, keepdims=True)
    return x * lax.rsqrt(ms + NORM_EPS) * g


def _mod_kernel(c_ref, w_ref, b_ref, o_ref):
    c = c_ref[...]
    s = (c * jax.nn.sigmoid(c)).astype(BF16)
    o_ref[...] = jnp.dot(s, w_ref[...].astype(BF16), preferred_element_type=F32) + b_ref[...]


def _modulation(cc, w_mod, b_mod):
    depth, d, nout = w_mod.shape
    tn = _pick([nout], [1024, 512, 256, 128])
    return pl.pallas_call(
        _mod_kernel,
        grid=(depth, nout // tn),
        in_specs=[
            pl.BlockSpec((MOD_ROWS, d), lambda l, j: (0, 0)),
            pl.BlockSpec((None, d, tn), lambda l, j: (l, 0, j)),
            pl.BlockSpec((None, 1, tn), lambda l, j: (l, 0, j)),
        ],
        out_specs=pl.BlockSpec((None, MOD_ROWS, tn), lambda l, j: (l, 0, j)),
        out_shape=jax.ShapeDtypeStruct((depth, MOD_ROWS, nout), F32),
        compiler_params=_params(("parallel", "parallel")),
        name="modulation",
    )(cc, w_mod, b_mod.reshape(depth, 1, nout))


def _norm_mod(x, g, shift, scale):
    return (_rms(x, g) * (1.0 + scale) + shift).astype(BF16)


def _prenorm_kernel(x_ref, c_ref, g_ref, sh_ref, sc_ref, xt_ref, h_ref, *, lat_tiles):
    x = jnp.where(pl.program_id(0) < lat_tiles, x_ref[...], c_ref[...])
    xt_ref[...] = x
    h_ref[...] = _norm_mod(x, g_ref[...], sh_ref[...], sc_ref[...])


def _row_group(dims, tm):
    return lambda i: jnp.minimum((i * tm) // dims["seq"], dims["batch"])


def _prenorm(x, ctx, gain, modr, dims):
    d = x.shape[1]
    nt = x.shape[0] + ctx.shape[0]
    tm = _pick([dims["seq"], dims["n_ctx"]], [512, 256, 128])
    lat_tiles = x.shape[0] // tm
    grp = _row_group(dims, tm)
    row_tile = pl.BlockSpec((tm, d), lambda i: (i, 0))
    return pl.pallas_call(
        functools.partial(_prenorm_kernel, lat_tiles=lat_tiles),
        grid=(nt // tm,),
        in_specs=[
            pl.BlockSpec((tm, d), lambda i: (jnp.minimum(i, lat_tiles - 1), 0)),
            pl.BlockSpec((tm, d), lambda i: (jnp.maximum(i - lat_tiles, 0), 0)),
            pl.BlockSpec((None, 1, d), lambda i: (0, 0, 0)),
            pl.BlockSpec((None, None, 1, d), lambda i: (0, grp(i), 0, 0)),
            pl.BlockSpec((None, None, 1, d), lambda i: (0, grp(i), 0, 1)),
        ],
        out_specs=[row_tile, row_tile],
        out_shape=[jax.ShapeDtypeStruct((nt, d), F32), jax.ShapeDtypeStruct((nt, d), BF16)],
        compiler_params=_params(("parallel",)),
        name="prenorm",
    )(x, ctx, gain, modr, modr)


def _proj_kernel(h_ref, w_ref, o_ref, *u_refs, u_tile, u_off):
    r = jnp.dot(h_ref[...], w_ref[...], preferred_element_type=F32)
    o_ref[...] = r.astype(o_ref.dtype)
    if u_refs:
        (ou_ref,) = u_refs

        @pl.when(pl.program_id(1) == u_tile)
        def _():
            for t in range(ou_ref.shape[0]):
                ou_ref[t] = r[:, u_off + LANES * t:u_off + LANES * (t + 1)]


def _projection(h, w, l, dims, tn, f32_cols=None):
    nt, d = h.shape
    n_out = w.shape[-1]
    tm = _pick([dims["seq"], dims["n_ctx"]], [1024, 512, 256, 128])
    assert n_out % tn == 0
    out_specs = [pl.BlockSpec((tm, tn), lambda i, j: (i, j))]
    out_shape = [jax.ShapeDtypeStruct((nt, n_out), BF16)]
    u_tile = u_off = 0
    if f32_cols is not None:
        start, width = f32_cols
        u_tile, u_off = start // tn, start % tn
        assert u_off + width <= tn and u_off % LANES == 0 and width % LANES == 0
        out_specs.append(pl.BlockSpec((width // LANES, tm, LANES), lambda i, j: (0, i, 0)))
        out_shape.append(jax.ShapeDtypeStruct((width // LANES, nt, LANES), F32))
    return pl.pallas_call(
        functools.partial(_proj_kernel, u_tile=u_tile, u_off=u_off),
        grid=(nt // tm, n_out // tn),
        in_specs=[
            pl.BlockSpec((tm, d), lambda i, j: (i, 0)),
            pl.BlockSpec((None, d, tn), lambda i, j: (l, 0, j)),
        ],
        out_specs=out_specs,
        out_shape=out_shape,
        compiler_params=_params(("parallel", "arbitrary")),
        name="in_projection",
    )(h, w)


def _head_permutation():
    quarter = HEAD_DIM // 4
    return jnp.concatenate([jnp.arange(quarter * k, quarter * (k + 1)) for k in (0, 2, 1, 3)])


def _qk_kernel(xq_ref, xk_ref, cos_ref, sin_ref, qg_ref, kg_ref, q_ref, k_ref, *, n_q, n_kv, scale):
    cos = cos_ref[...]
    sin = sin_ref[...]
    mean_cols = jnp.full((HEAD_DIM, HEAD_DIM), 1.0 / HEAD_DIM, BF16)
    for h in range(n_q + n_kv):
        if h < n_q:
            xh = xq_ref[:, HEAD_DIM * h:HEAD_DIM * (h + 1)].astype(F32)
        else:
            xh = xk_ref[:, HEAD_DIM * (h - n_q):HEAD_DIM * (h - n_q + 1)].astype(F32)
        gain = qg_ref[...] * scale if h < n_q else kg_ref[...]
        sq = xh * xh
        hi = sq.astype(BF16)
        lo = (sq - hi.astype(F32)).astype(BF16)
        ms = (jnp.dot(hi, mean_cols, preferred_element_type=F32)
              + jnp.dot(lo, mean_cols, preferred_element_type=F32))
        y = xh * lax.rsqrt(ms + NORM_EPS) * gain
        r = (y * cos + pltpu.roll(y, HEAD_DIM // 2, 1) * sin).astype(BF16)
        if h < n_q:
            q_ref[:, HEAD_DIM * h:HEAD_DIM * (h + 1)] = r
        else:
            k_ref[:, HEAD_DIM * (h - n_q):HEAD_DIM * (h - n_q + 1)] = r


def _qk_tile(dims):
    return _pick([dims["seq"], dims["n_ctx"]], [512, 256, 128, 64, 32])


def _qk_prepare(proj, cos_t, sin_t, q_gain, k_gain, l, dims):
    nt = proj.shape[0]
    attn_w, kv_w = dims["attn_w"], dims["kv_w"]
    tm = _qk_tile(dims)
    assert dims["o_q"] % attn_w == 0 and dims["o_k"] % kv_w == 0
    n_lat = dims["n_lat"] // tm
    n_seq = dims["seq"] // tm
    tab = lambda i: (jnp.where(i < n_lat, i % n_seq, n_seq), 0)
    return pl.pallas_call(
        functools.partial(_qk_kernel, n_q=attn_w // HEAD_DIM, n_kv=kv_w // HEAD_DIM,
                          scale=HEAD_DIM ** -0.5 * math.log2(math.e)),
        grid=(nt // tm,),
        in_specs=[
            pl.BlockSpec((tm, attn_w), lambda i: (i, dims["o_q"] // attn_w)),
            pl.BlockSpec((tm, kv_w), lambda i: (i, dims["o_k"] // kv_w)),
            pl.BlockSpec((tm, HEAD_DIM), tab),
            pl.BlockSpec((tm, HEAD_DIM), tab),
            pl.BlockSpec((None, 1, HEAD_DIM), lambda i: (l, 0, 0)),
            pl.BlockSpec((None, 1, HEAD_DIM), lambda i: (l, 0, 0)),
        ],
        out_specs=[
            pl.BlockSpec((tm, attn_w), lambda i: (i, 0)),
            pl.BlockSpec((tm, kv_w), lambda i: (i, 0)),
        ],
        out_shape=[
            jax.ShapeDtypeStruct((nt, attn_w), BF16),
            jax.ShapeDtypeStruct((nt, kv_w), BF16),
        ],
        compiler_params=_params(("parallel",)),
        name="qk_prepare",
    )(proj, proj, cos_t, sin_t, q_gain, k_gain)


def _rope_tables(seq, pad_rows):
    rows = seq // GRID_W
    row = jnp.repeat(jnp.arange(rows), GRID_W)
    col = jnp.tile(jnp.arange(GRID_W), rows)
    half = HEAD_DIM // 4
    inv_freq = ROPE_BASE ** (-jnp.arange(half, dtype=F32) / half)
    ang_r = row.astype(F32)[:, None] * inv_freq
    ang_c = col.astype(F32)[:, None] * inv_freq
    cr, sr, cc, sc = jnp.cos(ang_r), jnp.sin(ang_r), jnp.cos(ang_c), jnp.sin(ang_c)
    cos_t = jnp.concatenate([cr, cc, cr, cc], axis=-1)
    sin_t = jnp.concatenate([-sr, -sc, sr, sc], axis=-1)
    cos_t = jnp.concatenate([cos_t, jnp.ones((pad_rows, HEAD_DIM), F32)], axis=0)
    sin_t = jnp.concatenate([sin_t, jnp.zeros((pad_rows, HEAD_DIM), F32)], axis=0)
    return cos_t, sin_t


def _key_chunks(n_keys):
    unit = MXU_TILE if n_keys % MXU_TILE == 0 else LANES
    units = n_keys // unit
    n_chunks = -(-n_keys // MAX_KEY_CHUNK)
    sizes = [(units // n_chunks + (c < units % n_chunks)) * unit for c in range(n_chunks)]
    return [(sum(sizes[:c]), sizes[c]) for c in range(n_chunks)]


def _attn_kernel(*refs, tq, chunks, group, n_src):
    q_ref = refs[0]
    k_refs, v_refs = refs[1:1 + 2 * n_src:2], refs[2:2 + 2 * n_src:2]
    o_ref, ke_scr, ve_scr, s_scr, m_scr, acc_scr = refs[-6:]

    @pl.when(pl.program_id(2) == 0)
    def _():
        r0 = 0
        for k_ref, v_ref in zip(k_refs, v_refs):
            n = k_ref.shape[0]
            ke_scr[r0:r0 + n, :] = k_ref[...]
            ve_scr[r0:r0 + n, :HEAD_DIM] = v_ref[...]
            r0 += n
        ve_scr[:, HEAD_DIM:] = jnp.ones((ve_scr.shape[0], HEAD_DIM), BF16)

    q = jnp.concatenate([q_ref[:, HEAD_DIM * g:HEAD_DIM * (g + 1)] for g in range(group)], axis=0)

    one = jnp.minimum(pl.program_id(2) + 1, 1)

    def first(_, carry):
        m = None
        for k0, nk in chunks:
            s = lax.dot_general(q, ke_scr[k0:k0 + nk, :], (((1,), (1,)), ((), ())), preferred_element_type=F32)
            s_scr[:, k0:k0 + nk] = s
            for t in range(nk // LANES):
                slab = s[:, LANES * t:LANES * (t + 1)]
                m = slab if m is None else jnp.maximum(m, slab)
        m_scr[...] = m
        return carry

    lax.fori_loop(0, one, first, 0)
    m_scr[...] = jnp.broadcast_to(jnp.max(m_scr[...], axis=-1, keepdims=True), m_scr.shape)

    def second(_, carry):
        m_rep = m_scr[...]
        acc = None
        for k0, nk in chunks:
            p = jnp.concatenate(
                [jnp.exp2(s_scr[:, k0 + LANES * t:k0 + LANES * (t + 1)] - m_rep).astype(BF16)
                 for t in range(nk // LANES)], axis=1)
            part = jnp.dot(p, ve_scr[k0:k0 + nk, :], preferred_element_type=F32)
            acc = part if acc is None else acc + part
        acc_scr[...] = acc
        return carry

    lax.fori_loop(0, one, second, 0)
    acc = acc_scr[...]
    o = acc[:, :HEAD_DIM] / acc[:, HEAD_DIM:]
    for g in range(group):
        o_ref[:, HEAD_DIM * g:HEAD_DIM * (g + 1)] = o[tq * g:tq * (g + 1)].astype(o_ref.dtype)


def _attention(q, k, main, prev, dims):
    nt = q.shape[0]
    attn_w, seq, ctx, batch = dims["attn_w"], dims["seq"], dims["ctx"], dims["batch"]
    latent = prev is None
    group = attn_w // HEAD_DIM // N_KV_HEADS
    gw = group * HEAD_DIM
    v_blk = dims["o_v"] // HEAD_DIM
    lq = seq if latent else ctx
    tq = _pick([lq], [128, 64, 32, 16])
    n_keys = ctx + seq if latent else ctx
    row0 = 0 if latent else dims["n_lat"] // tq
    ctx0 = dims["n_lat"] // ctx
    rows = group * tq
    in_specs = [
        pl.BlockSpec((tq, gw), lambda b, h, i: (row0 + b * (lq // tq) + i, h)),
        pl.BlockSpec((ctx, HEAD_DIM), lambda b, h, i: (ctx0 + b, h)),
        pl.BlockSpec((ctx, HEAD_DIM), lambda b, h, i: (ctx0 + b, v_blk + h)),
    ]
    args = [q, k, main]
    aliases = {}
    if latent:
        in_specs += [
            pl.BlockSpec((seq, HEAD_DIM), lambda b, h, i: (b, h)),
            pl.BlockSpec((seq, HEAD_DIM), lambda b, h, i: (b, v_blk + h)),
        ]
        args += [k, main]
    n_src = (len(args) - 1) // 2
    if not latent:
        in_specs.append(pl.BlockSpec(memory_space=pl.ANY))
        args.append(prev)
        aliases = {len(args) - 1: 0}
    scratch = [pltpu.VMEM((n_keys, HEAD_DIM), BF16), pltpu.VMEM((n_keys, 2 * HEAD_DIM), BF16),
               pltpu.VMEM((rows, n_keys), F32), pltpu.VMEM((rows, LANES), F32),
               pltpu.VMEM((rows, 2 * HEAD_DIM), F32)]
    return pl.pallas_call(
        functools.partial(_attn_kernel, tq=tq, chunks=_key_chunks(n_keys), group=group, n_src=n_src),
        grid=(batch, N_KV_HEADS, lq // tq),
        in_specs=in_specs,
        out_specs=pl.BlockSpec((tq, gw), lambda b, h, i: (row0 + b * (lq // tq) + i, h)),
        out_shape=jax.ShapeDtypeStruct((nt, attn_w), BF16),
        scratch_shapes=scratch,
        input_output_aliases=aliases,
        compiler_params=_params(("parallel", "parallel", "arbitrary")),
        name="attention_latent" if latent else "attention_context",
    )(*args)


def _ssm_params_kernel(lrr_ref, lir_ref, ldr_ref, lrc_ref, lic_ref, ldc_ref, btr_ref, bti_ref,
                       ctr_ref, cti_ref, r1_ref, v_ref, a_ref):
    t = SSM_CHUNK
    pw = SSM_GROUP * t
    r1_ref[...] = jnp.zeros(r1_ref.shape, r1_ref.dtype)
    v_ref[...] = jnp.zeros(v_ref.shape, v_ref.dtype)
    lane = lax.broadcasted_iota(jnp.int32, (SSM_GROUP, LANES), 1)
    col = lax.broadcasted_iota(jnp.int32, (SSM_GROUP, pw), 1)
    tau_col = (lax.broadcasted_iota(jnp.int32, (LANES, pw), 1) // SSM_GROUP).astype(F32)
    tau_row = lax.broadcasted_iota(jnp.int32, (t, LANES), 0).astype(F32)
    masks = (lane < SSM_STATE, lane >= SSM_STATE)
    toep = []
    for d in range(2):
        lr = jnp.minimum(lrr_ref[d], SSM_RE_MAX)
        li = lir_ref[d]
        dt = jnp.exp(ldr_ref[d])
        mag = jnp.exp(lr * dt)
        ab_re = mag * jnp.cos(li * dt)
        ab_im = mag * jnp.sin(li * dt)
        nr = ab_re - 1.0
        den = lr * lr + li * li
        f_re = (nr * lr + ab_im * li) / den
        f_im = (ab_im * lr - nr * li) / den
        bb_re = f_re * btr_ref[d] - f_im * bti_ref[d]
        bb_im = f_re * bti_ref[d] + f_im * btr_ref[d]
        e16 = jnp.exp(lr * dt * float(t))
        a_ref[2 * d:2 * d + 1, :] = e16 * jnp.cos(li * dt * float(t))
        a_ref[2 * d + 1:2 * d + 2, :] = e16 * jnp.sin(li * dt * float(t))
        tau_s = (t - 1.0) - tau_row if d == 0 else tau_row
        es = jnp.exp(lr * dt * tau_s)
        pw_re = es * jnp.cos(li * dt * tau_s)
        pw_im = es * jnp.sin(li * dt * tau_s)
        for s in range(t):
            ar = pw_re[s:s + 1]
            ai = pw_im[s:s + 1]
            w_re = ar * bb_re - ai * bb_im
            w_im = ar * bb_im + ai * bb_re
            for gi in range(2):
                r0 = gi * pw + SSM_GROUP * s
                c0 = 2 * pw + 2 * LANES * d
                r1_ref[r0:r0 + SSM_GROUP, c0:c0 + LANES] = jnp.where(masks[gi], w_re, 0.0).astype(r1_ref.dtype)
                r1_ref[r0:r0 + SSM_GROUP, c0 + LANES:c0 + 2 * LANES] = (
                    jnp.where(masks[gi], w_im, 0.0).astype(r1_ref.dtype))
        lrc = jnp.minimum(lrc_ref[d], SSM_RE_MAX)
        lic = lic_ref[d]
        dtc = jnp.exp(ldc_ref[d])
        tau_q = tau_col if d == 0 else (t - 1.0) - tau_col
        eq = jnp.exp(lrc * dtc * tau_q)
        p_re = eq * jnp.cos(lic * dtc * tau_q)
        p_im = eq * jnp.sin(lic * dtc * tau_q)
        c_re = ctr_ref[d]
        c_im = cti_ref[d]
        q_re = p_re * c_re - p_im * c_im
        q_im = p_re * c_im + p_im * c_re
        per_group = []
        for gi in range(2):
            k_mat = (jnp.dot(jnp.where(masks[gi], bb_re, 0.0), q_re, precision=lax.Precision.HIGHEST,
                             preferred_element_type=F32)
                     - jnp.dot(jnp.where(masks[gi], bb_im, 0.0), q_im, precision=lax.Precision.HIGHEST,
                               preferred_element_type=F32))
            per_group.append(k_mat)
        toep.append(per_group)
        magc = jnp.exp(lrc * dtc)
        abc_re = magc * jnp.cos(lic * dtc)
        abc_im = magc * jnp.sin(lic * dtc)
        q1_re = abc_re * q_re - abc_im * q_im
        q1_im = abc_re * q_im + abc_im * q_re
        for gi in range(2):
            rows = slice(SSM_STATE * gi, SSM_STATE * (gi + 1))
            r0 = 2 * LANES * d + SSM_STATE * gi
            v_ref[r0:r0 + SSM_STATE, pw * gi:pw * (gi + 1)] = q1_re[rows].astype(v_ref.dtype)
            v_ref[r0 + LANES:r0 + LANES + SSM_STATE, pw * gi:pw * (gi + 1)] = (-q1_im[rows]).astype(v_ref.dtype)
    for gi in range(2):
        k_f, k_b = toep[0][gi], toep[1][gi]
        for s in range(t):
            sh = SSM_GROUP * s
            f = k_f if sh == 0 else jnp.where(col >= sh, pltpu.roll(k_f, sh, 1), 0.0)
            back = SSM_GROUP * (t - 1 - s)
            b = k_b if back == 0 else jnp.where(col < pw - back, pltpu.roll(k_b, pw - back, 1), 0.0)
            r0 = gi * pw + sh
            r1_ref[r0:r0 + SSM_GROUP, pw * gi:pw * (gi + 1)] = (f + b).astype(r1_ref.dtype)


def _pair_rows(a):
    depth, _, groups, n = a.shape
    return a.reshape(depth, 2, groups // 2, 2 * n).transpose(0, 2, 1, 3)


def _ssm_parameters(lam_re, lam_im, log_dt, b_re, b_im, c_re, c_im):
    depth, _, groups, n = lam_re.shape
    pairs = groups // 2
    t = SSM_CHUNK
    pw = SSM_GROUP * t
    ld = jnp.broadcast_to(log_dt[..., None], lam_re.shape)
    rows = [_pair_rows(a.astype(F32))[:, :, :, None, :] for a in (lam_re, lam_im, ld)]
    cols = [_pair_rows(a.astype(F32))[:, :, :, :, None] for a in (lam_re, lam_im, ld)]

    def bt(b):
        return (b.astype(F32).reshape(depth, 2, pairs, 2, n, SSM_GROUP).transpose(0, 2, 1, 5, 3, 4)
                .reshape(depth, pairs, 2, SSM_GROUP, 2 * n))

    def ct(c):
        c = (c.astype(F32).reshape(depth, 2, pairs, 2, SSM_GROUP, n).transpose(0, 2, 1, 3, 5, 4)
             .reshape(depth, pairs, 2, 2 * n, SSM_GROUP))
        return jnp.tile(c, (1, 1, 1, 1, t))

    row_spec = pl.BlockSpec((None, None, 2, 1, 2 * n), lambda l, j: (l, j, 0, 0, 0))
    col_spec = pl.BlockSpec((None, None, 2, 2 * n, 1), lambda l, j: (l, j, 0, 0, 0))
    bt_spec = pl.BlockSpec((None, None, 2, SSM_GROUP, 2 * n), lambda l, j: (l, j, 0, 0, 0))
    ct_spec = pl.BlockSpec((None, None, 2, 2 * n, pw), lambda l, j: (l, j, 0, 0, 0))
    return pl.pallas_call(
        _ssm_params_kernel,
        grid=(depth, pairs),
        in_specs=[row_spec] * 3 + [col_spec] * 3 + [bt_spec] * 2 + [ct_spec] * 2,
        out_specs=[
            pl.BlockSpec((None, None, 2 * pw, 4 * pw), lambda l, j: (l, j, 0, 0)),
            pl.BlockSpec((None, None, 8 * n, 2 * pw), lambda l, j: (l, j, 0, 0)),
            pl.BlockSpec((None, 4, 2 * n), lambda l, j: (l, 0, j)),
        ],
        out_shape=[
            jax.ShapeDtypeStruct((depth, pairs, 2 * pw, 4 * pw), BF16),
            jax.ShapeDtypeStruct((depth, pairs, 8 * n, 2 * pw), BF16),
            jax.ShapeDtypeStruct((depth, 4, groups * n), F32),
        ],
        compiler_params=_params(("parallel", "parallel")),
        name="ssm_parameters",
    )(*rows, *cols, bt(b_re), bt(b_im), ct(c_re), ct(c_im))


def _chunk_rows(u_ref):
    return jnp.concatenate([u_ref[t] for t in range(u_ref.shape[0])], axis=-1)


def _scan_chunk_states(s_ref, a_ref, h_ref, *, batch, lat_tiles, ctx_tiles):
    w = s_ref.shape[-1]
    per_tile = SUBLANES // batch
    row_grp = lax.broadcasted_iota(jnp.int32, (SUBLANES, w), 0) // batch
    zero = jnp.zeros((SUBLANES, w), F32)

    def make_tile(k, order, shift):
        ar = a_ref[k:k + 1, :]
        ai = a_ref[k + 1:k + 2, :]

        def tile(ti, carry):
            cur_re, cur_im = carry
            rows = pl.ds(pl.multiple_of(ti * SUBLANES, SUBLANES), SUBLANES)
            s_re = s_ref[k, rows, :]
            s_im = s_ref[k + 1, rows, :]
            h_re, h_im = zero, zero
            for pos in order:
                sel = row_grp == pos
                h_re = jnp.where(sel, cur_re, h_re)
                h_im = jnp.where(sel, cur_im, h_im)
                nxt_re = ar * cur_re - ai * cur_im + s_re
                nxt_im = ar * cur_im + ai * cur_re + s_im
                if batch == SUBLANES:
                    cur_re, cur_im = nxt_re, nxt_im
                else:
                    cur_re = pltpu.roll(nxt_re, shift, 0)
                    cur_im = pltpu.roll(nxt_im, shift, 0)
            h_ref[k, rows, :] = h_re
            h_ref[k + 1, rows, :] = h_im
            return cur_re, cur_im

        return tile

    fwd = make_tile(0, range(per_tile), batch)
    bwd = make_tile(2, range(per_tile - 1, -1, -1), SUBLANES - batch)

    def both(first_tile, n_tiles):
        def step(i, carry):
            return fwd(first_tile + i, carry[0]), bwd(first_tile + n_tiles - 1 - i, carry[1])
        return step

    carry = lax.fori_loop(0, ctx_tiles, both(lat_tiles, ctx_tiles), ((zero, zero), (zero, zero)))
    lax.fori_loop(0, lat_tiles, both(0, lat_tiles), carry)


def _ssm_core_kernel(u_ref, r1_ref, v_ref, a_ref, d_ref, y_ref, s_scr, h_scr, *, batch, lat_tiles, ctx_tiles):
    n_t = u_ref.shape[0]
    kw = n_t * LANES
    u = _chunk_rows(u_ref)
    r = jnp.dot(u, r1_ref[...], preferred_element_type=F32)
    for t in range(n_t):
        y_ref[t] = r[:, LANES * t:LANES * (t + 1)]
    for k in range(4):
        s_scr[k] = r[:, kw + LANES * k:kw + LANES * (k + 1)]
    _scan_chunk_states(s_scr, a_ref, h_scr, batch=batch, lat_tiles=lat_tiles, ctx_tiles=ctx_tiles)
    h = jnp.concatenate([h_scr[k] for k in range(4)], axis=-1).astype(BF16)
    y = jnp.dot(h, v_ref[...], preferred_element_type=F32) + d_ref[...] * u.astype(F32)
    for t in range(n_t):
        y_ref[t] += y[:, LANES * t:LANES * (t + 1)]


def _ssm_core(u, r1, v, a16, dsk, l, dims):
    pairs, n_t, nc, _ = u.shape
    kw = n_t * LANES
    batch = dims["batch"]
    lat_tiles = dims["seq"] // SSM_CHUNK * batch // SUBLANES
    ctx_tiles = dims["ctx"] // SSM_CHUNK * batch // SUBLANES
    return pl.pallas_call(
        functools.partial(_ssm_core_kernel, batch=batch, lat_tiles=lat_tiles, ctx_tiles=ctx_tiles),
        grid=(pairs,),
        in_specs=[
            pl.BlockSpec((None, n_t, nc, LANES), lambda j: (j, 0, 0, 0)),
            pl.BlockSpec((None, None, kw, 2 * kw), lambda j: (l, j, 0, 0)),
            pl.BlockSpec((None, None, 4 * LANES, kw), lambda j: (l, j, 0, 0)),
            pl.BlockSpec((None, 4, LANES), lambda j: (l, 0, j)),
            pl.BlockSpec((None, None, 1, kw), lambda j: (l, j, 0, 0)),
        ],
        out_specs=pl.BlockSpec((None, n_t, nc, LANES), lambda j: (j, 0, 0, 0)),
        out_shape=jax.ShapeDtypeStruct((pairs, n_t, nc, LANES), F32),
        scratch_shapes=[pltpu.VMEM((4, nc, LANES), F32), pltpu.VMEM((4, nc, LANES), F32)],
        compiler_params=_params(("parallel",)),
        name="ssm_core",
    )(u, r1, v, a16, dsk)


GROUPS_PER_TILE = LANES // SSM_GROUP


def _lane_slot(rows):
    return lax.broadcasted_iota(jnp.int32, (rows, LANES), 1) // SSM_GROUP


def _block_transposes(sets, slot):
    sets = [list(t) for t in sets]
    d = GROUPS_PER_TILE // 2
    while d:
        low = (slot & d) == 0
        for t in sets:
            for a in range(GROUPS_PER_TILE):
                if not a & d:
                    lo, hi = t[a], t[a + d]
                    t[a] = jnp.where(low, lo, pltpu.roll(hi, d * SSM_GROUP, 1))
                    t[a + d] = jnp.where(low, pltpu.roll(lo, LANES - d * SSM_GROUP, 1), hi)
        d //= 2
    return sets


def _chunk_tiling(dims):
    ncl, ncc = dims["seq"] // SSM_CHUNK, dims["ctx"] // SSM_CHUNK
    ct = _pick([ncl, ncc], [16, 8, 4, 2, 1])
    nlt, nct = ncl // ct, ncc // ct

    def token_block(k, b):
        return jnp.where(k < nlt, b * nlt + k, dims["batch"] * nlt + b * nct + (k - nlt))

    return ct, nlt + nct, token_block


def _ssm_pack_kernel(*refs, batch, ct):
    x_refs, o_ref, scr = refs[:batch], refs[batch], refs[batch + 1]
    slot = _lane_slot(ct)
    halves = SSM_CHUNK // GROUPS_PER_TILE
    for b in range(batch):
        where = [(j, half) for j in range(x_refs[b].shape[0]) for half in range(halves)]
        sets = [[x_refs[b][j, pl.ds(GROUPS_PER_TILE * half + k, ct, stride=SSM_CHUNK), :]
                 for k in range(GROUPS_PER_TILE)] for j, half in where]
        for (j, half), outs in zip(where, _block_transposes(sets, slot)):
            for gg, out in enumerate(outs):
                g = GROUPS_PER_TILE * j + gg
                scr[g // 2, (g % 2) * halves + half, pl.ds(b, ct, stride=batch), :] = out
    o_ref[...] = scr[...].astype(o_ref.dtype)


def _ssm_pack(u32, dims):
    batch, pairs = dims["batch"], dims["pairs"]
    ct, n_tiles, token_block = _chunk_tiling(dims)
    n_in, nt, _ = u32.shape
    n_t = 2 * SSM_CHUNK * SSM_GROUP // LANES
    return pl.pallas_call(
        functools.partial(_ssm_pack_kernel, batch=batch, ct=ct),
        grid=(n_tiles,),
        in_specs=[pl.BlockSpec((n_in, ct * SSM_CHUNK, LANES),
                               functools.partial(lambda k, b: (0, token_block(k, b), 0), b=b))
                  for b in range(batch)],
        out_specs=pl.BlockSpec((pairs, n_t, ct * batch, LANES), lambda k: (0, 0, k, 0)),
        out_shape=jax.ShapeDtypeStruct((pairs, n_t, nt // SSM_CHUNK, LANES), BF16),
        scratch_shapes=[pltpu.VMEM((pairs, n_t, ct * batch, LANES), F32)],
        compiler_params=_params(("parallel",)),
        name="ssm_pack",
    )(*([u32] * batch))


def _ssm_unpack_kernel(y_ref, o_ref, *, batch, ct):
    b = pl.program_id(1)
    slot = _lane_slot(ct)
    halves = SSM_CHUNK // GROUPS_PER_TILE
    where = [(j, half) for j in range(o_ref.shape[0]) for half in range(halves)]
    sets = []
    for j, half in where:
        groups = [GROUPS_PER_TILE * j + gg for gg in range(GROUPS_PER_TILE)]
        sets.append([y_ref[g // 2, (g % 2) * halves + half, pl.ds(b, ct, stride=batch), :] for g in groups])
    for (j, half), outs in zip(where, _block_transposes(sets, slot)):
        for k, out in enumerate(outs):
            o_ref[j, pl.ds(GROUPS_PER_TILE * half + k, ct, stride=SSM_CHUNK), :] = out


def _ssm_unpack(y, dims):
    batch = dims["batch"]
    pairs, n_t, nc, _ = y.shape
    ct, n_tiles, token_block = _chunk_tiling(dims)
    n_out = dims["ssm_w"] // LANES
    return pl.pallas_call(
        functools.partial(_ssm_unpack_kernel, batch=batch, ct=ct),
        grid=(n_tiles, batch),
        in_specs=[pl.BlockSpec((pairs, n_t, ct * batch, LANES), lambda k, b: (0, 0, k, 0))],
        out_specs=pl.BlockSpec((n_out, ct * SSM_CHUNK, LANES), lambda k, b: (0, token_block(k, b), 0)),
        out_shape=jax.ShapeDtypeStruct((n_out, nc * SSM_CHUNK, LANES), F32),
        compiler_params=_params(("parallel", "arbitrary")),
        name="ssm_unpack",
    )(y)


def _merge_kernel(pb_ref, pc_ref, pv_ref, hpc_ref, hpv_ref, hnc_ref, hnv_ref, cw_ref, ys_ref, at_ref,
                  g_ref, wc_ref, wg_ref, wa_ref, wo_ref, x_ref, gate_ref, gpost_ref, gn_ref, shn_ref, scn_ref,
                  o_ref, h_ref, m_scr, *, tm, tn, seq, ctx, n_lat):
    d = o_ref.shape[1]
    r0 = pl.program_id(0) * tm
    period = jnp.where(r0 >= n_lat, ctx, seq)
    starts = r0 % period == 0
    ends = (r0 + tm) % period == 0
    w = pc_ref[...].astype(F32) * pv_ref[...].astype(F32)
    prev = (hpc_ref[...].astype(F32) * hpv_ref[...].astype(F32))[HALO_ROWS - 1:HALO_ROWS]
    nxt = (hnc_ref[...].astype(F32) * hnv_ref[...].astype(F32))[0:1]
    prev = jnp.where(starts, 0.0, prev)
    nxt = jnp.where(ends, 0.0, nxt)
    row = lax.broadcasted_iota(jnp.int32, w.shape, 0)
    up = jnp.where(row == 0, prev, pltpu.roll(w, 1, 0))
    dn = jnp.where(row == tm - 1, nxt, pltpu.roll(w, tm - 1, 0))
    cw = cw_ref[...]
    cf = (pb_ref[...].astype(F32) * (cw[0:1] * up + cw[1:2] * w + cw[2:3] * dn)).astype(BF16)
    yg = jax.nn.gelu(_chunk_rows(ys_ref)).astype(BF16)
    at = at_ref[...]
    for c in range(d // tn):
        cols = slice(tn * c, tn * (c + 1))
        gate = lambda k: jax.nn.sigmoid(g_ref[:, k * d + tn * c:k * d + tn * (c + 1)].astype(F32))
        ya = jnp.dot(cf, wc_ref[:, cols], preferred_element_type=F32)
        yc = jnp.dot(at, wa_ref[:, cols], preferred_element_type=F32)
        ys = jnp.dot(yg, wg_ref[:, cols], preferred_element_type=F32) * jax.nn.sigmoid(
            jnp.dot(yg, wg_ref[:, d + tn * c:d + tn * (c + 1)], preferred_element_type=F32))
        m_scr[:, cols] = (gate(0) * ya + gate(1) * ys + gate(2) * yc).astype(m_scr.dtype)
    r = jnp.dot(m_scr[...], wo_ref[...], preferred_element_type=F32)
    x = x_ref[...] + gate_ref[...] * _rms(r, gpost_ref[...])
    o_ref[...] = x
    h_ref[...] = _norm_mod(x, gn_ref[...], shn_ref[...], scn_ref[...])


def _merge(main, gates, ys, attn, conv_w, w_conv_out, w_glu, w_attn_out, w_out, xt, modr, g_post, g_pre_mlp,
           l, dims, rows):
    d = dims["d"]
    cw_, sw, aw = dims["conv_w"], dims["ssm_w"], dims["attn_w"]
    tm = _pick([dims["seq"], dims["ctx"]], [512, 256, 128, 64, 32])
    tn = _pick([d], [512, 256, 128])
    n_halo = main.shape[0] // HALO_ROWS
    per = tm // HALO_ROWS
    prev_blk = lambda i: jnp.maximum(i * per - 1, 0)
    next_blk = lambda i: jnp.minimum((i + 1) * per, n_halo - 1)
    grp = _row_group(dims, tm)
    mod = lambda k: pl.BlockSpec((None, None, 1, d), lambda i: (l, grp(i), 0, k))
    vec = pl.BlockSpec((None, 1, d), lambda i: (l, 0, 0))
    row_tile = pl.BlockSpec((tm, d), lambda i: (i, 0))
    resident = lambda shape: pl.BlockSpec((None,) + shape, lambda i: (l, 0, 0), pipeline_mode=pl.Buffered(1))
    return pl.pallas_call(
        functools.partial(_merge_kernel, tm=tm, tn=tn, seq=dims["seq"], ctx=dims["ctx"], n_lat=dims["n_lat"]),
        grid=(rows // tm,),
        in_specs=[
            pl.BlockSpec((tm, cw_), lambda i: (i, 0)),
            pl.BlockSpec((tm, cw_), lambda i: (i, 1)),
            pl.BlockSpec((tm, cw_), lambda i: (i, 2)),
            pl.BlockSpec((HALO_ROWS, cw_), lambda i: (prev_blk(i), 1)),
            pl.BlockSpec((HALO_ROWS, cw_), lambda i: (prev_blk(i), 2)),
            pl.BlockSpec((HALO_ROWS, cw_), lambda i: (next_blk(i), 1)),
            pl.BlockSpec((HALO_ROWS, cw_), lambda i: (next_blk(i), 2)),
            resident((CONV_K, cw_)),
            pl.BlockSpec((sw // LANES, tm, LANES), lambda i: (0, i, 0)),
            pl.BlockSpec((tm, aw), lambda i: (i, 0)),
            pl.BlockSpec((tm, 3 * d), lambda i: (i, 0)),
            resident((cw_, d)),
            resident((sw, 2 * d)),
            resident((aw, d)),
            resident((d, d)),
            row_tile, mod(2), vec, vec, mod(3), mod(4),
        ],
        out_specs=[row_tile, row_tile],
        out_shape=[jax.ShapeDtypeStruct(xt.shape, F32), jax.ShapeDtypeStruct((rows, d), BF16)],
        scratch_shapes=[pltpu.VMEM((tm, d), BF16)],
        input_output_aliases={15: 0},
        compiler_params=_params(("parallel",)),
        name="mixer_output",
    )(main, main, main, main, main, main, main, conv_w, ys, attn, gates, w_conv_out, w_glu, w_attn_out,
      w_out, xt, modr, g_post, g_pre_mlp, modr, modr)


def _mlp_kernel(x_ref, h_ref, wu_ref, wd_ref, gate_ref, gpost_ref, *rest, with_next):
    j = pl.program_id(1)
    if with_next:
        gn_ref, shn_ref, scn_ref, o_ref, hn_ref, acc_scr = rest
    else:
        o_ref, acc_scr = rest

    last = pl.num_programs(1) - 1

    def partial_out(rows):
        u = jnp.maximum(jnp.dot(h_ref[rows, :], wu_ref[...], preferred_element_type=F32), 0.0)
        return jnp.dot((u * u).astype(BF16), wd_ref[...], preferred_element_type=F32)

    every = slice(None)

    @pl.when(j == 0)
    def _():
        acc_scr[...] = partial_out(every)

    @pl.when(jnp.logical_and(j > 0, j < last))
    def _():
        acc_scr[...] += partial_out(every)

    @pl.when(j == last)
    def _():
        part = acc_scr.shape[0] // MLP_TAIL_PARTS
        for k in range(MLP_TAIL_PARTS):
            rows = slice(part * k, part * (k + 1))
            r = acc_scr[rows, :] + partial_out(rows)
            x = x_ref[rows, :] + gate_ref[...] * _rms(r, gpost_ref[...])
            o_ref[rows, :] = x
            if with_next:
                hn_ref[rows, :] = _norm_mod(x, gn_ref[...], shn_ref[...], scn_ref[...])


def _mlp(xt, h, g_post, modr, w_up, w_down, l, dims, rows, g_next=None):
    d = xt.shape[1]
    d_ff = w_up.shape[-1]
    tm = _pick([dims["seq"], dims["n_ctx"]], [512, 256, 128])
    tf = _pick([d_ff], [1024, 512, 256, 128])
    grp = _row_group(dims, tm)
    mod = lambda layer, k: pl.BlockSpec((None, None, 1, d), lambda i, j: (layer, grp(i), 0, k))
    vec = lambda layer: pl.BlockSpec((None, 1, d), lambda i, j: (layer, 0, 0))
    row_tile = pl.BlockSpec((tm, d), lambda i, j: (i, 0))
    in_specs = [
        row_tile, row_tile,
        pl.BlockSpec((None, d, tf), lambda i, j: (l, 0, j)),
        pl.BlockSpec((None, tf, d), lambda i, j: (l, j, 0)),
        mod(l, 5), vec(l),
    ]
    args = [xt, h, w_up, w_down, modr, g_post]
    out_specs, out_shape = [row_tile], [jax.ShapeDtypeStruct((rows, d), F32)]
    if g_next is not None:
        assert rows == xt.shape[0]
        in_specs += [vec(l + 1), mod(l + 1, 0), mod(l + 1, 1)]
        args += [g_next, modr, modr]
        out_specs.append(row_tile)
        out_shape.append(jax.ShapeDtypeStruct((rows, d), BF16))
    return pl.pallas_call(
        functools.partial(_mlp_kernel, with_next=g_next is not None),
        grid=(rows // tm, d_ff // tf),
        in_specs=in_specs,
        out_specs=out_specs,
        out_shape=out_shape,
        scratch_shapes=[pltpu.VMEM((tm, d), F32)],
        input_output_aliases={0: 0} if g_next is not None else {},
        compiler_params=_params(("parallel", "arbitrary")),
        name="mlp",
    )(*args)


def kernel(x, c, ctx, c_ctx, w_mod, b_mod, g_pre_mix, g_post_mix, g_pre_mlp, g_post_mlp, w_in, conv_w,
           w_conv_out, ssm_lam_re, ssm_lam_im, ssm_log_dt, ssm_b_re, ssm_b_im, ssm_c_re, ssm_c_im, ssm_d,
           w_glu, q_gain, k_gain, w_attn_out, w_out, w_up, w_down):
    batch, seq, d = x.shape
    ctx_len = ctx.shape[1]
    depth = w_in.shape[0]
    in_w = w_in.shape[-1]
    conv_width = conv_w.shape[-1]
    ssm_w = ssm_d.shape[-1]
    attn_w = w_attn_out.shape[1]
    gate_w = 3 * d
    kv_w = (in_w - 3 * conv_width - ssm_w - attn_w - gate_w) // 2
    groups = ssm_lam_re.shape[2]
    assert batch + 1 <= MOD_ROWS and SUBLANES % batch == 0
    assert kv_w == N_KV_HEADS * HEAD_DIM and ssm_lam_re.shape[3] == SSM_STATE
    assert ssm_w == groups * SSM_GROUP and groups % 2 == 0
    assert seq % GRID_W == 0 and (batch * seq) % ctx_len == 0 and ctx_len % LANES == 0
    a_w = 3 * conv_width + ssm_w
    dims = dict(
        batch=batch, seq=seq, ctx=ctx_len, d=d, n_lat=batch * seq, n_ctx=batch * ctx_len,
        conv_w=conv_width, ssm_w=ssm_w, attn_w=attn_w, kv_w=kv_w, gate_w=gate_w, pairs=groups // 2,
        o_q=gate_w, o_k=gate_w + attn_w, o_v=gate_w + attn_w + kv_w,
    )
    n_lat, nt = dims["n_lat"], dims["n_lat"] + dims["n_ctx"]

    cc = jnp.concatenate([c, c_ctx[None, :], jnp.zeros((MOD_ROWS - batch - 1, d), F32)], axis=0)
    mod = _modulation(cc, w_mod, b_mod)
    modr = mod.reshape(depth, MOD_ROWS, 1, N_MOD * d)

    row_vec = lambda a: a.reshape(depth, 1, a.shape[-1])
    g_pre_mix, g_post_mix, g_pre_mlp, g_post_mlp = map(row_vec, (g_pre_mix, g_post_mix, g_pre_mlp, g_post_mlp))
    perm = _head_permutation()
    q_gain, k_gain = row_vec(q_gain[:, perm]), row_vec(k_gain[:, perm])
    w_in_a = w_in[:, :, :a_w].astype(BF16)
    w_qk = w_in[:, :, a_w:a_w + attn_w + kv_w].reshape(depth, d, -1, HEAD_DIM)[..., perm]
    w_in_b = jnp.concatenate([w_in[:, :, in_w - gate_w:], w_qk.reshape(depth, d, attn_w + kv_w),
                              w_in[:, :, a_w + attn_w + kv_w:in_w - gate_w]], axis=-1).astype(BF16)
    w_conv_out, w_glu, w_attn_out, w_out, w_up, w_down = (
        w.astype(BF16) for w in (w_conv_out, w_glu, w_attn_out, w_out, w_up, w_down))
    tn_a = _pick([a_w], [1280, 768, 512, 256, 128])
    tn_b = _pick([w_in_b.shape[-1]], [1536, 768, 512, 256, 128])

    cos_t, sin_t = _rope_tables(seq, _qk_tile(dims))
    r1, v_mat, a16 = _ssm_parameters(ssm_lam_re, ssm_lam_im, ssm_log_dt, ssm_b_re, ssm_b_im, ssm_c_re, ssm_c_im)
    dsk = jnp.broadcast_to(
        ssm_d.astype(F32).reshape(depth, groups // 2, 2, 1, SSM_GROUP),
        (depth, groups // 2, 2, SSM_CHUNK, SSM_GROUP)).reshape(depth, groups // 2, 1, 2 * SSM_CHUNK * SSM_GROUP)

    xt, h = _prenorm(x.reshape(n_lat, d), ctx.reshape(dims["n_ctx"], d), g_pre_mix, modr, dims)
    for l in range(depth):
        need_ctx = l < depth - 1
        rows = nt if need_ctx else n_lat
        proj_a, u32 = _projection(h, w_in_a, l, dims, tn_a, f32_cols=(3 * conv_width, ssm_w))
        (proj_b,) = _projection(h, w_in_b, l, dims, tn_b)
        u = _ssm_pack(u32, dims)
        ys = _ssm_unpack(_ssm_core(u, r1, v_mat, a16, dsk, l, dims), dims)
        q, k = _qk_prepare(proj_b, cos_t, sin_t, q_gain, k_gain, l, dims)
        attn = _attention(q, k, proj_b, None, dims)
        if need_ctx:
            attn = _attention(q, k, proj_b, attn, dims)
        xt, h = _merge(proj_a, proj_b, ys, attn, conv_w, w_conv_out, w_glu, w_attn_out, w_out, xt, modr,
                       g_post_mix, g_pre_mlp, l, dims, rows)
        if need_ctx:
            xt, h = _mlp(xt, h, g_post_mlp, modr, w_up, w_down, l, dims, rows, g_next=g_pre_mix)
        else:
            (xt,) = _mlp(xt, h, g_post_mlp, modr, w_up, w_down, l, dims, rows)
    return xt.reshape(batch, seq, d)
```

```python
import functools
import math

import jax
import jax.numpy as jnp
from jax import lax
from jax.experimental import pallas as pl
from jax.experimental.pallas import tpu as pltpu

F32 = jnp.float32
BF16 = jnp.bfloat16

NORM_EPS = 1e-6
N_MOD = 6
HEAD_DIM = 128
N_KV_HEADS = 2
GRID_W = 64
ROPE_BASE = 10000.0
CONV_K = 3
SSM_GROUP = 16
SSM_STATE = 64
SSM_RE_MAX = -1e-4
SSM_CHUNK = 16
SUBLANES = 8
LANES = 128
HALO_ROWS = 16
MXU_TILE = 256
MAX_KEY_CHUNK = 4352
MLP_TAIL_PARTS = 2
MOD_ROWS = 8
VMEM_LIMIT = 56 * 2**20


def _params(sem, vmem=VMEM_LIMIT):
    return pltpu.CompilerParams(dimension_semantics=sem, vmem_limit_bytes=vmem)


def _pick(dims, cands):
    for c in cands:
        if all(d % c == 0 for d in dims):
            return c
    raise ValueError(f"no tile in {cands} divides {dims}")


def _rms(x, g):
    ms = jnp.mean(x * x, axis=-1, keepdims=True)
    return x * lax.rsqrt(ms + NORM_EPS) * g


def _mod_kernel(c_ref, w_ref, b_ref, o_ref):
    c = c_ref[...]
    s = (c * jax.nn.sigmoid(c)).astype(BF16)
    o_ref[...] = jnp.dot(s, w_ref[...].astype(BF16), preferred_element_type=F32) + b_ref[...]


def _modulation(cc, w_mod, b_mod):
    depth, d, nout = w_mod.shape
    tn = _pick([nout], [1024, 512, 256, 128])
    return pl.pallas_call(
        _mod_kernel,
        grid=(depth, nout // tn),
        in_specs=[
            pl.BlockSpec((MOD_ROWS, d), lambda l, j: (0, 0)),
            pl.BlockSpec((None, d, tn), lambda l, j: (l, 0, j)),
            pl.BlockSpec((None, 1, tn), lambda l, j: (l, 0, j)),
        ],
        out_specs=pl.BlockSpec((None, MOD_ROWS, tn), lambda l, j: (l, 0, j)),
        out_shape=jax.ShapeDtypeStruct((depth, MOD_ROWS, nout), F32),
        compiler_params=_params(("parallel", "parallel")),
        name="modulation",
    )(cc, w_mod, b_mod.reshape(depth, 1, nout))


def _norm_mod(x, g, shift, scale):
    return (_rms(x, g) * (1.0 + scale) + shift).astype(BF16)


def _prenorm_kernel(x_ref, c_ref, g_ref, sh_ref, sc_ref, xt_ref, h_ref, *, lat_tiles):
    x = jnp.where(pl.program_id(0) < lat_tiles, x_ref[...], c_ref[...])
    xt_ref[...] = x
    h_ref[...] = _norm_mod(x, g_ref[...], sh_ref[...], sc_ref[...])


def _row_group(dims, tm):
    return lambda i: jnp.minimum((i * tm) // dims["seq"], dims["batch"])


def _prenorm(x, ctx, gain, modr, dims):
    d = x.shape[1]
    nt = x.shape[0] + ctx.shape[0]
    tm = _pick([dims["seq"], dims["n_ctx"]], [512, 256, 128])
    lat_tiles = x.shape[0] // tm
    grp = _row_group(dims, tm)
    row_tile = pl.BlockSpec((tm, d), lambda i: (i, 0))
    return pl.pallas_call(
        functools.partial(_prenorm_kernel, lat_tiles=lat_tiles),
        grid=(nt // tm,),
        in_specs=[
            pl.BlockSpec((tm, d), lambda i: (jnp.minimum(i, lat_tiles - 1), 0)),
            pl.BlockSpec((tm, d), lambda i: (jnp.maximum(i - lat_tiles, 0), 0)),
            pl.BlockSpec((None, 1, d), lambda i: (0, 0, 0)),
            pl.BlockSpec((None, None, 1, d), lambda i: (0, grp(i), 0, 0)),
            pl.BlockSpec((None, None, 1, d), lambda i: (0, grp(i), 0, 1)),
        ],
        out_specs=[row_tile, row_tile],
        out_shape=[jax.ShapeDtypeStruct((nt, d), F32), jax.ShapeDtypeStruct((nt, d), BF16)],
        compiler_params=_params(("parallel",)),
        name="prenorm",
    )(x, ctx, gain, modr, modr)


def _proj_kernel(h_ref, w_ref, o_ref, *u_refs, u_tile, u_off):
    r = jnp.dot(h_ref[...], w_ref[...], preferred_element_type=F32)
    o_ref[...] = r.astype(o_ref.dtype)
    if u_refs:
        (ou_ref,) = u_refs

        @pl.when(pl.program_id(1) == u_tile)
        def _():
            for t in range(ou_ref.shape[0]):
                ou_ref[t] = r[:, u_off + LANES * t:u_off + LANES * (t + 1)]


def _projection(h, w, l, dims, tn, f32_cols=None):
    nt, d = h.shape
    n_out = w.shape[-1]
    tm = _pick([dims["seq"], dims["n_ctx"]], [1024, 512, 256, 128])
    assert n_out % tn == 0
    out_specs = [pl.BlockSpec((tm, tn), lambda i, j: (i, j))]
    out_shape = [jax.ShapeDtypeStruct((nt, n_out), BF16)]
    u_tile = u_off = 0
    if f32_cols is not None:
        start, width = f32_cols
        u_tile, u_off = start // tn, start % tn
        assert u_off + width <= tn and u_off % LANES == 0 and width % LANES == 0
        out_specs.append(pl.BlockSpec((width // LANES, tm, LANES), lambda i, j: (0, i, 0)))
        out_shape.append(jax.ShapeDtypeStruct((width // LANES, nt, LANES), F32))
    return pl.pallas_call(
        functools.partial(_proj_kernel, u_tile=u_tile, u_off=u_off),
        grid=(nt // tm, n_out // tn),
        in_specs=[
            pl.BlockSpec((tm, d), lambda i, j: (i, 0)),
            pl.BlockSpec((None, d, tn), lambda i, j: (l, 0, j)),
        ],
        out_specs=out_specs,
        out_shape=out_shape,
        compiler_params=_params(("parallel", "arbitrary")),
        name="in_projection",
    )(h, w)


def _head_permutation():
    quarter = HEAD_DIM // 4
    return jnp.concatenate([jnp.arange(quarter * k, quarter * (k + 1)) for k in (0, 2, 1, 3)])


def _qk_kernel(xq_ref, xk_ref, cos_ref, sin_ref, qg_ref, kg_ref, q_ref, k_ref, *, n_q, n_kv, scale):
    cos = cos_ref[...]
    sin = sin_ref[...]
    mean_cols = jnp.full((HEAD_DIM, HEAD_DIM), 1.0 / HEAD_DIM, BF16)
    for h in range(n_q + n_kv):
        if h < n_q:
            xh = xq_ref[:, HEAD_DIM * h:HEAD_DIM * (h + 1)].astype(F32)
        else:
            xh = xk_ref[:, HEAD_DIM * (h - n_q):HEAD_DIM * (h - n_q + 1)].astype(F32)
        gain = qg_ref[...] * scale if h < n_q else kg_ref[...]
        sq = xh * xh
        hi = sq.astype(BF16)
        lo = (sq - hi.astype(F32)).astype(BF16)
        ms = (jnp.dot(hi, mean_cols, preferred_element_type=F32)
              + jnp.dot(lo, mean_cols, preferred_element_type=F32))
        y = xh * lax.rsqrt(ms + NORM_EPS) * gain
        r = (y * cos + pltpu.roll(y, HEAD_DIM // 2, 1) * sin).astype(BF16)
        if h < n_q:
            q_ref[:, HEAD_DIM * h:HEAD_DIM * (h + 1)] = r
        else:
            k_ref[:, HEAD_DIM * (h - n_q):HEAD_DIM * (h - n_q + 1)] = r


def _qk_tile(dims):
    return _pick([dims["seq"], dims["n_ctx"]], [512, 256, 128, 64, 32])


def _qk_prepare(proj, cos_t, sin_t, q_gain, k_gain, l, dims):
    nt = proj.shape[0]
    attn_w, kv_w = dims["attn_w"], dims["kv_w"]
    tm = _qk_tile(dims)
    assert dims["o_q"] % attn_w == 0 and dims["o_k"] % kv_w == 0
    n_lat = dims["n_lat"] // tm
    n_seq = dims["seq"] // tm
    tab = lambda i: (jnp.where(i < n_lat, i % n_seq, n_seq), 0)
    return pl.pallas_call(
        functools.partial(_qk_kernel, n_q=attn_w // HEAD_DIM, n_kv=kv_w // HEAD_DIM,
                          scale=HEAD_DIM ** -0.5 * math.log2(math.e)),
        grid=(nt // tm,),
        in_specs=[
            pl.BlockSpec((tm, attn_w), lambda i: (i, dims["o_q"] // attn_w)),
            pl.BlockSpec((tm, kv_w), lambda i: (i, dims["o_k"] // kv_w)),
            pl.BlockSpec((tm, HEAD_DIM), tab),
            pl.BlockSpec((tm, HEAD_DIM), tab),
            pl.BlockSpec((None, 1, HEAD_DIM), lambda i: (l, 0, 0)),
            pl.BlockSpec((None, 1, HEAD_DIM), lambda i: (l, 0, 0)),
        ],
        out_specs=[
            pl.BlockSpec((tm, attn_w), lambda i: (i, 0)),
            pl.BlockSpec((tm, kv_w), lambda i: (i, 0)),
        ],
        out_shape=[
            jax.ShapeDtypeStruct((nt, attn_w), BF16),
            jax.ShapeDtypeStruct((nt, kv_w), BF16),
        ],
        compiler_params=_params(("parallel",)),
        name="qk_prepare",
    )(proj, proj, cos_t, sin_t, q_gain, k_gain)


def _rope_tables(seq, pad_rows):
    rows = seq // GRID_W
    row = jnp.repeat(jnp.arange(rows), GRID_W)
    col = jnp.tile(jnp.arange(GRID_W), rows)
    half = HEAD_DIM // 4
    inv_freq = ROPE_BASE ** (-jnp.arange(half, dtype=F32) / half)
    ang_r = row.astype(F32)[:, None] * inv_freq
    ang_c = col.astype(F32)[:, None] * inv_freq
    cr, sr, cc, sc = jnp.cos(ang_r), jnp.sin(ang_r), jnp.cos(ang_c), jnp.sin(ang_c)
    cos_t = jnp.concatenate([cr, cc, cr, cc], axis=-1)
    sin_t = jnp.concatenate([-sr, -sc, sr, sc], axis=-1)
    cos_t = jnp.concatenate([cos_t, jnp.ones((pad_rows, HEAD_DIM), F32)], axis=0)
    sin_t = jnp.concatenate([sin_t, jnp.zeros((pad_rows, HEAD_DIM), F32)], axis=0)
    return cos_t, sin_t


def _key_chunks(n_keys):
    unit = MXU_TILE if n_keys % MXU_TILE == 0 else LANES
    units = n_keys // unit
    n_chunks = -(-n_keys // MAX_KEY_CHUNK)
    sizes = [(units // n_chunks + (c < units % n_chunks)) * unit for c in range(n_chunks)]
    return [(sum(sizes[:c]), sizes[c]) for c in range(n_chunks)]


def _attn_kernel(*refs, tq, chunks, group, n_src):
    q_ref = refs[0]
    k_refs, v_refs = refs[1:1 + 2 * n_src:2], refs[2:2 + 2 * n_src:2]
    o_ref, ke_scr, ve_scr, s_scr, m_scr, acc_scr = refs[-6:]

    @pl.when(pl.program_id(2) == 0)
    def _():
        r0 = 0
        for k_ref, v_ref in zip(k_refs, v_refs):
            n = k_ref.shape[0]
            ke_scr[r0:r0 + n, :] = k_ref[...]
            ve_scr[r0:r0 + n, :HEAD_DIM] = v_ref[...]
            r0 += n
        ve_scr[:, HEAD_DIM:] = jnp.ones((ve_scr.shape[0], HEAD_DIM), BF16)

    q = jnp.concatenate([q_ref[:, HEAD_DIM * g:HEAD_DIM * (g + 1)] for g in range(group)], axis=0)

    one = jnp.minimum(pl.program_id(2) + 1, 1)

    def first(_, carry):
        m = None
        for k0, nk in chunks:
            s = lax.dot_general(q, ke_scr[k0:k0 + nk, :], (((1,), (1,)), ((), ())), preferred_element_type=F32)
            s_scr[:, k0:k0 + nk] = s
            for t in range(nk // LANES):
                slab = s[:, LANES * t:LANES * (t + 1)]
                m = slab if m is None else jnp.maximum(m, slab)
        m_scr[...] = m
        return carry

    lax.fori_loop(0, one, first, 0)
    m_scr[...] = jnp.broadcast_to(jnp.max(m_scr[...], axis=-1, keepdims=True), m_scr.shape)

    def second(_, carry):
        m_rep = m_scr[...]
        acc = None
        for k0, nk in chunks:
            p = jnp.concatenate(
                [jnp.exp2(s_scr[:, k0 + LANES * t:k0 + LANES * (t + 1)] - m_rep).astype(BF16)
                 for t in range(nk // LANES)], axis=1)
            part = jnp.dot(p, ve_scr[k0:k0 + nk, :], preferred_element_type=F32)
            acc = part if acc is None else acc + part
        acc_scr[...] = acc
        return carry

    lax.fori_loop(0, one, second, 0)
    acc = acc_scr[...]
    o = acc[:, :HEAD_DIM] / acc[:, HEAD_DIM:]
    for g in range(group):
        o_ref[:, HEAD_DIM * g:HEAD_DIM * (g + 1)] = o[tq * g:tq * (g + 1)].astype(o_ref.dtype)


def _attention(q, k, main, prev, dims):
    nt = q.shape[0]
    attn_w, seq, ctx, batch = dims["attn_w"], dims["seq"], dims["ctx"], dims["batch"]
    latent = prev is None
    group = attn_w // HEAD_DIM // N_KV_HEADS
    gw = group * HEAD_DIM
    v_blk = dims["o_v"] // HEAD_DIM
    lq = seq if latent else ctx
    tq = _pick([lq], [256, 128, 64, 32, 16])
    n_keys = ctx + seq if latent else ctx
    row0 = 0 if latent else dims["n_lat"] // tq
    ctx0 = dims["n_lat"] // ctx
    rows = group * tq
    in_specs = [
        pl.BlockSpec((tq, gw), lambda b, h, i: (row0 + b * (lq // tq) + i, h)),
        pl.BlockSpec((ctx, HEAD_DIM), lambda b, h, i: (ctx0 + b, h)),
        pl.BlockSpec((ctx, HEAD_DIM), lambda b, h, i: (ctx0 + b, v_blk + h)),
    ]
    args = [q, k, main]
    aliases = {}
    if latent:
        in_specs += [
            pl.BlockSpec((seq, HEAD_DIM), lambda b, h, i: (b, h), pipeline_mode=pl.Buffered(1)),
            pl.BlockSpec((seq, HEAD_DIM), lambda b, h, i: (b, v_blk + h), pipeline_mode=pl.Buffered(1)),
        ]
        args += [k, main]
    n_src = (len(args) - 1) // 2
    if not latent:
        in_specs.append(pl.BlockSpec(memory_space=pl.ANY))
        args.append(prev)
        aliases = {len(args) - 1: 0}
    scratch = [pltpu.VMEM((n_keys, HEAD_DIM), BF16), pltpu.VMEM((n_keys, 2 * HEAD_DIM), BF16),
               pltpu.VMEM((rows, n_keys), F32), pltpu.VMEM((rows, LANES), F32),
               pltpu.VMEM((rows, 2 * HEAD_DIM), F32)]
    return pl.pallas_call(
        functools.partial(_attn_kernel, tq=tq, chunks=_key_chunks(n_keys), group=group, n_src=n_src),
        grid=(batch, N_KV_HEADS, lq // tq),
        in_specs=in_specs,
        out_specs=pl.BlockSpec((tq, gw), lambda b, h, i: (row0 + b * (lq // tq) + i, h)),
        out_shape=jax.ShapeDtypeStruct((nt, attn_w), BF16),
        scratch_shapes=scratch,
        input_output_aliases=aliases,
        compiler_params=_params(("parallel", "parallel", "arbitrary")),
        name="attention_latent" if latent else "attention_context",
    )(*args)


def _ssm_params_kernel(lrr_ref, lir_ref, ldr_ref, lrc_ref, lic_ref, ldc_ref, btr_ref, bti_ref,
                       ctr_ref, cti_ref, r1_ref, v_ref, a_ref):
    t = SSM_CHUNK
    pw = SSM_GROUP * t
    r1_ref[...] = jnp.zeros(r1_ref.shape, r1_ref.dtype)
    v_ref[...] = jnp.zeros(v_ref.shape, v_ref.dtype)
    lane = lax.broadcasted_iota(jnp.int32, (SSM_GROUP, LANES), 1)
    col = lax.broadcasted_iota(jnp.int32, (SSM_GROUP, pw), 1)
    tau_col = (lax.broadcasted_iota(jnp.int32, (LANES, pw), 1) // SSM_GROUP).astype(F32)
    tau_row = lax.broadcasted_iota(jnp.int32, (t, LANES), 0).astype(F32)
    masks = (lane < SSM_STATE, lane >= SSM_STATE)
    toep = []
    for d in range(2):
        lr = jnp.minimum(lrr_ref[d], SSM_RE_MAX)
        li = lir_ref[d]
        dt = jnp.exp(ldr_ref[d])
        mag = jnp.exp(lr * dt)
        ab_re = mag * jnp.cos(li * dt)
        ab_im = mag * jnp.sin(li * dt)
        nr = ab_re - 1.0
        den = lr * lr + li * li
        f_re = (nr * lr + ab_im * li) / den
        f_im = (ab_im * lr - nr * li) / den
        bb_re = f_re * btr_ref[d] - f_im * bti_ref[d]
        bb_im = f_re * bti_ref[d] + f_im * btr_ref[d]
        e16 = jnp.exp(lr * dt * float(t))
        a_ref[2 * d:2 * d + 1, :] = e16 * jnp.cos(li * dt * float(t))
        a_ref[2 * d + 1:2 * d + 2, :] = e16 * jnp.sin(li * dt * float(t))
        tau_s = (t - 1.0) - tau_row if d == 0 else tau_row
        es = jnp.exp(lr * dt * tau_s)
        pw_re = es * jnp.cos(li * dt * tau_s)
        pw_im = es * jnp.sin(li * dt * tau_s)
        for s in range(t):
            ar = pw_re[s:s + 1]
            ai = pw_im[s:s + 1]
            w_re = ar * bb_re - ai * bb_im
            w_im = ar * bb_im + ai * bb_re
            for gi in range(2):
                r0 = gi * pw + SSM_GROUP * s
                c0 = 2 * pw + 2 * LANES * d
                r1_ref[r0:r0 + SSM_GROUP, c0:c0 + LANES] = jnp.where(masks[gi], w_re, 0.0).astype(r1_ref.dtype)
                r1_ref[r0:r0 + SSM_GROUP, c0 + LANES:c0 + 2 * LANES] = (
                    jnp.where(masks[gi], w_im, 0.0).astype(r1_ref.dtype))
        lrc = jnp.minimum(lrc_ref[d], SSM_RE_MAX)
        lic = lic_ref[d]
        dtc = jnp.exp(ldc_ref[d])
        tau_q = tau_col if d == 0 else (t - 1.0) - tau_col
        eq = jnp.exp(lrc * dtc * tau_q)
        p_re = eq * jnp.cos(lic * dtc * tau_q)
        p_im = eq * jnp.sin(lic * dtc * tau_q)
        c_re = ctr_ref[d]
        c_im = cti_ref[d]
        q_re = p_re * c_re - p_im * c_im
        q_im = p_re * c_im + p_im * c_re
        per_group = []
        for gi in range(2):
            k_mat = (jnp.dot(jnp.where(masks[gi], bb_re, 0.0), q_re, precision=lax.Precision.HIGHEST,
                             preferred_element_type=F32)
                     - jnp.dot(jnp.where(masks[gi], bb_im, 0.0), q_im, precision=lax.Precision.HIGHEST,
                               preferred_element_type=F32))
            per_group.append(k_mat)
        toep.append(per_group)
        magc = jnp.exp(lrc * dtc)
        abc_re = magc * jnp.cos(lic * dtc)
        abc_im = magc * jnp.sin(lic * dtc)
        q1_re = abc_re * q_re - abc_im * q_im
        q1_im = abc_re * q_im + abc_im * q_re
        for gi in range(2):
            rows = slice(SSM_STATE * gi, SSM_STATE * (gi + 1))
            r0 = 2 * LANES * d + SSM_STATE * gi
            v_ref[r0:r0 + SSM_STATE, pw * gi:pw * (gi + 1)] = q1_re[rows].astype(v_ref.dtype)
            v_ref[r0 + LANES:r0 + LANES + SSM_STATE, pw * gi:pw * (gi + 1)] = (-q1_im[rows]).astype(v_ref.dtype)
    for gi in range(2):
        k_f, k_b = toep[0][gi], toep[1][gi]
        for s in range(t):
            sh = SSM_GROUP * s
            f = k_f if sh == 0 else jnp.where(col >= sh, pltpu.roll(k_f, sh, 1), 0.0)
            back = SSM_GROUP * (t - 1 - s)
            b = k_b if back == 0 else jnp.where(col < pw - back, pltpu.roll(k_b, pw - back, 1), 0.0)
            r0 = gi * pw + sh
            r1_ref[r0:r0 + SSM_GROUP, pw * gi:pw * (gi + 1)] = (f + b).astype(r1_ref.dtype)


def _pair_rows(a):
    depth, _, groups, n = a.shape
    return a.reshape(depth, 2, groups // 2, 2 * n).transpose(0, 2, 1, 3)


def _ssm_parameters(lam_re, lam_im, log_dt, b_re, b_im, c_re, c_im):
    depth, _, groups, n = lam_re.shape
    pairs = groups // 2
    t = SSM_CHUNK
    pw = SSM_GROUP * t
    ld = jnp.broadcast_to(log_dt[..., None], lam_re.shape)
    rows = [_pair_rows(a.astype(F32))[:, :, :, None, :] for a in (lam_re, lam_im, ld)]
    cols = [_pair_rows(a.astype(F32))[:, :, :, :, None] for a in (lam_re, lam_im, ld)]

    def bt(b):
        return (b.astype(F32).reshape(depth, 2, pairs, 2, n, SSM_GROUP).transpose(0, 2, 1, 5, 3, 4)
                .reshape(depth, pairs, 2, SSM_GROUP, 2 * n))

    def ct(c):
        c = (c.astype(F32).reshape(depth, 2, pairs, 2, SSM_GROUP, n).transpose(0, 2, 1, 3, 5, 4)
             .reshape(depth, pairs, 2, 2 * n, SSM_GROUP))
        return jnp.tile(c, (1, 1, 1, 1, t))

    row_spec = pl.BlockSpec((None, None, 2, 1, 2 * n), lambda l, j: (l, j, 0, 0, 0))
    col_spec = pl.BlockSpec((None, None, 2, 2 * n, 1), lambda l, j: (l, j, 0, 0, 0))
    bt_spec = pl.BlockSpec((None, None, 2, SSM_GROUP, 2 * n), lambda l, j: (l, j, 0, 0, 0))
    ct_spec = pl.BlockSpec((None, None, 2, 2 * n, pw), lambda l, j: (l, j, 0, 0, 0))
    return pl.pallas_call(
        _ssm_params_kernel,
        grid=(depth, pairs),
        in_specs=[row_spec] * 3 + [col_spec] * 3 + [bt_spec] * 2 + [ct_spec] * 2,
        out_specs=[
            pl.BlockSpec((None, None, 2 * pw, 4 * pw), lambda l, j: (l, j, 0, 0)),
            pl.BlockSpec((None, None, 8 * n, 2 * pw), lambda l, j: (l, j, 0, 0)),
            pl.BlockSpec((None, 4, 2 * n), lambda l, j: (l, 0, j)),
        ],
        out_shape=[
            jax.ShapeDtypeStruct((depth, pairs, 2 * pw, 4 * pw), BF16),
            jax.ShapeDtypeStruct((depth, pairs, 8 * n, 2 * pw), BF16),
            jax.ShapeDtypeStruct((depth, 4, groups * n), F32),
        ],
        compiler_params=_params(("parallel", "parallel")),
        name="ssm_parameters",
    )(*rows, *cols, bt(b_re), bt(b_im), ct(c_re), ct(c_im))


def _chunk_rows(u_ref):
    return jnp.concatenate([u_ref[t] for t in range(u_ref.shape[0])], axis=-1)


def _scan_chunk_states(s_ref, a_ref, h_ref, *, batch, lat_tiles, ctx_tiles):
    w = s_ref.shape[-1]
    per_tile = SUBLANES // batch
    row_grp = lax.broadcasted_iota(jnp.int32, (SUBLANES, w), 0) // batch
    zero = jnp.zeros((SUBLANES, w), F32)

    def make_tile(k, order, shift):
        ar = a_ref[k:k + 1, :]
        ai = a_ref[k + 1:k + 2, :]

        def tile(ti, carry):
            cur_re, cur_im = carry
            rows = pl.ds(pl.multiple_of(ti * SUBLANES, SUBLANES), SUBLANES)
            s_re = s_ref[k, rows, :]
            s_im = s_ref[k + 1, rows, :]
            h_re, h_im = zero, zero
            for pos in order:
                sel = row_grp == pos
                h_re = jnp.where(sel, cur_re, h_re)
                h_im = jnp.where(sel, cur_im, h_im)
                nxt_re = ar * cur_re - ai * cur_im + s_re
                nxt_im = ar * cur_im + ai * cur_re + s_im
                if batch == SUBLANES:
                    cur_re, cur_im = nxt_re, nxt_im
                else:
                    cur_re = pltpu.roll(nxt_re, shift, 0)
                    cur_im = pltpu.roll(nxt_im, shift, 0)
            h_ref[k, rows, :] = h_re
            h_ref[k + 1, rows, :] = h_im
            return cur_re, cur_im

        return tile

    fwd = make_tile(0, range(per_tile), batch)
    bwd = make_tile(2, range(per_tile - 1, -1, -1), SUBLANES - batch)

    def both(first_tile, n_tiles):
        def step(i, carry):
            return fwd(first_tile + i, carry[0]), bwd(first_tile + n_tiles - 1 - i, carry[1])
        return step

    carry = lax.fori_loop(0, ctx_tiles, both(lat_tiles, ctx_tiles), ((zero, zero), (zero, zero)))
    lax.fori_loop(0, lat_tiles, both(0, lat_tiles), carry)


def _ssm_core_kernel(u_ref, r1_ref, v_ref, a_ref, d_ref, y_ref, s_scr, h_scr, *, batch, lat_tiles, ctx_tiles):
    n_t = u_ref.shape[0]
    kw = n_t * LANES
    u = _chunk_rows(u_ref)
    r = jnp.dot(u, r1_ref[...], preferred_element_type=F32)
    for t in range(n_t):
        y_ref[t] = r[:, LANES * t:LANES * (t + 1)]
    for k in range(4):
        s_scr[k] = r[:, kw + LANES * k:kw + LANES * (k + 1)]
    _scan_chunk_states(s_scr, a_ref, h_scr, batch=batch, lat_tiles=lat_tiles, ctx_tiles=ctx_tiles)
    h = jnp.concatenate([h_scr[k] for k in range(4)], axis=-1).astype(BF16)
    y = jnp.dot(h, v_ref[...], preferred_element_type=F32) + d_ref[...] * u.astype(F32)
    for t in range(n_t):
        y_ref[t] += y[:, LANES * t:LANES * (t + 1)]


def _ssm_core(u, r1, v, a16, dsk, l, dims):
    pairs, n_t, nc, _ = u.shape
    kw = n_t * LANES
    batch = dims["batch"]
    lat_tiles = dims["seq"] // SSM_CHUNK * batch // SUBLANES
    ctx_tiles = dims["ctx"] // SSM_CHUNK * batch // SUBLANES
    return pl.pallas_call(
        functools.partial(_ssm_core_kernel, batch=batch, lat_tiles=lat_tiles, ctx_tiles=ctx_tiles),
        grid=(pairs,),
        in_specs=[
            pl.BlockSpec((None, n_t, nc, LANES), lambda j: (j, 0, 0, 0)),
            pl.BlockSpec((None, None, kw, 2 * kw), lambda j: (l, j, 0, 0)),
            pl.BlockSpec((None, None, 4 * LANES, kw), lambda j: (l, j, 0, 0)),
            pl.BlockSpec((None, 4, LANES), lambda j: (l, 0, j)),
            pl.BlockSpec((None, None, 1, kw), lambda j: (l, j, 0, 0)),
        ],
        out_specs=pl.BlockSpec((None, n_t, nc, LANES), lambda j: (j, 0, 0, 0)),
        out_shape=jax.ShapeDtypeStruct((pairs, n_t, nc, LANES), F32),
        scratch_shapes=[pltpu.VMEM((4, nc, LANES), F32), pltpu.VMEM((4, nc, LANES), F32)],
        compiler_params=_params(("parallel",)),
        name="ssm_core",
    )(u, r1, v, a16, dsk)


GROUPS_PER_TILE = LANES // SSM_GROUP


def _lane_slot(rows):
    return lax.broadcasted_iota(jnp.int32, (rows, LANES), 1) // SSM_GROUP


def _block_transposes(sets, slot):
    sets = [list(t) for t in sets]
    d = GROUPS_PER_TILE // 2
    while d:
        low = (slot & d) == 0
        for t in sets:
            for a in range(GROUPS_PER_TILE):
                if not a & d:
                    lo, hi = t[a], t[a + d]
                    t[a] = jnp.where(low, lo, pltpu.roll(hi, d * SSM_GROUP, 1))
                    t[a + d] = jnp.where(low, pltpu.roll(lo, LANES - d * SSM_GROUP, 1), hi)
        d //= 2
    return sets


def _chunk_tiling(dims):
    ncl, ncc = dims["seq"] // SSM_CHUNK, dims["ctx"] // SSM_CHUNK
    ct = _pick([ncl, ncc], [16, 8, 4, 2, 1])
    nlt, nct = ncl // ct, ncc // ct

    def token_block(k, b):
        return jnp.where(k < nlt, b * nlt + k, dims["batch"] * nlt + b * nct + (k - nlt))

    return ct, nlt + nct, token_block


def _ssm_pack_kernel(*refs, batch, ct):
    x_refs, o_ref, scr = refs[:batch], refs[batch], refs[batch + 1]
    slot = _lane_slot(ct)
    halves = SSM_CHUNK // GROUPS_PER_TILE
    for b in range(batch):
        where = [(j, half) for j in range(x_refs[b].shape[0]) for half in range(halves)]
        sets = [[x_refs[b][j, pl.ds(GROUPS_PER_TILE * half + k, ct, stride=SSM_CHUNK), :]
                 for k in range(GROUPS_PER_TILE)] for j, half in where]
        for (j, half), outs in zip(where, _block_transposes(sets, slot)):
            for gg, out in enumerate(outs):
                g = GROUPS_PER_TILE * j + gg
                scr[g // 2, (g % 2) * halves + half, pl.ds(b, ct, stride=batch), :] = out
    o_ref[...] = scr[...].astype(o_ref.dtype)


def _ssm_pack(u32, dims):
    batch, pairs = dims["batch"], dims["pairs"]
    ct, n_tiles, token_block = _chunk_tiling(dims)
    n_in, nt, _ = u32.shape
    n_t = 2 * SSM_CHUNK * SSM_GROUP // LANES
    return pl.pallas_call(
        functools.partial(_ssm_pack_kernel, batch=batch, ct=ct),
        grid=(n_tiles,),
        in_specs=[pl.BlockSpec((n_in, ct * SSM_CHUNK, LANES),
                               functools.partial(lambda k, b: (0, token_block(k, b), 0), b=b))
                  for b in range(batch)],
        out_specs=pl.BlockSpec((pairs, n_t, ct * batch, LANES), lambda k: (0, 0, k, 0)),
        out_shape=jax.ShapeDtypeStruct((pairs, n_t, nt // SSM_CHUNK, LANES), BF16),
        scratch_shapes=[pltpu.VMEM((pairs, n_t, ct * batch, LANES), F32)],
        compiler_params=_params(("parallel",)),
        name="ssm_pack",
    )(*([u32] * batch))


def _ssm_unpack_kernel(y_ref, o_ref, *, batch, ct):
    b = pl.program_id(1)
    slot = _lane_slot(ct)
    halves = SSM_CHUNK // GROUPS_PER_TILE
    where = [(j, half) for j in range(o_ref.shape[0]) for half in range(halves)]
    sets = []
    for j, half in where:
        groups = [GROUPS_PER_TILE * j + gg for gg in range(GROUPS_PER_TILE)]
        sets.append([y_ref[g // 2, (g % 2) * halves + half, pl.ds(b, ct, stride=batch), :] for g in groups])
    for (j, half), outs in zip(where, _block_transposes(sets, slot)):
        for k, out in enumerate(outs):
            o_ref[j, pl.ds(GROUPS_PER_TILE * half + k, ct, stride=SSM_CHUNK), :] = out


def _ssm_unpack(y, dims):
    batch = dims["batch"]
    pairs, n_t, nc, _ = y.shape
    ct, n_tiles, token_block = _chunk_tiling(dims)
    n_out = dims["ssm_w"] // LANES
    return pl.pallas_call(
        functools.partial(_ssm_unpack_kernel, batch=batch, ct=ct),
        grid=(n_tiles, batch),
        in_specs=[pl.BlockSpec((pairs, n_t, ct * batch, LANES), lambda k, b: (0, 0, k, 0))],
        out_specs=pl.BlockSpec((n_out, ct * SSM_CHUNK, LANES), lambda k, b: (0, token_block(k, b), 0)),
        out_shape=jax.ShapeDtypeStruct((n_out, nc * SSM_CHUNK, LANES), F32),
        compiler_params=_params(("parallel", "arbitrary")),
        name="ssm_unpack",
    )(y)


def _merge_kernel(pb_ref, pc_ref, pv_ref, hpc_ref, hpv_ref, hnc_ref, hnv_ref, cw_ref, ys_ref, at_ref,
                  g_ref, wc_ref, wg_ref, wa_ref, wo_ref, x_ref, gate_ref, gpost_ref, gn_ref, shn_ref, scn_ref,
                  o_ref, h_ref, m_scr, *, tm, tn, seq, ctx, n_lat):
    d = o_ref.shape[1]
    r0 = pl.program_id(0) * tm
    period = jnp.where(r0 >= n_lat, ctx, seq)
    starts = r0 % period == 0
    ends = (r0 + tm) % period == 0
    w = pc_ref[...].astype(F32) * pv_ref[...].astype(F32)
    prev = (hpc_ref[...].astype(F32) * hpv_ref[...].astype(F32))[HALO_ROWS - 1:HALO_ROWS]
    nxt = (hnc_ref[...].astype(F32) * hnv_ref[...].astype(F32))[0:1]
    prev = jnp.where(starts, 0.0, prev)
    nxt = jnp.where(ends, 0.0, nxt)
    row = lax.broadcasted_iota(jnp.int32, w.shape, 0)
    up = jnp.where(row == 0, prev, pltpu.roll(w, 1, 0))
    dn = jnp.where(row == tm - 1, nxt, pltpu.roll(w, tm - 1, 0))
    cw = cw_ref[...]
    cf = (pb_ref[...].astype(F32) * (cw[0:1] * up + cw[1:2] * w + cw[2:3] * dn)).astype(BF16)
    yg = jax.nn.gelu(_chunk_rows(ys_ref)).astype(BF16)
    at = at_ref[...]
    for c in range(d // tn):
        cols = slice(tn * c, tn * (c + 1))
        gate = lambda k: jax.nn.sigmoid(g_ref[:, k * d + tn * c:k * d + tn * (c + 1)].astype(F32))
        ya = jnp.dot(cf, wc_ref[:, cols], preferred_element_type=F32)
        yc = jnp.dot(at, wa_ref[:, cols], preferred_element_type=F32)
        ys = jnp.dot(yg, wg_ref[:, cols], preferred_element_type=F32) * jax.nn.sigmoid(
            jnp.dot(yg, wg_ref[:, d + tn * c:d + tn * (c + 1)], preferred_element_type=F32))
        m_scr[:, cols] = (gate(0) * ya + gate(1) * ys + gate(2) * yc).astype(m_scr.dtype)
    r = jnp.dot(m_scr[...], wo_ref[...], preferred_element_type=F32)
    x = x_ref[...] + gate_ref[...] * _rms(r, gpost_ref[...])
    o_ref[...] = x
    h_ref[...] = _norm_mod(x, gn_ref[...], shn_ref[...], scn_ref[...])


def _merge(main, gates, ys, attn, conv_w, w_conv_out, w_glu, w_attn_out, w_out, xt, modr, g_post, g_pre_mlp,
           l, dims, rows):
    d = dims["d"]
    cw_, sw, aw = dims["conv_w"], dims["ssm_w"], dims["attn_w"]
    tm = _pick([dims["seq"], dims["ctx"]], [512, 256, 128, 64, 32])
    tn = _pick([d], [512, 256, 128])
    n_halo = main.shape[0] // HALO_ROWS
    per = tm // HALO_ROWS
    prev_blk = lambda i: jnp.maximum(i * per - 1, 0)
    next_blk = lambda i: jnp.minimum((i + 1) * per, n_halo - 1)
    grp = _row_group(dims, tm)
    mod = lambda k: pl.BlockSpec((None, None, 1, d), lambda i: (l, grp(i), 0, k))
    vec = pl.BlockSpec((None, 1, d), lambda i: (l, 0, 0))
    row_tile = pl.BlockSpec((tm, d), lambda i: (i, 0))
    resident = lambda shape: pl.BlockSpec((None,) + shape, lambda i: (l, 0, 0), pipeline_mode=pl.Buffered(1))
    return pl.pallas_call(
        functools.partial(_merge_kernel, tm=tm, tn=tn, seq=dims["seq"], ctx=dims["ctx"], n_lat=dims["n_lat"]),
        grid=(rows // tm,),
        in_specs=[
            pl.BlockSpec((tm, cw_), lambda i: (i, 0)),
            pl.BlockSpec((tm, cw_), lambda i: (i, 1)),
            pl.BlockSpec((tm, cw_), lambda i: (i, 2)),
            pl.BlockSpec((HALO_ROWS, cw_), lambda i: (prev_blk(i), 1)),
            pl.BlockSpec((HALO_ROWS, cw_), lambda i: (prev_blk(i), 2)),
            pl.BlockSpec((HALO_ROWS, cw_), lambda i: (next_blk(i), 1)),
            pl.BlockSpec((HALO_ROWS, cw_), lambda i: (next_blk(i), 2)),
            resident((CONV_K, cw_)),
            pl.BlockSpec((sw // LANES, tm, LANES), lambda i: (0, i, 0)),
            pl.BlockSpec((tm, aw), lambda i: (i, 0)),
            pl.BlockSpec((tm, 3 * d), lambda i: (i, 0)),
            resident((cw_, d)),
            resident((sw, 2 * d)),
            resident((aw, d)),
            resident((d, d)),
            row_tile, mod(2), vec, vec, mod(3), mod(4),
        ],
        out_specs=[row_tile, row_tile],
        out_shape=[jax.ShapeDtypeStruct(xt.shape, F32), jax.ShapeDtypeStruct((rows, d), BF16)],
        scratch_shapes=[pltpu.VMEM((tm, d), BF16)],
        input_output_aliases={15: 0},
        compiler_params=_params(("parallel",)),
        name="mixer_output",
    )(main, main, main, main, main, main, main, conv_w, ys, attn, gates, w_conv_out, w_glu, w_attn_out,
      w_out, xt, modr, g_post, g_pre_mlp, modr, modr)


def _mlp_kernel(x_ref, h_ref, wu_ref, wd_ref, gate_ref, gpost_ref, *rest, with_next):
    j = pl.program_id(1)
    if with_next:
        gn_ref, shn_ref, scn_ref, o_ref, hn_ref, acc_scr = rest
    else:
        o_ref, acc_scr = rest

    last = pl.num_programs(1) - 1

    def partial_out(rows):
        u = jnp.maximum(jnp.dot(h_ref[rows, :], wu_ref[...], preferred_element_type=F32), 0.0)
        return jnp.dot((u * u).astype(BF16), wd_ref[...], preferred_element_type=F32)

    every = slice(None)

    @pl.when(j == 0)
    def _():
        acc_scr[...] = partial_out(every)

    @pl.when(jnp.logical_and(j > 0, j < last))
    def _():
        acc_scr[...] += partial_out(every)

    @pl.when(j == last)
    def _():
        part = acc_scr.shape[0] // MLP_TAIL_PARTS
        for k in range(MLP_TAIL_PARTS):
            rows = slice(part * k, part * (k + 1))
            r = acc_scr[rows, :] + partial_out(rows)
            x = x_ref[rows, :] + gate_ref[...] * _rms(r, gpost_ref[...])
            o_ref[rows, :] = x
            if with_next:
                hn_ref[rows, :] = _norm_mod(x, gn_ref[...], shn_ref[...], scn_ref[...])


def _mlp(xt, h, g_post, modr, w_up, w_down, l, dims, rows, g_next=None):
    d = xt.shape[1]
    d_ff = w_up.shape[-1]
    tm = _pick([dims["seq"], dims["n_ctx"]], [512, 256, 128])
    tf = _pick([d_ff], [1024, 512, 256, 128])
    grp = _row_group(dims, tm)
    mod = lambda layer, k: pl.BlockSpec((None, None, 1, d), lambda i, j: (layer, grp(i), 0, k))
    vec = lambda layer: pl.BlockSpec((None, 1, d), lambda i, j: (layer, 0, 0))
    row_tile = pl.BlockSpec((tm, d), lambda i, j: (i, 0))
    in_specs = [
        row_tile, row_tile,
        pl.BlockSpec((None, d, tf), lambda i, j: (l, 0, j)),
        pl.BlockSpec((None, tf, d), lambda i, j: (l, j, 0)),
        mod(l, 5), vec(l),
    ]
    args = [xt, h, w_up, w_down, modr, g_post]
    out_specs, out_shape = [row_tile], [jax.ShapeDtypeStruct((rows, d), F32)]
    if g_next is not None:
        assert rows == xt.shape[0]
        in_specs += [vec(l + 1), mod(l + 1, 0), mod(l + 1, 1)]
        args += [g_next, modr, modr]
        out_specs.append(row_tile)
        out_shape.append(jax.ShapeDtypeStruct((rows, d), BF16))
    return pl.pallas_call(
        functools.partial(_mlp_kernel, with_next=g_next is not None),
        grid=(rows // tm, d_ff // tf),
        in_specs=in_specs,
        out_specs=out_specs,
        out_shape=out_shape,
        scratch_shapes=[pltpu.VMEM((tm, d), F32)],
        input_output_aliases={0: 0} if g_next is not None else {},
        compiler_params=_params(("parallel", "arbitrary")),
        name="mlp",
    )(*args)


def kernel(x, c, ctx, c_ctx, w_mod, b_mod, g_pre_mix, g_post_mix, g_pre_mlp, g_post_mlp, w_in, conv_w,
           w_conv_out, ssm_lam_re, ssm_lam_im, ssm_log_dt, ssm_b_re, ssm_b_im, ssm_c_re, ssm_c_im, ssm_d,
           w_glu, q_gain, k_gain, w_attn_out, w_out, w_up, w_down):
    batch, seq, d = x.shape
    ctx_len = ctx.shape[1]
    depth = w_in.shape[0]
    in_w = w_in.shape[-1]
    conv_width = conv_w.shape[-1]
    ssm_w = ssm_d.shape[-1]
    attn_w = w_attn_out.shape[1]
    gate_w = 3 * d
    kv_w = (in_w - 3 * conv_width - ssm_w - attn_w - gate_w) // 2
    groups = ssm_lam_re.shape[2]
    assert batch + 1 <= MOD_ROWS and SUBLANES % batch == 0
    assert kv_w == N_KV_HEADS * HEAD_DIM and ssm_lam_re.shape[3] == SSM_STATE
    assert ssm_w == groups * SSM_GROUP and groups % 2 == 0
    assert seq % GRID_W == 0 and (batch * seq) % ctx_len == 0 and ctx_len % LANES == 0
    a_w = 3 * conv_width + ssm_w
    dims = dict(
        batch=batch, seq=seq, ctx=ctx_len, d=d, n_lat=batch * seq, n_ctx=batch * ctx_len,
        conv_w=conv_width, ssm_w=ssm_w, attn_w=attn_w, kv_w=kv_w, gate_w=gate_w, pairs=groups // 2,
        o_q=gate_w, o_k=gate_w + attn_w, o_v=gate_w + attn_w + kv_w,
    )
    n_lat, nt = dims["n_lat"], dims["n_lat"] + dims["n_ctx"]

    cc = jnp.concatenate([c, c_ctx[None, :], jnp.zeros((MOD_ROWS - batch - 1, d), F32)], axis=0)
    mod = _modulation(cc, w_mod, b_mod)
    modr = mod.reshape(depth, MOD_ROWS, 1, N_MOD * d)

    row_vec = lambda a: a.reshape(depth, 1, a.shape[-1])
    g_pre_mix, g_post_mix, g_pre_mlp, g_post_mlp = map(row_vec, (g_pre_mix, g_post_mix, g_pre_mlp, g_post_mlp))
    perm = _head_permutation()
    q_gain, k_gain = row_vec(q_gain[:, perm]), row_vec(k_gain[:, perm])
    w_in_a = w_in[:, :, :a_w].astype(BF16)
    w_qk = w_in[:, :, a_w:a_w + attn_w + kv_w].reshape(depth, d, -1, HEAD_DIM)[..., perm]
    w_in_b = jnp.concatenate([w_in[:, :, in_w - gate_w:], w_qk.reshape(depth, d, attn_w + kv_w),
                              w_in[:, :, a_w + attn_w + kv_w:in_w - gate_w]], axis=-1).astype(BF16)
    w_conv_out, w_glu, w_attn_out, w_out, w_up, w_down = (
        w.astype(BF16) for w in (w_conv_out, w_glu, w_attn_out, w_out, w_up, w_down))
    tn_a = _pick([a_w], [1280, 768, 512, 256, 128])
    tn_b = _pick([w_in_b.shape[-1]], [1536, 768, 512, 256, 128])

    cos_t, sin_t = _rope_tables(seq, _qk_tile(dims))
    r1, v_mat, a16 = _ssm_parameters(ssm_lam_re, ssm_lam_im, ssm_log_dt, ssm_b_re, ssm_b_im, ssm_c_re, ssm_c_im)
    dsk = jnp.broadcast_to(
        ssm_d.astype(F32).reshape(depth, groups // 2, 2, 1, SSM_GROUP),
        (depth, groups // 2, 2, SSM_CHUNK, SSM_GROUP)).reshape(depth, groups // 2, 1, 2 * SSM_CHUNK * SSM_GROUP)

    xt, h = _prenorm(x.reshape(n_lat, d), ctx.reshape(dims["n_ctx"], d), g_pre_mix, modr, dims)
    for l in range(depth):
        need_ctx = l < depth - 1
        rows = nt if need_ctx else n_lat
        proj_a, u32 = _projection(h, w_in_a, l, dims, tn_a, f32_cols=(3 * conv_width, ssm_w))
        (proj_b,) = _projection(h, w_in_b, l, dims, tn_b)
        u = _ssm_pack(u32, dims)
        ys = _ssm_unpack(_ssm_core(u, r1, v_mat, a16, dsk, l, dims), dims)
        q, k = _qk_prepare(proj_b, cos_t, sin_t, q_gain, k_gain, l, dims)
        attn = _attention(q, k, proj_b, None, dims)
        if need_ctx:
            attn = _attention(q, k, proj_b, attn, dims)
        xt, h = _merge(proj_a, proj_b, ys, attn, conv_w, w_conv_out, w_glu, w_attn_out, w_out, xt, modr,
                       g_post_mix, g_pre_mlp, l, dims, rows)
        if need_ctx:
            xt, h = _mlp(xt, h, g_post_mlp, modr, w_up, w_down, l, dims, rows, g_next=g_pre_mix)
        else:
            (xt,) = _mlp(xt, h, g_post_mlp, modr, w_up, w_down, l, dims, rows)
    return xt.reshape(batch, seq, d)
```

```python
import functools
import math

import jax
import jax.numpy as jnp
from jax import lax
from jax.experimental import pallas as pl
from jax.experimental.pallas import tpu as pltpu

F32 = jnp.float32
BF16 = jnp.bfloat16

NORM_EPS = 1e-6
N_MOD = 6
HEAD_DIM = 128
N_KV_HEADS = 2
GRID_W = 64
ROPE_BASE = 10000.0
CONV_K = 3
SSM_GROUP = 16
SSM_STATE = 64
SSM_RE_MAX = -1e-4
SSM_CHUNK = 16
SUBLANES = 8
LANES = 128
HALO_ROWS = 16
MXU_TILE = 256
MAX_KEY_CHUNK = 4352
MLP_TAIL_PARTS = 2
MOD_ROWS = 8
VMEM_LIMIT = 56 * 2**20


def _params(sem, vmem=VMEM_LIMIT):
    return pltpu.CompilerParams(dimension_semantics=sem, vmem_limit_bytes=vmem)


def _pick(dims, cands):
    for c in cands:
        if all(d % c == 0 for d in dims):
            return c
    raise ValueError(f"no tile in {cands} divides {dims}")


def _rms(x, g):
    ms = jnp.mean(x * x, axis=-1, keepdims=True)
    return x * lax.rsqrt(ms + NORM_EPS) * g


def _mod_kernel(c_ref, w_ref, b_ref, o_ref):
    c = c_ref[...]
    s = (c * jax.nn.sigmoid(c)).astype(BF16)
    o_ref[...] = jnp.dot(s, w_ref[...].astype(BF16), preferred_element_type=F32) + b_ref[...]


def _modulation(cc, w_mod, b_mod):
    depth, d, nout = w_mod.shape
    tn = _pick([nout], [1024, 512, 256, 128])
    return pl.pallas_call(
        _mod_kernel,
        grid=(depth, nout // tn),
        in_specs=[
            pl.BlockSpec((MOD_ROWS, d), lambda l, j: (0, 0)),
            pl.BlockSpec((None, d, tn), lambda l, j: (l, 0, j)),
            pl.BlockSpec((None, 1, tn), lambda l, j: (l, 0, j)),
        ],
        out_specs=pl.BlockSpec((None, MOD_ROWS, tn), lambda l, j: (l, 0, j)),
        out_shape=jax.ShapeDtypeStruct((depth, MOD_ROWS, nout), F32),
        compiler_params=_params(("parallel", "parallel")),
        name="modulation",
    )(cc, w_mod, b_mod.reshape(depth, 1, nout))


def _norm_mod(x, g, shift, scale):
    return (_rms(x, g) * (1.0 + scale) + shift).astype(BF16)


def _prenorm_kernel(x_ref, c_ref, g_ref, sh_ref, sc_ref, xt_ref, h_ref, *, lat_tiles):
    x = jnp.where(pl.program_id(0) < lat_tiles, x_ref[...], c_ref[...])
    xt_ref[...] = x
    h_ref[...] = _norm_mod(x, g_ref[...], sh_ref[...], sc_ref[...])


def _row_group(dims, tm):
    return lambda i: jnp.minimum((i * tm) // dims["seq"], dims["batch"])


def _prenorm(x, ctx, gain, modr, dims):
    d = x.shape[1]
    nt = x.shape[0] + ctx.shape[0]
    tm = _pick([dims["seq"], dims["n_ctx"]], [512, 256, 128])
    lat_tiles = x.shape[0] // tm
    grp = _row_group(dims, tm)
    row_tile = pl.BlockSpec((tm, d), lambda i: (i, 0))
    return pl.pallas_call(
        functools.partial(_prenorm_kernel, lat_tiles=lat_tiles),
        grid=(nt // tm,),
        in_specs=[
            pl.BlockSpec((tm, d), lambda i: (jnp.minimum(i, lat_tiles - 1), 0)),
            pl.BlockSpec((tm, d), lambda i: (jnp.maximum(i - lat_tiles, 0), 0)),
            pl.BlockSpec((None, 1, d), lambda i: (0, 0, 0)),
            pl.BlockSpec((None, None, 1, d), lambda i: (0, grp(i), 0, 0)),
            pl.BlockSpec((None, None, 1, d), lambda i: (0, grp(i), 0, 1)),
        ],
        out_specs=[row_tile, row_tile],
        out_shape=[jax.ShapeDtypeStruct((nt, d), F32), jax.ShapeDtypeStruct((nt, d), BF16)],
        compiler_params=_params(("parallel",)),
        name="prenorm",
    )(x, ctx, gain, modr, modr)


def _proj_kernel(h_ref, w_ref, o_ref, *u_refs, u_tile, u_off):
    r = jnp.dot(h_ref[...], w_ref[...], preferred_element_type=F32)
    o_ref[...] = r.astype(o_ref.dtype)
    if u_refs:
        (ou_ref,) = u_refs

        @pl.when(pl.program_id(1) == u_tile)
        def _():
            for t in range(ou_ref.shape[0]):
                ou_ref[t] = r[:, u_off + LANES * t:u_off + LANES * (t + 1)]


def _col_tiles(w, tn):
    depth, k, n = w.shape
    return w.reshape(depth, k, n // tn, tn).transpose(0, 2, 1, 3)


def _projection(h, w, l, dims, f32_cols=None):
    nt, d = h.shape
    tn = w.shape[-1]
    n_out = w.shape[1] * tn
    tm = _pick([dims["seq"], dims["n_ctx"]], [1024, 512, 256, 128])
    out_specs = [pl.BlockSpec((tm, tn), lambda i, j: (i, j))]
    out_shape = [jax.ShapeDtypeStruct((nt, n_out), BF16)]
    u_tile = u_off = 0
    if f32_cols is not None:
        start, width = f32_cols
        u_tile, u_off = start // tn, start % tn
        assert u_off + width <= tn and u_off % LANES == 0 and width % LANES == 0
        out_specs.append(pl.BlockSpec((width // LANES, tm, LANES), lambda i, j: (0, i, 0)))
        out_shape.append(jax.ShapeDtypeStruct((width // LANES, nt, LANES), F32))
    return pl.pallas_call(
        functools.partial(_proj_kernel, u_tile=u_tile, u_off=u_off),
        grid=(nt // tm, n_out // tn),
        in_specs=[
            pl.BlockSpec((tm, d), lambda i, j: (i, 0)),
            pl.BlockSpec((None, None, d, tn), lambda i, j: (l, j, 0, 0)),
        ],
        out_specs=out_specs,
        out_shape=out_shape,
        compiler_params=_params(("parallel", "arbitrary")),
        name="in_projection",
    )(h, w)


def _head_permutation():
    quarter = HEAD_DIM // 4
    return jnp.concatenate([jnp.arange(quarter * k, quarter * (k + 1)) for k in (0, 2, 1, 3)])


def _qk_kernel(xq_ref, xk_ref, cos_ref, sin_ref, qg_ref, kg_ref, q_ref, k_ref, *, n_q, n_kv, scale):
    cos = cos_ref[...]
    sin = sin_ref[...]
    mean_cols = jnp.full((HEAD_DIM, HEAD_DIM), 1.0 / HEAD_DIM, BF16)
    for h in range(n_q + n_kv):
        if h < n_q:
            xh = xq_ref[:, HEAD_DIM * h:HEAD_DIM * (h + 1)].astype(F32)
        else:
            xh = xk_ref[:, HEAD_DIM * (h - n_q):HEAD_DIM * (h - n_q + 1)].astype(F32)
        gain = qg_ref[...] * scale if h < n_q else kg_ref[...]
        sq = xh * xh
        hi = sq.astype(BF16)
        lo = (sq - hi.astype(F32)).astype(BF16)
        ms = (jnp.dot(hi, mean_cols, preferred_element_type=F32)
              + jnp.dot(lo, mean_cols, preferred_element_type=F32))
        y = xh * lax.rsqrt(ms + NORM_EPS) * gain
        r = (y * cos + pltpu.roll(y, HEAD_DIM // 2, 1) * sin).astype(BF16)
        if h < n_q:
            q_ref[:, HEAD_DIM * h:HEAD_DIM * (h + 1)] = r
        else:
            k_ref[:, HEAD_DIM * (h - n_q):HEAD_DIM * (h - n_q + 1)] = r


def _qk_tile(dims):
    return _pick([dims["seq"], dims["n_ctx"]], [512, 256, 128, 64, 32])


def _qk_prepare(proj, cos_t, sin_t, q_gain, k_gain, l, dims):
    nt = proj.shape[0]
    attn_w, kv_w = dims["attn_w"], dims["kv_w"]
    tm = _qk_tile(dims)
    assert dims["o_q"] % attn_w == 0 and dims["o_k"] % kv_w == 0
    n_lat = dims["n_lat"] // tm
    n_seq = dims["seq"] // tm
    tab = lambda i: (jnp.where(i < n_lat, i % n_seq, n_seq), 0)
    return pl.pallas_call(
        functools.partial(_qk_kernel, n_q=attn_w // HEAD_DIM, n_kv=kv_w // HEAD_DIM,
                          scale=HEAD_DIM ** -0.5 * math.log2(math.e)),
        grid=(nt // tm,),
        in_specs=[
            pl.BlockSpec((tm, attn_w), lambda i: (i, dims["o_q"] // attn_w)),
            pl.BlockSpec((tm, kv_w), lambda i: (i, dims["o_k"] // kv_w)),
            pl.BlockSpec((tm, HEAD_DIM), tab),
            pl.BlockSpec((tm, HEAD_DIM), tab),
            pl.BlockSpec((None, 1, HEAD_DIM), lambda i: (l, 0, 0)),
            pl.BlockSpec((None, 1, HEAD_DIM), lambda i: (l, 0, 0)),
        ],
        out_specs=[
            pl.BlockSpec((tm, attn_w), lambda i: (i, 0)),
            pl.BlockSpec((tm, kv_w), lambda i: (i, 0)),
        ],
        out_shape=[
            jax.ShapeDtypeStruct((nt, attn_w), BF16),
            jax.ShapeDtypeStruct((nt, kv_w), BF16),
        ],
        compiler_params=_params(("parallel",)),
        name="qk_prepare",
    )(proj, proj, cos_t, sin_t, q_gain, k_gain)


def _rope_tables(seq, pad_rows):
    rows = seq // GRID_W
    row = jnp.repeat(jnp.arange(rows), GRID_W)
    col = jnp.tile(jnp.arange(GRID_W), rows)
    half = HEAD_DIM // 4
    inv_freq = ROPE_BASE ** (-jnp.arange(half, dtype=F32) / half)
    ang_r = row.astype(F32)[:, None] * inv_freq
    ang_c = col.astype(F32)[:, None] * inv_freq
    cr, sr, cc, sc = jnp.cos(ang_r), jnp.sin(ang_r), jnp.cos(ang_c), jnp.sin(ang_c)
    cos_t = jnp.concatenate([cr, cc, cr, cc], axis=-1)
    sin_t = jnp.concatenate([-sr, -sc, sr, sc], axis=-1)
    cos_t = jnp.concatenate([cos_t, jnp.ones((pad_rows, HEAD_DIM), F32)], axis=0)
    sin_t = jnp.concatenate([sin_t, jnp.zeros((pad_rows, HEAD_DIM), F32)], axis=0)
    return cos_t, sin_t


def _key_chunks(n_keys):
    unit = MXU_TILE if n_keys % MXU_TILE == 0 else LANES
    units = n_keys // unit
    n_chunks = -(-n_keys // MAX_KEY_CHUNK)
    sizes = [(units // n_chunks + (c < units % n_chunks)) * unit for c in range(n_chunks)]
    return [(sum(sizes[:c]), sizes[c]) for c in range(n_chunks)]


def _attn_kernel(*refs, tq, chunks, group, n_src):
    q_ref = refs[0]
    k_refs, v_refs = refs[1:1 + 2 * n_src:2], refs[2:2 + 2 * n_src:2]
    o_ref, ke_scr, ve_scr, s_scr, m_scr, acc_scr = refs[-6:]

    @pl.when(pl.program_id(2) == 0)
    def _():
        r0 = 0
        for k_ref, v_ref in zip(k_refs, v_refs):
            n = k_ref.shape[0]
            ke_scr[r0:r0 + n, :] = k_ref[...]
            ve_scr[r0:r0 + n, :HEAD_DIM] = v_ref[...]
            r0 += n
        ve_scr[:, HEAD_DIM:] = jnp.ones((ve_scr.shape[0], HEAD_DIM), BF16)

    q = jnp.concatenate([q_ref[:, HEAD_DIM * g:HEAD_DIM * (g + 1)] for g in range(group)], axis=0)

    one = jnp.minimum(pl.program_id(2) + 1, 1)

    def first(_, carry):
        m = None
        for k0, nk in chunks:
            s = lax.dot_general(q, ke_scr[k0:k0 + nk, :], (((1,), (1,)), ((), ())), preferred_element_type=F32)
            s_scr[:, k0:k0 + nk] = s
            for t in range(nk // LANES):
                slab = s[:, LANES * t:LANES * (t + 1)]
                m = slab if m is None else jnp.maximum(m, slab)
        m_scr[...] = m
        return carry

    lax.fori_loop(0, one, first, 0)
    m_scr[...] = jnp.broadcast_to(jnp.max(m_scr[...], axis=-1, keepdims=True), m_scr.shape)

    def second(_, carry):
        m_rep = m_scr[...]
        acc = None
        for k0, nk in chunks:
            p = jnp.concatenate(
                [jnp.exp2(s_scr[:, k0 + LANES * t:k0 + LANES * (t + 1)] - m_rep).astype(BF16)
                 for t in range(nk // LANES)], axis=1)
            part = jnp.dot(p, ve_scr[k0:k0 + nk, :], preferred_element_type=F32)
            acc = part if acc is None else acc + part
        acc_scr[...] = acc
        return carry

    lax.fori_loop(0, one, second, 0)
    acc = acc_scr[...]
    o = acc[:, :HEAD_DIM] / acc[:, HEAD_DIM:]
    for g in range(group):
        o_ref[:, HEAD_DIM * g:HEAD_DIM * (g + 1)] = o[tq * g:tq * (g + 1)].astype(o_ref.dtype)


def _attention(q, k, main, prev, dims):
    nt = q.shape[0]
    attn_w, seq, ctx, batch = dims["attn_w"], dims["seq"], dims["ctx"], dims["batch"]
    latent = prev is None
    group = attn_w // HEAD_DIM // N_KV_HEADS
    gw = group * HEAD_DIM
    v_blk = dims["o_v"] // HEAD_DIM
    lq = seq if latent else ctx
    tq = _pick([lq], [256, 128, 64, 32, 16])
    n_keys = ctx + seq if latent else ctx
    row0 = 0 if latent else dims["n_lat"] // tq
    ctx0 = dims["n_lat"] // ctx
    rows = group * tq
    in_specs = [
        pl.BlockSpec((tq, gw), lambda b, h, i: (row0 + b * (lq // tq) + i, h)),
        pl.BlockSpec((ctx, HEAD_DIM), lambda b, h, i: (ctx0 + b, h)),
        pl.BlockSpec((ctx, HEAD_DIM), lambda b, h, i: (ctx0 + b, v_blk + h)),
    ]
    args = [q, k, main]
    aliases = {}
    if latent:
        in_specs += [
            pl.BlockSpec((seq, HEAD_DIM), lambda b, h, i: (b, h), pipeline_mode=pl.Buffered(1)),
            pl.BlockSpec((seq, HEAD_DIM), lambda b, h, i: (b, v_blk + h), pipeline_mode=pl.Buffered(1)),
        ]
        args += [k, main]
    n_src = (len(args) - 1) // 2
    if not latent:
        in_specs.append(pl.BlockSpec(memory_space=pl.ANY))
        args.append(prev)
        aliases = {len(args) - 1: 0}
    scratch = [pltpu.VMEM((n_keys, HEAD_DIM), BF16), pltpu.VMEM((n_keys, 2 * HEAD_DIM), BF16),
               pltpu.VMEM((rows, n_keys), F32), pltpu.VMEM((rows, LANES), F32),
               pltpu.VMEM((rows, 2 * HEAD_DIM), F32)]
    return pl.pallas_call(
        functools.partial(_attn_kernel, tq=tq, chunks=_key_chunks(n_keys), group=group, n_src=n_src),
        grid=(batch, N_KV_HEADS, lq // tq),
        in_specs=in_specs,
        out_specs=pl.BlockSpec((tq, gw), lambda b, h, i: (row0 + b * (lq // tq) + i, h)),
        out_shape=jax.ShapeDtypeStruct((nt, attn_w), BF16),
        scratch_shapes=scratch,
        input_output_aliases=aliases,
        compiler_params=_params(("parallel", "parallel", "arbitrary")),
        name="attention_latent" if latent else "attention_context",
    )(*args)


def _ssm_params_kernel(lrr_ref, lir_ref, ldr_ref, lrc_ref, lic_ref, ldc_ref, btr_ref, bti_ref,
                       ctr_ref, cti_ref, r1_ref, v_ref, a_ref):
    t = SSM_CHUNK
    pw = SSM_GROUP * t
    r1_ref[...] = jnp.zeros(r1_ref.shape, r1_ref.dtype)
    v_ref[...] = jnp.zeros(v_ref.shape, v_ref.dtype)
    lane = lax.broadcasted_iota(jnp.int32, (SSM_GROUP, LANES), 1)
    col = lax.broadcasted_iota(jnp.int32, (SSM_GROUP, pw), 1)
    tau_col = (lax.broadcasted_iota(jnp.int32, (LANES, pw), 1) // SSM_GROUP).astype(F32)
    tau_row = lax.broadcasted_iota(jnp.int32, (t, LANES), 0).astype(F32)
    masks = (lane < SSM_STATE, lane >= SSM_STATE)
    toep = []
    for d in range(2):
        lr = jnp.minimum(lrr_ref[d], SSM_RE_MAX)
        li = lir_ref[d]
        dt = jnp.exp(ldr_ref[d])
        mag = jnp.exp(lr * dt)
        ab_re = mag * jnp.cos(li * dt)
        ab_im = mag * jnp.sin(li * dt)
        nr = ab_re - 1.0
        den = lr * lr + li * li
        f_re = (nr * lr + ab_im * li) / den
        f_im = (ab_im * lr - nr * li) / den
        bb_re = f_re * btr_ref[d] - f_im * bti_ref[d]
        bb_im = f_re * bti_ref[d] + f_im * btr_ref[d]
        e16 = jnp.exp(lr * dt * float(t))
        a_ref[2 * d:2 * d + 1, :] = e16 * jnp.cos(li * dt * float(t))
        a_ref[2 * d + 1:2 * d + 2, :] = e16 * jnp.sin(li * dt * float(t))
        tau_s = (t - 1.0) - tau_row if d == 0 else tau_row
        es = jnp.exp(lr * dt * tau_s)
        pw_re = es * jnp.cos(li * dt * tau_s)
        pw_im = es * jnp.sin(li * dt * tau_s)
        for s in range(t):
            ar = pw_re[s:s + 1]
            ai = pw_im[s:s + 1]
            w_re = ar * bb_re - ai * bb_im
            w_im = ar * bb_im + ai * bb_re
            for gi in range(2):
                r0 = gi * pw + SSM_GROUP * s
                c0 = 2 * pw + 2 * LANES * d
                r1_ref[r0:r0 + SSM_GROUP, c0:c0 + LANES] = jnp.where(masks[gi], w_re, 0.0).astype(r1_ref.dtype)
                r1_ref[r0:r0 + SSM_GROUP, c0 + LANES:c0 + 2 * LANES] = (
                    jnp.where(masks[gi], w_im, 0.0).astype(r1_ref.dtype))
        lrc = jnp.minimum(lrc_ref[d], SSM_RE_MAX)
        lic = lic_ref[d]
        dtc = jnp.exp(ldc_ref[d])
        tau_q = tau_col if d == 0 else (t - 1.0) - tau_col
        eq = jnp.exp(lrc * dtc * tau_q)
        p_re = eq * jnp.cos(lic * dtc * tau_q)
        p_im = eq * jnp.sin(lic * dtc * tau_q)
        c_re = ctr_ref[d]
        c_im = cti_ref[d]
        q_re = p_re * c_re - p_im * c_im
        q_im = p_re * c_im + p_im * c_re
        per_group = []
        for gi in range(2):
            k_mat = (jnp.dot(jnp.where(masks[gi], bb_re, 0.0), q_re, precision=lax.Precision.HIGHEST,
                             preferred_element_type=F32)
                     - jnp.dot(jnp.where(masks[gi], bb_im, 0.0), q_im, precision=lax.Precision.HIGHEST,
                               preferred_element_type=F32))
            per_group.append(k_mat)
        toep.append(per_group)
        magc = jnp.exp(lrc * dtc)
        abc_re = magc * jnp.cos(lic * dtc)
        abc_im = magc * jnp.sin(lic * dtc)
        q1_re = abc_re * q_re - abc_im * q_im
        q1_im = abc_re * q_im + abc_im * q_re
        for gi in range(2):
            rows = slice(SSM_STATE * gi, SSM_STATE * (gi + 1))
            r0 = 2 * LANES * d + SSM_STATE * gi
            v_ref[r0:r0 + SSM_STATE, pw * gi:pw * (gi + 1)] = q1_re[rows].astype(v_ref.dtype)
            v_ref[r0 + LANES:r0 + LANES + SSM_STATE, pw * gi:pw * (gi + 1)] = (-q1_im[rows]).astype(v_ref.dtype)
    for gi in range(2):
        k_f, k_b = toep[0][gi], toep[1][gi]
        for s in range(t):
            sh = SSM_GROUP * s
            f = k_f if sh == 0 else jnp.where(col >= sh, pltpu.roll(k_f, sh, 1), 0.0)
            back = SSM_GROUP * (t - 1 - s)
            b = k_b if back == 0 else jnp.where(col < pw - back, pltpu.roll(k_b, pw - back, 1), 0.0)
            r0 = gi * pw + sh
            r1_ref[r0:r0 + SSM_GROUP, pw * gi:pw * (gi + 1)] = (f + b).astype(r1_ref.dtype)


def _pair_rows(a):
    depth, _, groups, n = a.shape
    return a.reshape(depth, 2, groups // 2, 2 * n).transpose(0, 2, 1, 3)


def _ssm_parameters(lam_re, lam_im, log_dt, b_re, b_im, c_re, c_im):
    depth, _, groups, n = lam_re.shape
    pairs = groups // 2
    t = SSM_CHUNK
    pw = SSM_GROUP * t
    ld = jnp.broadcast_to(log_dt[..., None], lam_re.shape)
    rows = [_pair_rows(a.astype(F32))[:, :, :, None, :] for a in (lam_re, lam_im, ld)]
    cols = [_pair_rows(a.astype(F32))[:, :, :, :, None] for a in (lam_re, lam_im, ld)]

    def bt(b):
        return (b.astype(F32).reshape(depth, 2, pairs, 2, n, SSM_GROUP).transpose(0, 2, 1, 5, 3, 4)
                .reshape(depth, pairs, 2, SSM_GROUP, 2 * n))

    def ct(c):
        c = (c.astype(F32).reshape(depth, 2, pairs, 2, SSM_GROUP, n).transpose(0, 2, 1, 3, 5, 4)
             .reshape(depth, pairs, 2, 2 * n, SSM_GROUP))
        return jnp.tile(c, (1, 1, 1, 1, t))

    row_spec = pl.BlockSpec((None, None, 2, 1, 2 * n), lambda l, j: (l, j, 0, 0, 0))
    col_spec = pl.BlockSpec((None, None, 2, 2 * n, 1), lambda l, j: (l, j, 0, 0, 0))
    bt_spec = pl.BlockSpec((None, None, 2, SSM_GROUP, 2 * n), lambda l, j: (l, j, 0, 0, 0))
    ct_spec = pl.BlockSpec((None, None, 2, 2 * n, pw), lambda l, j: (l, j, 0, 0, 0))
    return pl.pallas_call(
        _ssm_params_kernel,
        grid=(depth, pairs),
        in_specs=[row_spec] * 3 + [col_spec] * 3 + [bt_spec] * 2 + [ct_spec] * 2,
        out_specs=[
            pl.BlockSpec((None, None, 2 * pw, 4 * pw), lambda l, j: (l, j, 0, 0)),
            pl.BlockSpec((None, None, 8 * n, 2 * pw), lambda l, j: (l, j, 0, 0)),
            pl.BlockSpec((None, 4, 2 * n), lambda l, j: (l, 0, j)),
        ],
        out_shape=[
            jax.ShapeDtypeStruct((depth, pairs, 2 * pw, 4 * pw), BF16),
            jax.ShapeDtypeStruct((depth, pairs, 8 * n, 2 * pw), BF16),
            jax.ShapeDtypeStruct((depth, 4, groups * n), F32),
        ],
        compiler_params=_params(("parallel", "parallel")),
        name="ssm_parameters",
    )(*rows, *cols, bt(b_re), bt(b_im), ct(c_re), ct(c_im))


def _chunk_rows(u_ref):
    return jnp.concatenate([u_ref[t] for t in range(u_ref.shape[0])], axis=-1)


def _scan_chunk_states(s_ref, a_ref, h_ref, *, batch, lat_tiles, ctx_tiles):
    w = s_ref.shape[-1]
    per_tile = SUBLANES // batch
    row_grp = lax.broadcasted_iota(jnp.int32, (SUBLANES, w), 0) // batch
    zero = jnp.zeros((SUBLANES, w), F32)

    def make_tile(k, order, shift):
        ar = a_ref[k:k + 1, :]
        ai = a_ref[k + 1:k + 2, :]

        def tile(ti, carry):
            cur_re, cur_im = carry
            rows = pl.ds(pl.multiple_of(ti * SUBLANES, SUBLANES), SUBLANES)
            s_re = s_ref[k, rows, :]
            s_im = s_ref[k + 1, rows, :]
            h_re, h_im = zero, zero
            for pos in order:
                sel = row_grp == pos
                h_re = jnp.where(sel, cur_re, h_re)
                h_im = jnp.where(sel, cur_im, h_im)
                nxt_re = ar * cur_re - ai * cur_im + s_re
                nxt_im = ar * cur_im + ai * cur_re + s_im
                if batch == SUBLANES:
                    cur_re, cur_im = nxt_re, nxt_im
                else:
                    cur_re = pltpu.roll(nxt_re, shift, 0)
                    cur_im = pltpu.roll(nxt_im, shift, 0)
            h_ref[k, rows, :] = h_re
            h_ref[k + 1, rows, :] = h_im
            return cur_re, cur_im

        return tile

    fwd = make_tile(0, range(per_tile), batch)
    bwd = make_tile(2, range(per_tile - 1, -1, -1), SUBLANES - batch)

    def both(first_tile, n_tiles):
        def step(i, carry):
            return fwd(first_tile + i, carry[0]), bwd(first_tile + n_tiles - 1 - i, carry[1])
        return step

    carry = lax.fori_loop(0, ctx_tiles, both(lat_tiles, ctx_tiles), ((zero, zero), (zero, zero)))
    lax.fori_loop(0, lat_tiles, both(0, lat_tiles), carry)


def _ssm_core_kernel(u_ref, r1_ref, v_ref, a_ref, d_ref, y_ref, s_scr, h_scr, *, batch, lat_tiles, ctx_tiles):
    n_t = u_ref.shape[0]
    kw = n_t * LANES
    u = _chunk_rows(u_ref)
    r = jnp.dot(u, r1_ref[...], preferred_element_type=F32)
    for t in range(n_t):
        y_ref[t] = r[:, LANES * t:LANES * (t + 1)]
    for k in range(4):
        s_scr[k] = r[:, kw + LANES * k:kw + LANES * (k + 1)]
    _scan_chunk_states(s_scr, a_ref, h_scr, batch=batch, lat_tiles=lat_tiles, ctx_tiles=ctx_tiles)
    h = jnp.concatenate([h_scr[k] for k in range(4)], axis=-1).astype(BF16)
    y = jnp.dot(h, v_ref[...], preferred_element_type=F32) + d_ref[...] * u.astype(F32)
    for t in range(n_t):
        y_ref[t] += y[:, LANES * t:LANES * (t + 1)]


def _ssm_core(u, r1, v, a16, dsk, l, dims):
    pairs, n_t, nc, _ = u.shape
    kw = n_t * LANES
    batch = dims["batch"]
    lat_tiles = dims["seq"] // SSM_CHUNK * batch // SUBLANES
    ctx_tiles = dims["ctx"] // SSM_CHUNK * batch // SUBLANES
    return pl.pallas_call(
        functools.partial(_ssm_core_kernel, batch=batch, lat_tiles=lat_tiles, ctx_tiles=ctx_tiles),
        grid=(pairs,),
        in_specs=[
            pl.BlockSpec((None, n_t, nc, LANES), lambda j: (j, 0, 0, 0)),
            pl.BlockSpec((None, None, kw, 2 * kw), lambda j: (l, j, 0, 0)),
            pl.BlockSpec((None, None, 4 * LANES, kw), lambda j: (l, j, 0, 0)),
            pl.BlockSpec((None, 4, LANES), lambda j: (l, 0, j)),
            pl.BlockSpec((None, None, 1, kw), lambda j: (l, j, 0, 0)),
        ],
        out_specs=pl.BlockSpec((None, n_t, nc, LANES), lambda j: (j, 0, 0, 0)),
        out_shape=jax.ShapeDtypeStruct((pairs, n_t, nc, LANES), F32),
        scratch_shapes=[pltpu.VMEM((4, nc, LANES), F32), pltpu.VMEM((4, nc, LANES), F32)],
        compiler_params=_params(("parallel",)),
        name="ssm_core",
    )(u, r1, v, a16, dsk)


GROUPS_PER_TILE = LANES // SSM_GROUP


def _lane_slot(rows):
    return lax.broadcasted_iota(jnp.int32, (rows, LANES), 1) // SSM_GROUP


def _block_transposes(sets, slot):
    sets = [list(t) for t in sets]
    d = GROUPS_PER_TILE // 2
    while d:
        low = (slot & d) == 0
        for t in sets:
            for a in range(GROUPS_PER_TILE):
                if not a & d:
                    lo, hi = t[a], t[a + d]
                    t[a] = jnp.where(low, lo, pltpu.roll(hi, d * SSM_GROUP, 1))
                    t[a + d] = jnp.where(low, pltpu.roll(lo, LANES - d * SSM_GROUP, 1), hi)
        d //= 2
    return sets


def _chunk_tiling(dims):
    ncl, ncc = dims["seq"] // SSM_CHUNK, dims["ctx"] // SSM_CHUNK
    ct = _pick([ncl, ncc], [16, 8, 4, 2, 1])
    nlt, nct = ncl // ct, ncc // ct

    def token_block(k, b):
        return jnp.where(k < nlt, b * nlt + k, dims["batch"] * nlt + b * nct + (k - nlt))

    return ct, nlt + nct, token_block


def _ssm_pack_kernel(*refs, batch, ct):
    x_refs, o_ref, scr = refs[:batch], refs[batch], refs[batch + 1]
    slot = _lane_slot(ct)
    halves = SSM_CHUNK // GROUPS_PER_TILE
    for b in range(batch):
        where = [(j, half) for j in range(x_refs[b].shape[0]) for half in range(halves)]
        sets = [[x_refs[b][j, pl.ds(GROUPS_PER_TILE * half + k, ct, stride=SSM_CHUNK), :]
                 for k in range(GROUPS_PER_TILE)] for j, half in where]
        for (j, half), outs in zip(where, _block_transposes(sets, slot)):
            for gg, out in enumerate(outs):
                g = GROUPS_PER_TILE * j + gg
                scr[g // 2, (g % 2) * halves + half, pl.ds(b, ct, stride=batch), :] = out
    o_ref[...] = scr[...].astype(o_ref.dtype)


def _ssm_pack(u32, dims):
    batch, pairs = dims["batch"], dims["pairs"]
    ct, n_tiles, token_block = _chunk_tiling(dims)
    n_in, nt, _ = u32.shape
    n_t = 2 * SSM_CHUNK * SSM_GROUP // LANES
    return pl.pallas_call(
        functools.partial(_ssm_pack_kernel, batch=batch, ct=ct),
        grid=(n_tiles,),
        in_specs=[pl.BlockSpec((n_in, ct * SSM_CHUNK, LANES),
                               functools.partial(lambda k, b: (0, token_block(k, b), 0), b=b))
                  for b in range(batch)],
        out_specs=pl.BlockSpec((pairs, n_t, ct * batch, LANES), lambda k: (0, 0, k, 0)),
        out_shape=jax.ShapeDtypeStruct((pairs, n_t, nt // SSM_CHUNK, LANES), BF16),
        scratch_shapes=[pltpu.VMEM((pairs, n_t, ct * batch, LANES), F32)],
        compiler_params=_params(("parallel",)),
        name="ssm_pack",
    )(*([u32] * batch))


def _ssm_unpack_kernel(y_ref, o_ref, *, batch, ct):
    b = pl.program_id(1)
    slot = _lane_slot(ct)
    halves = SSM_CHUNK // GROUPS_PER_TILE
    where = [(j, half) for j in range(o_ref.shape[0]) for half in range(halves)]
    sets = []
    for j, half in where:
        groups = [GROUPS_PER_TILE * j + gg for gg in range(GROUPS_PER_TILE)]
        sets.append([y_ref[g // 2, (g % 2) * halves + half, pl.ds(b, ct, stride=batch), :] for g in groups])
    for (j, half), outs in zip(where, _block_transposes(sets, slot)):
        for k, out in enumerate(outs):
            o_ref[j, pl.ds(GROUPS_PER_TILE * half + k, ct, stride=SSM_CHUNK), :] = out


def _ssm_unpack(y, dims):
    batch = dims["batch"]
    pairs, n_t, nc, _ = y.shape
    ct, n_tiles, token_block = _chunk_tiling(dims)
    n_out = dims["ssm_w"] // LANES
    return pl.pallas_call(
        functools.partial(_ssm_unpack_kernel, batch=batch, ct=ct),
        grid=(n_tiles, batch),
        in_specs=[pl.BlockSpec((pairs, n_t, ct * batch, LANES), lambda k, b: (0, 0, k, 0))],
        out_specs=pl.BlockSpec((n_out, ct * SSM_CHUNK, LANES), lambda k, b: (0, token_block(k, b), 0)),
        out_shape=jax.ShapeDtypeStruct((n_out, nc * SSM_CHUNK, LANES), F32),
        compiler_params=_params(("parallel", "arbitrary")),
        name="ssm_unpack",
    )(y)


def _merge_kernel(pb_ref, pc_ref, pv_ref, hpc_ref, hpv_ref, hnc_ref, hnv_ref, cw_ref, ys_ref, at_ref,
                  g_ref, wc_ref, wg_ref, wa_ref, wo_ref, x_ref, gate_ref, gpost_ref, gn_ref, shn_ref, scn_ref,
                  o_ref, h_ref, m_scr, *, tm, tn, seq, ctx, n_lat):
    d = o_ref.shape[1]
    r0 = pl.program_id(0) * tm
    period = jnp.where(r0 >= n_lat, ctx, seq)
    starts = r0 % period == 0
    ends = (r0 + tm) % period == 0
    w = pc_ref[...].astype(F32) * pv_ref[...].astype(F32)
    prev = (hpc_ref[...].astype(F32) * hpv_ref[...].astype(F32))[HALO_ROWS - 1:HALO_ROWS]
    nxt = (hnc_ref[...].astype(F32) * hnv_ref[...].astype(F32))[0:1]
    prev = jnp.where(starts, 0.0, prev)
    nxt = jnp.where(ends, 0.0, nxt)
    row = lax.broadcasted_iota(jnp.int32, w.shape, 0)
    up = jnp.where(row == 0, prev, pltpu.roll(w, 1, 0))
    dn = jnp.where(row == tm - 1, nxt, pltpu.roll(w, tm - 1, 0))
    cw = cw_ref[...]
    cf = (pb_ref[...].astype(F32) * (cw[0:1] * up + cw[1:2] * w + cw[2:3] * dn)).astype(BF16)
    yg = jax.nn.gelu(_chunk_rows(ys_ref)).astype(BF16)
    at = at_ref[...]
    for c in range(d // tn):
        cols = slice(tn * c, tn * (c + 1))
        gate = lambda k: jax.nn.sigmoid(g_ref[:, k * d + tn * c:k * d + tn * (c + 1)].astype(F32))
        ya = jnp.dot(cf, wc_ref[:, cols], preferred_element_type=F32)
        yc = jnp.dot(at, wa_ref[:, cols], preferred_element_type=F32)
        ys = jnp.dot(yg, wg_ref[:, cols], preferred_element_type=F32) * jax.nn.sigmoid(
            jnp.dot(yg, wg_ref[:, d + tn * c:d + tn * (c + 1)], preferred_element_type=F32))
        m_scr[:, cols] = (gate(0) * ya + gate(1) * ys + gate(2) * yc).astype(m_scr.dtype)
    r = jnp.dot(m_scr[...], wo_ref[...], preferred_element_type=F32)
    x = x_ref[...] + gate_ref[...] * _rms(r, gpost_ref[...])
    o_ref[...] = x
    h_ref[...] = _norm_mod(x, gn_ref[...], shn_ref[...], scn_ref[...])


def _merge(main, gates, ys, attn, conv_w, w_conv_out, w_glu, w_attn_out, w_out, xt, modr, g_post, g_pre_mlp,
           l, dims, rows):
    d = dims["d"]
    cw_, sw, aw = dims["conv_w"], dims["ssm_w"], dims["attn_w"]
    tm = _pick([dims["seq"], dims["ctx"]], [512, 256, 128, 64, 32])
    tn = _pick([d], [512, 256, 128])
    n_halo = main.shape[0] // HALO_ROWS
    per = tm // HALO_ROWS
    prev_blk = lambda i: jnp.maximum(i * per - 1, 0)
    next_blk = lambda i: jnp.minimum((i + 1) * per, n_halo - 1)
    grp = _row_group(dims, tm)
    mod = lambda k: pl.BlockSpec((None, None, 1, d), lambda i: (l, grp(i), 0, k))
    vec = pl.BlockSpec((None, 1, d), lambda i: (l, 0, 0))
    row_tile = pl.BlockSpec((tm, d), lambda i: (i, 0))
    resident = lambda shape: pl.BlockSpec((None,) + shape, lambda i: (l, 0, 0), pipeline_mode=pl.Buffered(1))
    return pl.pallas_call(
        functools.partial(_merge_kernel, tm=tm, tn=tn, seq=dims["seq"], ctx=dims["ctx"], n_lat=dims["n_lat"]),
        grid=(rows // tm,),
        in_specs=[
            pl.BlockSpec((tm, cw_), lambda i: (i, 0)),
            pl.BlockSpec((tm, cw_), lambda i: (i, 1)),
            pl.BlockSpec((tm, cw_), lambda i: (i, 2)),
            pl.BlockSpec((HALO_ROWS, cw_), lambda i: (prev_blk(i), 1)),
            pl.BlockSpec((HALO_ROWS, cw_), lambda i: (prev_blk(i), 2)),
            pl.BlockSpec((HALO_ROWS, cw_), lambda i: (next_blk(i), 1)),
            pl.BlockSpec((HALO_ROWS, cw_), lambda i: (next_blk(i), 2)),
            resident((CONV_K, cw_)),
            pl.BlockSpec((sw // LANES, tm, LANES), lambda i: (0, i, 0)),
            pl.BlockSpec((tm, aw), lambda i: (i, 0)),
            pl.BlockSpec((tm, 3 * d), lambda i: (i, 0)),
            resident((cw_, d)),
            resident((sw, 2 * d)),
            resident((aw, d)),
            resident((d, d)),
            row_tile, mod(2), vec, vec, mod(3), mod(4),
        ],
        out_specs=[row_tile, row_tile],
        out_shape=[jax.ShapeDtypeStruct(xt.shape, F32), jax.ShapeDtypeStruct((rows, d), BF16)],
        scratch_shapes=[pltpu.VMEM((tm, d), BF16)],
        input_output_aliases={15: 0},
        compiler_params=_params(("parallel",)),
        name="mixer_output",
    )(main, main, main, main, main, main, main, conv_w, ys, attn, gates, w_conv_out, w_glu, w_attn_out,
      w_out, xt, modr, g_post, g_pre_mlp, modr, modr)


def _mlp_kernel(x_ref, h_ref, wu_ref, wd_ref, gate_ref, gpost_ref, *rest, with_next):
    j = pl.program_id(1)
    if with_next:
        gn_ref, shn_ref, scn_ref, o_ref, hn_ref, acc_scr = rest
    else:
        o_ref, acc_scr = rest

    last = pl.num_programs(1) - 1

    def partial_out(rows):
        u = jnp.maximum(jnp.dot(h_ref[rows, :], wu_ref[...], preferred_element_type=F32), 0.0)
        return jnp.dot((u * u).astype(BF16), wd_ref[...], preferred_element_type=F32)

    every = slice(None)

    @pl.when(j == 0)
    def _():
        acc_scr[...] = partial_out(every)

    @pl.when(jnp.logical_and(j > 0, j < last))
    def _():
        acc_scr[...] += partial_out(every)

    @pl.when(j == last)
    def _():
        part = acc_scr.shape[0] // MLP_TAIL_PARTS
        for k in range(MLP_TAIL_PARTS):
            rows = slice(part * k, part * (k + 1))
            r = acc_scr[rows, :] + partial_out(rows)
            x = x_ref[rows, :] + gate_ref[...] * _rms(r, gpost_ref[...])
            o_ref[rows, :] = x
            if with_next:
                hn_ref[rows, :] = _norm_mod(x, gn_ref[...], shn_ref[...], scn_ref[...])


def _mlp(xt, h, g_post, modr, w_up, w_down, l, dims, rows, g_next=None):
    d = xt.shape[1]
    tf = w_up.shape[-1]
    d_ff = w_up.shape[1] * tf
    tm = _pick([dims["seq"], dims["n_ctx"]], [512, 256, 128])
    grp = _row_group(dims, tm)
    mod = lambda layer, k: pl.BlockSpec((None, None, 1, d), lambda i, j: (layer, grp(i), 0, k))
    vec = lambda layer: pl.BlockSpec((None, 1, d), lambda i, j: (layer, 0, 0))
    row_tile = pl.BlockSpec((tm, d), lambda i, j: (i, 0))
    in_specs = [
        row_tile, row_tile,
        pl.BlockSpec((None, None, d, tf), lambda i, j: (l, j, 0, 0)),
        pl.BlockSpec((None, tf, d), lambda i, j: (l, j, 0)),
        mod(l, 5), vec(l),
    ]
    args = [xt, h, w_up, w_down, modr, g_post]
    out_specs, out_shape = [row_tile], [jax.ShapeDtypeStruct((rows, d), F32)]
    if g_next is not None:
        assert rows == xt.shape[0]
        in_specs += [vec(l + 1), mod(l + 1, 0), mod(l + 1, 1)]
        args += [g_next, modr, modr]
        out_specs.append(row_tile)
        out_shape.append(jax.ShapeDtypeStruct((rows, d), BF16))
    return pl.pallas_call(
        functools.partial(_mlp_kernel, with_next=g_next is not None),
        grid=(rows // tm, d_ff // tf),
        in_specs=in_specs,
        out_specs=out_specs,
        out_shape=out_shape,
        scratch_shapes=[pltpu.VMEM((tm, d), F32)],
        input_output_aliases={0: 0} if g_next is not None else {},
        compiler_params=_params(("parallel", "arbitrary")),
        name="mlp",
    )(*args)


def kernel(x, c, ctx, c_ctx, w_mod, b_mod, g_pre_mix, g_post_mix, g_pre_mlp, g_post_mlp, w_in, conv_w,
           w_conv_out, ssm_lam_re, ssm_lam_im, ssm_log_dt, ssm_b_re, ssm_b_im, ssm_c_re, ssm_c_im, ssm_d,
           w_glu, q_gain, k_gain, w_attn_out, w_out, w_up, w_down):
    batch, seq, d = x.shape
    ctx_len = ctx.shape[1]
    depth = w_in.shape[0]
    in_w = w_in.shape[-1]
    conv_width = conv_w.shape[-1]
    ssm_w = ssm_d.shape[-1]
    attn_w = w_attn_out.shape[1]
    gate_w = 3 * d
    kv_w = (in_w - 3 * conv_width - ssm_w - attn_w - gate_w) // 2
    groups = ssm_lam_re.shape[2]
    assert batch + 1 <= MOD_ROWS and SUBLANES % batch == 0
    assert kv_w == N_KV_HEADS * HEAD_DIM and ssm_lam_re.shape[3] == SSM_STATE
    assert ssm_w == groups * SSM_GROUP and groups % 2 == 0
    assert seq % GRID_W == 0 and (batch * seq) % ctx_len == 0 and ctx_len % LANES == 0
    a_w = 3 * conv_width + ssm_w
    dims = dict(
        batch=batch, seq=seq, ctx=ctx_len, d=d, n_lat=batch * seq, n_ctx=batch * ctx_len,
        conv_w=conv_width, ssm_w=ssm_w, attn_w=attn_w, kv_w=kv_w, gate_w=gate_w, pairs=groups // 2,
        o_q=gate_w, o_k=gate_w + attn_w, o_v=gate_w + attn_w + kv_w,
    )
    n_lat, nt = dims["n_lat"], dims["n_lat"] + dims["n_ctx"]

    cc = jnp.concatenate([c, c_ctx[None, :], jnp.zeros((MOD_ROWS - batch - 1, d), F32)], axis=0)
    mod = _modulation(cc, w_mod, b_mod)
    modr = mod.reshape(depth, MOD_ROWS, 1, N_MOD * d)

    row_vec = lambda a: a.reshape(depth, 1, a.shape[-1])
    g_pre_mix, g_post_mix, g_pre_mlp, g_post_mlp = map(row_vec, (g_pre_mix, g_post_mix, g_pre_mlp, g_post_mlp))
    perm = _head_permutation()
    q_gain, k_gain = row_vec(q_gain[:, perm]), row_vec(k_gain[:, perm])
    w_in_a = w_in[:, :, :a_w].astype(BF16)
    w_qk = w_in[:, :, a_w:a_w + attn_w + kv_w].reshape(depth, d, -1, HEAD_DIM)[..., perm]
    w_in_b = jnp.concatenate([w_in[:, :, in_w - gate_w:], w_qk.reshape(depth, d, attn_w + kv_w),
                              w_in[:, :, a_w + attn_w + kv_w:in_w - gate_w]], axis=-1).astype(BF16)
    w_conv_out, w_glu, w_attn_out, w_out, w_up, w_down = (
        w.astype(BF16) for w in (w_conv_out, w_glu, w_attn_out, w_out, w_up, w_down))
    w_in_a = _col_tiles(w_in_a, _pick([a_w], [1280, 768, 512, 256, 128]))
    w_in_b = _col_tiles(w_in_b, _pick([w_in_b.shape[-1]], [1536, 768, 512, 256, 128]))
    w_up = _col_tiles(w_up, _pick([w_up.shape[-1]], [1024, 512, 256, 128]))

    cos_t, sin_t = _rope_tables(seq, _qk_tile(dims))
    r1, v_mat, a16 = _ssm_parameters(ssm_lam_re, ssm_lam_im, ssm_log_dt, ssm_b_re, ssm_b_im, ssm_c_re, ssm_c_im)
    dsk = jnp.broadcast_to(
        ssm_d.astype(F32).reshape(depth, groups // 2, 2, 1, SSM_GROUP),
        (depth, groups // 2, 2, SSM_CHUNK, SSM_GROUP)).reshape(depth, groups // 2, 1, 2 * SSM_CHUNK * SSM_GROUP)

    xt, h = _prenorm(x.reshape(n_lat, d), ctx.reshape(dims["n_ctx"], d), g_pre_mix, modr, dims)
    for l in range(depth):
        need_ctx = l < depth - 1
        rows = nt if need_ctx else n_lat
        proj_a, u32 = _projection(h, w_in_a, l, dims, f32_cols=(3 * conv_width, ssm_w))
        (proj_b,) = _projection(h, w_in_b, l, dims)
        u = _ssm_pack(u32, dims)
        ys = _ssm_unpack(_ssm_core(u, r1, v_mat, a16, dsk, l, dims), dims)
        q, k = _qk_prepare(proj_b, cos_t, sin_t, q_gain, k_gain, l, dims)
        attn = _attention(q, k, proj_b, None, dims)
        if need_ctx:
            attn = _attention(q, k, proj_b, attn, dims)
        xt, h = _merge(proj_a, proj_b, ys, attn, conv_w, w_conv_out, w_glu, w_attn_out, w_out, xt, modr,
                       g_post_mix, g_pre_mlp, l, dims, rows)
        if need_ctx:
            xt, h = _mlp(xt, h, g_post_mlp, modr, w_up, w_down, l, dims, rows, g_next=g_pre_mix)
        else:
            (xt,) = _mlp(xt, h, g_post_mlp, modr, w_up, w_down, l, dims, rows)
    return xt.reshape(batch, seq, d)
```

```python
import functools
import math

import jax
import jax.numpy as jnp
from jax import lax
from jax.experimental import pallas as pl
from jax.experimental.pallas import tpu as pltpu

F32 = jnp.float32
BF16 = jnp.bfloat16

NORM_EPS = 1e-6
N_MOD = 6
HEAD_DIM = 128
N_KV_HEADS = 2
GRID_W = 64
ROPE_BASE = 10000.0
CONV_K = 3
SSM_GROUP = 16
SSM_STATE = 64
SSM_RE_MAX = -1e-4
SSM_CHUNK = 16
SUBLANES = 8
LANES = 128
HALO_ROWS = 16
MXU_TILE = 256
MAX_KEY_CHUNK = 4352
MLP_TAIL_PARTS = 2
MOD_ROWS = 8
VMEM_LIMIT = 56 * 2**20


def _params(sem, vmem=VMEM_LIMIT):
    return pltpu.CompilerParams(dimension_semantics=sem, vmem_limit_bytes=vmem)


def _pick(dims, cands):
    for c in cands:
        if all(d % c == 0 for d in dims):
            return c
    raise ValueError(f"no tile in {cands} divides {dims}")


def _rms(x, g):
    ms = jnp.mean(x * x, axis=-1, keepdims=True)
    return x * lax.rsqrt(ms + NORM_EPS) * g


def _mod_kernel(c_ref, w_ref, b_ref, o_ref):
    c = c_ref[...]
    s = (c * jax.nn.sigmoid(c)).astype(BF16)
    o_ref[...] = jnp.dot(s, w_ref[...].astype(BF16), preferred_element_type=F32) + b_ref[...]


def _modulation(cc, w_mod, b_mod):
    depth, d, nout = w_mod.shape
    tn = _pick([nout], [1024, 512, 256, 128])
    return pl.pallas_call(
        _mod_kernel,
        grid=(depth, nout // tn),
        in_specs=[
            pl.BlockSpec((MOD_ROWS, d), lambda l, j: (0, 0)),
            pl.BlockSpec((None, d, tn), lambda l, j: (l, 0, j)),
            pl.BlockSpec((None, 1, tn), lambda l, j: (l, 0, j)),
        ],
        out_specs=pl.BlockSpec((None, MOD_ROWS, tn), lambda l, j: (l, 0, j)),
        out_shape=jax.ShapeDtypeStruct((depth, MOD_ROWS, nout), F32),
        compiler_params=_params(("parallel", "parallel")),
        name="modulation",
    )(cc, w_mod, b_mod.reshape(depth, 1, nout))


def _norm_mod(x, g, shift, scale):
    return (_rms(x, g) * (1.0 + scale) + shift).astype(BF16)


def _prenorm_kernel(x_ref, c_ref, g_ref, sh_ref, sc_ref, xt_ref, h_ref, *, lat_tiles):
    x = jnp.where(pl.program_id(0) < lat_tiles, x_ref[...], c_ref[...])
    xt_ref[...] = x
    h_ref[...] = _norm_mod(x, g_ref[...], sh_ref[...], sc_ref[...])


def _row_group(dims, tm):
    return lambda i: jnp.minimum((i * tm) // dims["seq"], dims["batch"])


def _prenorm(x, ctx, gain, modr, dims):
    d = x.shape[1]
    nt = x.shape[0] + ctx.shape[0]
    tm = _pick([dims["seq"], dims["n_ctx"]], [512, 256, 128])
    lat_tiles = x.shape[0] // tm
    grp = _row_group(dims, tm)
    row_tile = pl.BlockSpec((tm, d), lambda i: (i, 0))
    return pl.pallas_call(
        functools.partial(_prenorm_kernel, lat_tiles=lat_tiles),
        grid=(nt // tm,),
        in_specs=[
            pl.BlockSpec((tm, d), lambda i: (jnp.minimum(i, lat_tiles - 1), 0)),
            pl.BlockSpec((tm, d), lambda i: (jnp.maximum(i - lat_tiles, 0), 0)),
            pl.BlockSpec((None, 1, d), lambda i: (0, 0, 0)),
            pl.BlockSpec((None, None, 1, d), lambda i: (0, grp(i), 0, 0)),
            pl.BlockSpec((None, None, 1, d), lambda i: (0, grp(i), 0, 1)),
        ],
        out_specs=[row_tile, row_tile],
        out_shape=[jax.ShapeDtypeStruct((nt, d), F32), jax.ShapeDtypeStruct((nt, d), BF16)],
        compiler_params=_params(("parallel",)),
        name="prenorm",
    )(x, ctx, gain, modr, modr)


def _proj_kernel(h_ref, w_ref, o_ref, *u_refs, u_tile, u_off):
    r = jnp.dot(h_ref[...], w_ref[...], preferred_element_type=F32)
    o_ref[...] = r.astype(o_ref.dtype)
    if u_refs:
        (ou_ref,) = u_refs

        @pl.when(pl.program_id(1) == u_tile)
        def _():
            for t in range(ou_ref.shape[0]):
                ou_ref[t] = r[:, u_off + LANES * t:u_off + LANES * (t + 1)]


def _projection(h, w, l, dims, tn, f32_cols=None):
    nt, d = h.shape
    n_out = w.shape[-1]
    tm = _pick([dims["seq"], dims["n_ctx"]], [1024, 512, 256, 128])
    assert n_out % tn == 0
    out_specs = [pl.BlockSpec((tm, tn), lambda i, j: (i, j))]
    out_shape = [jax.ShapeDtypeStruct((nt, n_out), BF16)]
    u_tile = u_off = 0
    if f32_cols is not None:
        start, width = f32_cols
        u_tile, u_off = start // tn, start % tn
        assert u_off + width <= tn and u_off % LANES == 0 and width % LANES == 0
        out_specs.append(pl.BlockSpec((width // LANES, tm, LANES), lambda i, j: (0, i, 0)))
        out_shape.append(jax.ShapeDtypeStruct((width // LANES, nt, LANES), F32))
    return pl.pallas_call(
        functools.partial(_proj_kernel, u_tile=u_tile, u_off=u_off),
        grid=(nt // tm, n_out // tn),
        in_specs=[
            pl.BlockSpec((tm, d), lambda i, j: (i, 0)),
            pl.BlockSpec((None, d, tn), lambda i, j: (l, 0, j)),
        ],
        out_specs=out_specs,
        out_shape=out_shape,
        compiler_params=_params(("parallel", "arbitrary")),
        name="in_projection",
    )(h, w)


def _head_permutation():
    quarter = HEAD_DIM // 4
    return jnp.concatenate([jnp.arange(quarter * k, quarter * (k + 1)) for k in (0, 2, 1, 3)])


def _qk_kernel(xq_ref, xk_ref, cos_ref, sin_ref, qg_ref, kg_ref, q_ref, k_ref, *, n_q, n_kv, scale):
    cos = cos_ref[...]
    sin = sin_ref[...]
    mean_cols = jnp.full((HEAD_DIM, HEAD_DIM), 1.0 / HEAD_DIM, BF16)
    for h in range(n_q + n_kv):
        if h < n_q:
            xh = xq_ref[:, HEAD_DIM * h:HEAD_DIM * (h + 1)].astype(F32)
        else:
            xh = xk_ref[:, HEAD_DIM * (h - n_q):HEAD_DIM * (h - n_q + 1)].astype(F32)
        gain = qg_ref[...] * scale if h < n_q else kg_ref[...]
        sq = xh * xh
        hi = sq.astype(BF16)
        lo = (sq - hi.astype(F32)).astype(BF16)
        ms = (jnp.dot(hi, mean_cols, preferred_element_type=F32)
              + jnp.dot(lo, mean_cols, preferred_element_type=F32))
        y = xh * lax.rsqrt(ms + NORM_EPS) * gain
        r = (y * cos + pltpu.roll(y, HEAD_DIM // 2, 1) * sin).astype(BF16)
        if h < n_q:
            q_ref[:, HEAD_DIM * h:HEAD_DIM * (h + 1)] = r
        else:
            k_ref[:, HEAD_DIM * (h - n_q):HEAD_DIM * (h - n_q + 1)] = r


def _qk_tile(dims):
    return _pick([dims["seq"], dims["n_ctx"]], [512, 256, 128, 64, 32])


def _qk_prepare(proj, cos_t, sin_t, q_gain, k_gain, l, dims):
    nt = proj.shape[0]
    attn_w, kv_w = dims["attn_w"], dims["kv_w"]
    tm = _qk_tile(dims)
    assert dims["o_q"] % attn_w == 0 and dims["o_k"] % kv_w == 0
    n_lat = dims["n_lat"] // tm
    n_seq = dims["seq"] // tm
    tab = lambda i: (jnp.where(i < n_lat, i % n_seq, n_seq), 0)
    return pl.pallas_call(
        functools.partial(_qk_kernel, n_q=attn_w // HEAD_DIM, n_kv=kv_w // HEAD_DIM,
                          scale=HEAD_DIM ** -0.5 * math.log2(math.e)),
        grid=(nt // tm,),
        in_specs=[
            pl.BlockSpec((tm, attn_w), lambda i: (i, dims["o_q"] // attn_w)),
            pl.BlockSpec((tm, kv_w), lambda i: (i, dims["o_k"] // kv_w)),
            pl.BlockSpec((tm, HEAD_DIM), tab),
            pl.BlockSpec((tm, HEAD_DIM), tab),
            pl.BlockSpec((None, 1, HEAD_DIM), lambda i: (l, 0, 0)),
            pl.BlockSpec((None, 1, HEAD_DIM), lambda i: (l, 0, 0)),
        ],
        out_specs=[
            pl.BlockSpec((tm, attn_w), lambda i: (i, 0)),
            pl.BlockSpec((tm, kv_w), lambda i: (i, 0)),
        ],
        out_shape=[
            jax.ShapeDtypeStruct((nt, attn_w), BF16),
            jax.ShapeDtypeStruct((nt, kv_w), BF16),
        ],
        compiler_params=_params(("parallel",)),
        name="qk_prepare",
    )(proj, proj, cos_t, sin_t, q_gain, k_gain)


def _rope_tables(seq, pad_rows):
    rows = seq // GRID_W
    row = jnp.repeat(jnp.arange(rows), GRID_W)
    col = jnp.tile(jnp.arange(GRID_W), rows)
    half = HEAD_DIM // 4
    inv_freq = ROPE_BASE ** (-jnp.arange(half, dtype=F32) / half)
    ang_r = row.astype(F32)[:, None] * inv_freq
    ang_c = col.astype(F32)[:, None] * inv_freq
    cr, sr, cc, sc = jnp.cos(ang_r), jnp.sin(ang_r), jnp.cos(ang_c), jnp.sin(ang_c)
    cos_t = jnp.concatenate([cr, cc, cr, cc], axis=-1)
    sin_t = jnp.concatenate([-sr, -sc, sr, sc], axis=-1)
    cos_t = jnp.concatenate([cos_t, jnp.ones((pad_rows, HEAD_DIM), F32)], axis=0)
    sin_t = jnp.concatenate([sin_t, jnp.zeros((pad_rows, HEAD_DIM), F32)], axis=0)
    return cos_t, sin_t


def _key_chunks(n_keys):
    unit = MXU_TILE if n_keys % MXU_TILE == 0 else LANES
    units = n_keys // unit
    n_chunks = -(-n_keys // MAX_KEY_CHUNK)
    sizes = [(units // n_chunks + (c < units % n_chunks)) * unit for c in range(n_chunks)]
    return [(sum(sizes[:c]), sizes[c]) for c in range(n_chunks)]


def _attn_kernel(*refs, tq, chunks, group, n_src):
    q_ref = refs[0]
    k_refs, v_refs = refs[1:1 + 2 * n_src:2], refs[2:2 + 2 * n_src:2]
    o_ref, ke_scr, ve_scr, s_scr, m_scr, acc_scr = refs[-6:]

    @pl.when(pl.program_id(2) == 0)
    def _():
        r0 = 0
        for k_ref, v_ref in zip(k_refs, v_refs):
            n = k_ref.shape[0]
            ke_scr[r0:r0 + n, :] = k_ref[...]
            ve_scr[r0:r0 + n, :HEAD_DIM] = v_ref[...]
            r0 += n
        ve_scr[:, HEAD_DIM:] = jnp.ones((ve_scr.shape[0], HEAD_DIM), BF16)

    q = jnp.concatenate([q_ref[:, HEAD_DIM * g:HEAD_DIM * (g + 1)] for g in range(group)], axis=0)

    one = jnp.minimum(pl.program_id(2) + 1, 1)

    def first(_, carry):
        m = None
        for k0, nk in chunks:
            s = lax.dot_general(q, ke_scr[k0:k0 + nk, :], (((1,), (1,)), ((), ())), preferred_element_type=F32)
            s_scr[:, k0:k0 + nk] = s
            for t in range(nk // LANES):
                slab = s[:, LANES * t:LANES * (t + 1)]
                m = slab if m is None else jnp.maximum(m, slab)
        m_scr[...] = m
        return carry

    lax.fori_loop(0, one, first, 0)
    m_scr[...] = jnp.broadcast_to(jnp.max(m_scr[...], axis=-1, keepdims=True), m_scr.shape)

    def second(_, carry):
        m_rep = m_scr[...]
        acc = None
        for k0, nk in chunks:
            p = jnp.concatenate(
                [jnp.exp2(s_scr[:, k0 + LANES * t:k0 + LANES * (t + 1)] - m_rep).astype(BF16)
                 for t in range(nk // LANES)], axis=1)
            part = jnp.dot(p, ve_scr[k0:k0 + nk, :], preferred_element_type=F32)
            acc = part if acc is None else acc + part
        acc_scr[...] = acc
        return carry

    lax.fori_loop(0, one, second, 0)
    acc = acc_scr[...]
    o = acc[:, :HEAD_DIM] / acc[:, HEAD_DIM:]
    for g in range(group):
        o_ref[:, HEAD_DIM * g:HEAD_DIM * (g + 1)] = o[tq * g:tq * (g + 1)].astype(o_ref.dtype)


def _attention(q, k, main, prev, dims):
    nt = q.shape[0]
    attn_w, seq, ctx, batch = dims["attn_w"], dims["seq"], dims["ctx"], dims["batch"]
    latent = prev is None
    group = attn_w // HEAD_DIM // N_KV_HEADS
    gw = group * HEAD_DIM
    v_blk = dims["o_v"] // HEAD_DIM
    lq = seq if latent else ctx
    tq = _pick([lq], [256, 128, 64, 32, 16])
    n_keys = ctx + seq if latent else ctx
    row0 = 0 if latent else dims["n_lat"] // tq
    ctx0 = dims["n_lat"] // ctx
    rows = group * tq
    in_specs = [
        pl.BlockSpec((tq, gw), lambda b, h, i: (row0 + b * (lq // tq) + i, h)),
        pl.BlockSpec((ctx, HEAD_DIM), lambda b, h, i: (ctx0 + b, h)),
        pl.BlockSpec((ctx, HEAD_DIM), lambda b, h, i: (ctx0 + b, v_blk + h)),
    ]
    args = [q, k, main]
    aliases = {}
    if latent:
        in_specs += [
            pl.BlockSpec((seq, HEAD_DIM), lambda b, h, i: (b, h), pipeline_mode=pl.Buffered(1)),
            pl.BlockSpec((seq, HEAD_DIM), lambda b, h, i: (b, v_blk + h), pipeline_mode=pl.Buffered(1)),
        ]
        args += [k, main]
    n_src = (len(args) - 1) // 2
    if not latent:
        in_specs.append(pl.BlockSpec(memory_space=pl.ANY))
        args.append(prev)
        aliases = {len(args) - 1: 0}
    scratch = [pltpu.VMEM((n_keys, HEAD_DIM), BF16), pltpu.VMEM((n_keys, 2 * HEAD_DIM), BF16),
               pltpu.VMEM((rows, n_keys), F32), pltpu.VMEM((rows, LANES), F32),
               pltpu.VMEM((rows, 2 * HEAD_DIM), F32)]
    return pl.pallas_call(
        functools.partial(_attn_kernel, tq=tq, chunks=_key_chunks(n_keys), group=group, n_src=n_src),
        grid=(batch, N_KV_HEADS, lq // tq),
        in_specs=in_specs,
        out_specs=pl.BlockSpec((tq, gw), lambda b, h, i: (row0 + b * (lq // tq) + i, h)),
        out_shape=jax.ShapeDtypeStruct((nt, attn_w), BF16),
        scratch_shapes=scratch,
        input_output_aliases=aliases,
        compiler_params=_params(("parallel", "parallel", "arbitrary")),
        name="attention_latent" if latent else "attention_context",
    )(*args)


def _ssm_params_kernel(lrr_ref, lir_ref, ldr_ref, lrc_ref, lic_ref, ldc_ref, btr_ref, bti_ref,
                       ctr_ref, cti_ref, r1_ref, v_ref, a_ref):
    t = SSM_CHUNK
    pw = SSM_GROUP * t
    r1_ref[...] = jnp.zeros(r1_ref.shape, r1_ref.dtype)
    v_ref[...] = jnp.zeros(v_ref.shape, v_ref.dtype)
    lane = lax.broadcasted_iota(jnp.int32, (SSM_GROUP, LANES), 1)
    col = lax.broadcasted_iota(jnp.int32, (SSM_GROUP, pw), 1)
    tau_col = (lax.broadcasted_iota(jnp.int32, (LANES, pw), 1) // SSM_GROUP).astype(F32)
    tau_row = lax.broadcasted_iota(jnp.int32, (t, LANES), 0).astype(F32)
    masks = (lane < SSM_STATE, lane >= SSM_STATE)
    toep = []
    for d in range(2):
        lr = jnp.minimum(lrr_ref[d], SSM_RE_MAX)
        li = lir_ref[d]
        dt = jnp.exp(ldr_ref[d])
        mag = jnp.exp(lr * dt)
        ab_re = mag * jnp.cos(li * dt)
        ab_im = mag * jnp.sin(li * dt)
        nr = ab_re - 1.0
        den = lr * lr + li * li
        f_re = (nr * lr + ab_im * li) / den
        f_im = (ab_im * lr - nr * li) / den
        bb_re = f_re * btr_ref[d] - f_im * bti_ref[d]
        bb_im = f_re * bti_ref[d] + f_im * btr_ref[d]
        e16 = jnp.exp(lr * dt * float(t))
        a_ref[2 * d:2 * d + 1, :] = e16 * jnp.cos(li * dt * float(t))
        a_ref[2 * d + 1:2 * d + 2, :] = e16 * jnp.sin(li * dt * float(t))
        tau_s = (t - 1.0) - tau_row if d == 0 else tau_row
        es = jnp.exp(lr * dt * tau_s)
        pw_re = es * jnp.cos(li * dt * tau_s)
        pw_im = es * jnp.sin(li * dt * tau_s)
        for s in range(t):
            ar = pw_re[s:s + 1]
            ai = pw_im[s:s + 1]
            w_re = ar * bb_re - ai * bb_im
            w_im = ar * bb_im + ai * bb_re
            for gi in range(2):
                r0 = gi * pw + SSM_GROUP * s
                c0 = 2 * pw + 2 * LANES * d
                r1_ref[r0:r0 + SSM_GROUP, c0:c0 + LANES] = jnp.where(masks[gi], w_re, 0.0).astype(r1_ref.dtype)
                r1_ref[r0:r0 + SSM_GROUP, c0 + LANES:c0 + 2 * LANES] = (
                    jnp.where(masks[gi], w_im, 0.0).astype(r1_ref.dtype))
        lrc = jnp.minimum(lrc_ref[d], SSM_RE_MAX)
        lic = lic_ref[d]
        dtc = jnp.exp(ldc_ref[d])
        tau_q = tau_col if d == 0 else (t - 1.0) - tau_col
        eq = jnp.exp(lrc * dtc * tau_q)
        p_re = eq * jnp.cos(lic * dtc * tau_q)
        p_im = eq * jnp.sin(lic * dtc * tau_q)
        c_re = ctr_ref[d]
        c_im = cti_ref[d]
        q_re = p_re * c_re - p_im * c_im
        q_im = p_re * c_im + p_im * c_re
        per_group = []
        for gi in range(2):
            k_mat = (jnp.dot(jnp.where(masks[gi], bb_re, 0.0), q_re, precision=lax.Precision.HIGHEST,
                             preferred_element_type=F32)
                     - jnp.dot(jnp.where(masks[gi], bb_im, 0.0), q_im, precision=lax.Precision.HIGHEST,
                               preferred_element_type=F32))
            per_group.append(k_mat)
        toep.append(per_group)
        magc = jnp.exp(lrc * dtc)
        abc_re = magc * jnp.cos(lic * dtc)
        abc_im = magc * jnp.sin(lic * dtc)
        q1_re = abc_re * q_re - abc_im * q_im
        q1_im = abc_re * q_im + abc_im * q_re
        for gi in range(2):
            rows = slice(SSM_STATE * gi, SSM_STATE * (gi + 1))
            r0 = 2 * LANES * d + SSM_STATE * gi
            v_ref[r0:r0 + SSM_STATE, pw * gi:pw * (gi + 1)] = q1_re[rows].astype(v_ref.dtype)
            v_ref[r0 + LANES:r0 + LANES + SSM_STATE, pw * gi:pw * (gi + 1)] = (-q1_im[rows]).astype(v_ref.dtype)
    for gi in range(2):
        k_f, k_b = toep[0][gi], toep[1][gi]
        for s in range(t):
            sh = SSM_GROUP * s
            f = k_f if sh == 0 else jnp.where(col >= sh, pltpu.roll(k_f, sh, 1), 0.0)
            back = SSM_GROUP * (t - 1 - s)
            b = k_b if back == 0 else jnp.where(col < pw - back, pltpu.roll(k_b, pw - back, 1), 0.0)
            r0 = gi * pw + sh
            r1_ref[r0:r0 + SSM_GROUP, pw * gi:pw * (gi + 1)] = (f + b).astype(r1_ref.dtype)


def _pair_rows(a):
    depth, _, groups, n = a.shape
    return a.reshape(depth, 2, groups // 2, 2 * n).transpose(0, 2, 1, 3)


def _ssm_parameters(lam_re, lam_im, log_dt, b_re, b_im, c_re, c_im):
    depth, _, groups, n = lam_re.shape
    pairs = groups // 2
    t = SSM_CHUNK
    pw = SSM_GROUP * t
    ld = jnp.broadcast_to(log_dt[..., None], lam_re.shape)
    rows = [_pair_rows(a.astype(F32))[:, :, :, None, :] for a in (lam_re, lam_im, ld)]
    cols = [_pair_rows(a.astype(F32))[:, :, :, :, None] for a in (lam_re, lam_im, ld)]

    def bt(b):
        return (b.astype(F32).reshape(depth, 2, pairs, 2, n, SSM_GROUP).transpose(0, 2, 1, 5, 3, 4)
                .reshape(depth, pairs, 2, SSM_GROUP, 2 * n))

    def ct(c):
        c = (c.astype(F32).reshape(depth, 2, pairs, 2, SSM_GROUP, n).transpose(0, 2, 1, 3, 5, 4)
             .reshape(depth, pairs, 2, 2 * n, SSM_GROUP))
        return jnp.tile(c, (1, 1, 1, 1, t))

    row_spec = pl.BlockSpec((None, None, 2, 1, 2 * n), lambda l, j: (l, j, 0, 0, 0))
    col_spec = pl.BlockSpec((None, None, 2, 2 * n, 1), lambda l, j: (l, j, 0, 0, 0))
    bt_spec = pl.BlockSpec((None, None, 2, SSM_GROUP, 2 * n), lambda l, j: (l, j, 0, 0, 0))
    ct_spec = pl.BlockSpec((None, None, 2, 2 * n, pw), lambda l, j: (l, j, 0, 0, 0))
    return pl.pallas_call(
        _ssm_params_kernel,
        grid=(depth, pairs),
        in_specs=[row_spec] * 3 + [col_spec] * 3 + [bt_spec] * 2 + [ct_spec] * 2,
        out_specs=[
            pl.BlockSpec((None, None, 2 * pw, 4 * pw), lambda l, j: (l, j, 0, 0)),
            pl.BlockSpec((None, None, 8 * n, 2 * pw), lambda l, j: (l, j, 0, 0)),
            pl.BlockSpec((None, 4, 2 * n), lambda l, j: (l, 0, j)),
        ],
        out_shape=[
            jax.ShapeDtypeStruct((depth, pairs, 2 * pw, 4 * pw), BF16),
            jax.ShapeDtypeStruct((depth, pairs, 8 * n, 2 * pw), BF16),
            jax.ShapeDtypeStruct((depth, 4, groups * n), F32),
        ],
        compiler_params=_params(("parallel", "parallel")),
        name="ssm_parameters",
    )(*rows, *cols, bt(b_re), bt(b_im), ct(c_re), ct(c_im))


def _chunk_rows(u_ref):
    return jnp.concatenate([u_ref[t] for t in range(u_ref.shape[0])], axis=-1)


def _scan_chunk_states(s_ref, a_ref, h_ref, *, batch, lat_tiles, ctx_tiles):
    w = s_ref.shape[-1]
    per_tile = SUBLANES // batch
    row_grp = lax.broadcasted_iota(jnp.int32, (SUBLANES, w), 0) // batch
    zero = jnp.zeros((SUBLANES, w), F32)

    def make_tile(k, order, shift):
        ar = a_ref[k:k + 1, :]
        ai = a_ref[k + 1:k + 2, :]

        def tile(ti, carry):
            cur_re, cur_im = carry
            rows = pl.ds(pl.multiple_of(ti * SUBLANES, SUBLANES), SUBLANES)
            s_re = s_ref[k, rows, :]
            s_im = s_ref[k + 1, rows, :]
            h_re, h_im = zero, zero
            for pos in order:
                sel = row_grp == pos
                h_re = jnp.where(sel, cur_re, h_re)
                h_im = jnp.where(sel, cur_im, h_im)
                nxt_re = ar * cur_re - ai * cur_im + s_re
                nxt_im = ar * cur_im + ai * cur_re + s_im
                if batch == SUBLANES:
                    cur_re, cur_im = nxt_re, nxt_im
                else:
                    cur_re = pltpu.roll(nxt_re, shift, 0)
                    cur_im = pltpu.roll(nxt_im, shift, 0)
            h_ref[k, rows, :] = h_re
            h_ref[k + 1, rows, :] = h_im
            return cur_re, cur_im

        return tile

    fwd = make_tile(0, range(per_tile), batch)
    bwd = make_tile(2, range(per_tile - 1, -1, -1), SUBLANES - batch)

    def both(first_tile, n_tiles):
        def step(i, carry):
            return fwd(first_tile + i, carry[0]), bwd(first_tile + n_tiles - 1 - i, carry[1])
        return step

    carry = lax.fori_loop(0, ctx_tiles, both(lat_tiles, ctx_tiles), ((zero, zero), (zero, zero)))
    lax.fori_loop(0, lat_tiles, both(0, lat_tiles), carry)


def _ssm_core_kernel(u_ref, r1_ref, v_ref, a_ref, d_ref, y_ref, s_scr, h_scr, *, batch, lat_tiles, ctx_tiles):
    n_t = u_ref.shape[0]
    kw = n_t * LANES
    u = _chunk_rows(u_ref)
    r = jnp.dot(u, r1_ref[...], preferred_element_type=F32)
    for t in range(n_t):
        y_ref[t] = r[:, LANES * t:LANES * (t + 1)]
    for k in range(4):
        s_scr[k] = r[:, kw + LANES * k:kw + LANES * (k + 1)]
    _scan_chunk_states(s_scr, a_ref, h_scr, batch=batch, lat_tiles=lat_tiles, ctx_tiles=ctx_tiles)
    h = jnp.concatenate([h_scr[k] for k in range(4)], axis=-1).astype(BF16)
    y = jnp.dot(h, v_ref[...], preferred_element_type=F32) + d_ref[...] * u.astype(F32)
    for t in range(n_t):
        y_ref[t] += y[:, LANES * t:LANES * (t + 1)]


def _ssm_core(u, r1, v, a16, dsk, l, dims):
    pairs, n_t, nc, _ = u.shape
    kw = n_t * LANES
    batch = dims["batch"]
    lat_tiles = dims["seq"] // SSM_CHUNK * batch // SUBLANES
    ctx_tiles = dims["ctx"] // SSM_CHUNK * batch // SUBLANES
    return pl.pallas_call(
        functools.partial(_ssm_core_kernel, batch=batch, lat_tiles=lat_tiles, ctx_tiles=ctx_tiles),
        grid=(pairs,),
        in_specs=[
            pl.BlockSpec((None, n_t, nc, LANES), lambda j: (j, 0, 0, 0)),
            pl.BlockSpec((None, None, kw, 2 * kw), lambda j: (l, j, 0, 0)),
            pl.BlockSpec((None, None, 4 * LANES, kw), lambda j: (l, j, 0, 0)),
            pl.BlockSpec((None, 4, LANES), lambda j: (l, 0, j)),
            pl.BlockSpec((None, None, 1, kw), lambda j: (l, j, 0, 0)),
        ],
        out_specs=pl.BlockSpec((None, n_t, nc, LANES), lambda j: (j, 0, 0, 0)),
        out_shape=jax.ShapeDtypeStruct((pairs, n_t, nc, LANES), F32),
        scratch_shapes=[pltpu.VMEM((4, nc, LANES), F32), pltpu.VMEM((4, nc, LANES), F32)],
        compiler_params=_params(("parallel",)),
        name="ssm_core",
    )(u, r1, v, a16, dsk)


GROUPS_PER_TILE = LANES // SSM_GROUP


def _lane_slot(rows):
    return lax.broadcasted_iota(jnp.int32, (rows, LANES), 1) // SSM_GROUP


def _block_transposes(sets, slot):
    sets = [list(t) for t in sets]
    d = GROUPS_PER_TILE // 2
    while d:
        low = (slot & d) == 0
        for t in sets:
            for a in range(GROUPS_PER_TILE):
                if not a & d:
                    lo, hi = t[a], t[a + d]
                    t[a] = jnp.where(low, lo, pltpu.roll(hi, d * SSM_GROUP, 1))
                    t[a + d] = jnp.where(low, pltpu.roll(lo, LANES - d * SSM_GROUP, 1), hi)
        d //= 2
    return sets


def _chunk_tiling(dims):
    ncl, ncc = dims["seq"] // SSM_CHUNK, dims["ctx"] // SSM_CHUNK
    ct = _pick([ncl, ncc], [16, 8, 4, 2, 1])
    nlt, nct = ncl // ct, ncc // ct

    def token_block(k, b):
        return jnp.where(k < nlt, b * nlt + k, dims["batch"] * nlt + b * nct + (k - nlt))

    return ct, nlt + nct, token_block


def _ssm_pack_kernel(*refs, batch, ct):
    x_refs, o_ref, scr = refs[:batch], refs[batch], refs[batch + 1]
    slot = _lane_slot(ct)
    halves = SSM_CHUNK // GROUPS_PER_TILE
    for b in range(batch):
        where = [(j, half) for j in range(x_refs[b].shape[0]) for half in range(halves)]
        sets = [[x_refs[b][j, pl.ds(GROUPS_PER_TILE * half + k, ct, stride=SSM_CHUNK), :]
                 for k in range(GROUPS_PER_TILE)] for j, half in where]
        for (j, half), outs in zip(where, _block_transposes(sets, slot)):
            for gg, out in enumerate(outs):
                g = GROUPS_PER_TILE * j + gg
                scr[g // 2, (g % 2) * halves + half, pl.ds(b, ct, stride=batch), :] = out
    o_ref[...] = scr[...].astype(o_ref.dtype)


def _ssm_pack(u32, dims):
    batch, pairs = dims["batch"], dims["pairs"]
    ct, n_tiles, token_block = _chunk_tiling(dims)
    n_in, nt, _ = u32.shape
    n_t = 2 * SSM_CHUNK * SSM_GROUP // LANES
    return pl.pallas_call(
        functools.partial(_ssm_pack_kernel, batch=batch, ct=ct),
        grid=(n_tiles,),
        in_specs=[pl.BlockSpec((n_in, ct * SSM_CHUNK, LANES),
                               functools.partial(lambda k, b: (0, token_block(k, b), 0), b=b))
                  for b in range(batch)],
        out_specs=pl.BlockSpec((pairs, n_t, ct * batch, LANES), lambda k: (0, 0, k, 0)),
        out_shape=jax.ShapeDtypeStruct((pairs, n_t, nt // SSM_CHUNK, LANES), BF16),
        scratch_shapes=[pltpu.VMEM((pairs, n_t, ct * batch, LANES), F32)],
        compiler_params=_params(("parallel",)),
        name="ssm_pack",
    )(*([u32] * batch))


def _ssm_unpack_kernel(y_ref, o_ref, *, batch, ct):
    slot = _lane_slot(ct)
    halves = SSM_CHUNK // GROUPS_PER_TILE
    tokens = ct * SSM_CHUNK
    where = [(j, half) for j in range(o_ref.shape[0]) for half in range(halves)]
    for b in range(batch):
        sets = []
        for j, half in where:
            groups = [GROUPS_PER_TILE * j + gg for gg in range(GROUPS_PER_TILE)]
            sets.append([y_ref[g // 2, (g % 2) * halves + half, pl.ds(b, ct, stride=batch), :] for g in groups])
        for (j, half), outs in zip(where, _block_transposes(sets, slot)):
            for k, out in enumerate(outs):
                o_ref[j, pl.ds(tokens * b + GROUPS_PER_TILE * half + k, ct, stride=SSM_CHUNK), :] = out


def _unpacked_block(dims, tm):
    ct, _, _ = _chunk_tiling(dims)
    assert tm == ct * SSM_CHUNK
    batch = dims["batch"]
    nlt, nct = dims["seq"] // tm, dims["ctx"] // tm

    def block(i):
        c = i - batch * nlt
        return jnp.where(i < batch * nlt, (i % nlt) * batch + i // nlt, (nlt + c % nct) * batch + c // nct)

    return block


def _ssm_unpack(y, dims):
    batch = dims["batch"]
    pairs, n_t, nc, _ = y.shape
    ct, n_tiles, _ = _chunk_tiling(dims)
    n_out = dims["ssm_w"] // LANES
    return pl.pallas_call(
        functools.partial(_ssm_unpack_kernel, batch=batch, ct=ct),
        grid=(n_tiles,),
        in_specs=[pl.BlockSpec((pairs, n_t, ct * batch, LANES), lambda k: (0, 0, k, 0))],
        out_specs=pl.BlockSpec((n_out, batch * ct * SSM_CHUNK, LANES), lambda k: (0, k, 0)),
        out_shape=jax.ShapeDtypeStruct((n_out, nc * SSM_CHUNK, LANES), F32),
        compiler_params=_params(("parallel",)),
        name="ssm_unpack",
    )(y)


def _merge_kernel(pb_ref, pc_ref, pv_ref, hpc_ref, hpv_ref, hnc_ref, hnv_ref, cw_ref, ys_ref, at_ref,
                  g_ref, wc_ref, wg_ref, wa_ref, wo_ref, x_ref, gate_ref, gpost_ref, gn_ref, shn_ref, scn_ref,
                  o_ref, h_ref, m_scr, *, tm, tn, seq, ctx, n_lat):
    d = o_ref.shape[1]
    r0 = pl.program_id(0) * tm
    period = jnp.where(r0 >= n_lat, ctx, seq)
    starts = r0 % period == 0
    ends = (r0 + tm) % period == 0
    w = pc_ref[...].astype(F32) * pv_ref[...].astype(F32)
    prev = (hpc_ref[...].astype(F32) * hpv_ref[...].astype(F32))[HALO_ROWS - 1:HALO_ROWS]
    nxt = (hnc_ref[...].astype(F32) * hnv_ref[...].astype(F32))[0:1]
    prev = jnp.where(starts, 0.0, prev)
    nxt = jnp.where(ends, 0.0, nxt)
    row = lax.broadcasted_iota(jnp.int32, w.shape, 0)
    up = jnp.where(row == 0, prev, pltpu.roll(w, 1, 0))
    dn = jnp.where(row == tm - 1, nxt, pltpu.roll(w, tm - 1, 0))
    cw = cw_ref[...]
    cf = (pb_ref[...].astype(F32) * (cw[0:1] * up + cw[1:2] * w + cw[2:3] * dn)).astype(BF16)
    yg = jax.nn.gelu(_chunk_rows(ys_ref)).astype(BF16)
    at = at_ref[...]
    for c in range(d // tn):
        cols = slice(tn * c, tn * (c + 1))
        gate = lambda k: jax.nn.sigmoid(g_ref[:, k * d + tn * c:k * d + tn * (c + 1)].astype(F32))
        ya = jnp.dot(cf, wc_ref[:, cols], preferred_element_type=F32)
        yc = jnp.dot(at, wa_ref[:, cols], preferred_element_type=F32)
        ys = jnp.dot(yg, wg_ref[:, cols], preferred_element_type=F32) * jax.nn.sigmoid(
            jnp.dot(yg, wg_ref[:, d + tn * c:d + tn * (c + 1)], preferred_element_type=F32))
        m_scr[:, cols] = (gate(0) * ya + gate(1) * ys + gate(2) * yc).astype(m_scr.dtype)
    r = jnp.dot(m_scr[...], wo_ref[...], preferred_element_type=F32)
    x = x_ref[...] + gate_ref[...] * _rms(r, gpost_ref[...])
    o_ref[...] = x
    h_ref[...] = _norm_mod(x, gn_ref[...], shn_ref[...], scn_ref[...])


def _merge(main, gates, ys, attn, conv_w, w_conv_out, w_glu, w_attn_out, w_out, xt, modr, g_post, g_pre_mlp,
           l, dims, rows):
    d = dims["d"]
    cw_, sw, aw = dims["conv_w"], dims["ssm_w"], dims["attn_w"]
    tm = _pick([dims["seq"], dims["ctx"]], [512, 256, 128, 64, 32])
    tn = _pick([d], [512, 256, 128])
    n_halo = main.shape[0] // HALO_ROWS
    per = tm // HALO_ROWS
    prev_blk = lambda i: jnp.maximum(i * per - 1, 0)
    next_blk = lambda i: jnp.minimum((i + 1) * per, n_halo - 1)
    grp = _row_group(dims, tm)
    ys_block = _unpacked_block(dims, tm)
    mod = lambda k: pl.BlockSpec((None, None, 1, d), lambda i: (l, grp(i), 0, k))
    vec = pl.BlockSpec((None, 1, d), lambda i: (l, 0, 0))
    row_tile = pl.BlockSpec((tm, d), lambda i: (i, 0))
    resident = lambda shape: pl.BlockSpec((None,) + shape, lambda i: (l, 0, 0), pipeline_mode=pl.Buffered(1))
    return pl.pallas_call(
        functools.partial(_merge_kernel, tm=tm, tn=tn, seq=dims["seq"], ctx=dims["ctx"], n_lat=dims["n_lat"]),
        grid=(rows // tm,),
        in_specs=[
            pl.BlockSpec((tm, cw_), lambda i: (i, 0)),
            pl.BlockSpec((tm, cw_), lambda i: (i, 1)),
            pl.BlockSpec((tm, cw_), lambda i: (i, 2)),
            pl.BlockSpec((HALO_ROWS, cw_), lambda i: (prev_blk(i), 1)),
            pl.BlockSpec((HALO_ROWS, cw_), lambda i: (prev_blk(i), 2)),
            pl.BlockSpec((HALO_ROWS, cw_), lambda i: (next_blk(i), 1)),
            pl.BlockSpec((HALO_ROWS, cw_), lambda i: (next_blk(i), 2)),
            resident((CONV_K, cw_)),
            pl.BlockSpec((sw // LANES, tm, LANES), lambda i: (0, ys_block(i), 0)),
            pl.BlockSpec((tm, aw), lambda i: (i, 0)),
            pl.BlockSpec((tm, 3 * d), lambda i: (i, 0)),
            resident((cw_, d)),
            resident((sw, 2 * d)),
            resident((aw, d)),
            resident((d, d)),
            row_tile, mod(2), vec, vec, mod(3), mod(4),
        ],
        out_specs=[row_tile, row_tile],
        out_shape=[jax.ShapeDtypeStruct(xt.shape, F32), jax.ShapeDtypeStruct((rows, d), BF16)],
        scratch_shapes=[pltpu.VMEM((tm, d), BF16)],
        input_output_aliases={15: 0},
        compiler_params=_params(("parallel",)),
        name="mixer_output",
    )(main, main, main, main, main, main, main, conv_w, ys, attn, gates, w_conv_out, w_glu, w_attn_out,
      w_out, xt, modr, g_post, g_pre_mlp, modr, modr)


def _mlp_kernel(x_ref, h_ref, wu_ref, wd_ref, gate_ref, gpost_ref, *rest, with_next):
    j = pl.program_id(1)
    if with_next:
        gn_ref, shn_ref, scn_ref, o_ref, hn_ref, acc_scr = rest
    else:
        o_ref, acc_scr = rest

    last = pl.num_programs(1) - 1

    def partial_out(rows):
        u = jnp.maximum(jnp.dot(h_ref[rows, :], wu_ref[...], preferred_element_type=F32), 0.0)
        return jnp.dot((u * u).astype(BF16), wd_ref[...], preferred_element_type=F32)

    every = slice(None)

    @pl.when(j == 0)
    def _():
        acc_scr[...] = partial_out(every)

    @pl.when(jnp.logical_and(j > 0, j < last))
    def _():
        acc_scr[...] += partial_out(every)

    @pl.when(j == last)
    def _():
        part = acc_scr.shape[0] // MLP_TAIL_PARTS
        for k in range(MLP_TAIL_PARTS):
            rows = slice(part * k, part * (k + 1))
            r = acc_scr[rows, :] + partial_out(rows)
            x = x_ref[rows, :] + gate_ref[...] * _rms(r, gpost_ref[...])
            o_ref[rows, :] = x
            if with_next:
                hn_ref[rows, :] = _norm_mod(x, gn_ref[...], shn_ref[...], scn_ref[...])


def _mlp(xt, h, g_post, modr, w_up, w_down, l, dims, rows, g_next=None):
    d = xt.shape[1]
    d_ff = w_up.shape[-1]
    tm = _pick([dims["seq"], dims["n_ctx"]], [512, 256, 128])
    tf = _pick([d_ff], [1024, 512, 256, 128])
    grp = _row_group(dims, tm)
    mod = lambda layer, k: pl.BlockSpec((None, None, 1, d), lambda i, j: (layer, grp(i), 0, k))
    vec = lambda layer: pl.BlockSpec((None, 1, d), lambda i, j: (layer, 0, 0))
    row_tile = pl.BlockSpec((tm, d), lambda i, j: (i, 0))
    in_specs = [
        row_tile, row_tile,
        pl.BlockSpec((None, d, tf), lambda i, j: (l, 0, j)),
        pl.BlockSpec((None, tf, d), lambda i, j: (l, j, 0)),
        mod(l, 5), vec(l),
    ]
    args = [xt, h, w_up, w_down, modr, g_post]
    out_specs, out_shape = [row_tile], [jax.ShapeDtypeStruct((rows, d), F32)]
    if g_next is not None:
        assert rows == xt.shape[0]
        in_specs += [vec(l + 1), mod(l + 1, 0), mod(l + 1, 1)]
        args += [g_next, modr, modr]
        out_specs.append(row_tile)
        out_shape.append(jax.ShapeDtypeStruct((rows, d), BF16))
    return pl.pallas_call(
        functools.partial(_mlp_kernel, with_next=g_next is not None),
        grid=(rows // tm, d_ff // tf),
        in_specs=in_specs,
        out_specs=out_specs,
        out_shape=out_shape,
        scratch_shapes=[pltpu.VMEM((tm, d), F32)],
        input_output_aliases={0: 0} if g_next is not None else {},
        compiler_params=_params(("parallel", "arbitrary")),
        name="mlp",
    )(*args)


def kernel(x, c, ctx, c_ctx, w_mod, b_mod, g_pre_mix, g_post_mix, g_pre_mlp, g_post_mlp, w_in, conv_w,
           w_conv_out, ssm_lam_re, ssm_lam_im, ssm_log_dt, ssm_b_re, ssm_b_im, ssm_c_re, ssm_c_im, ssm_d,
           w_glu, q_gain, k_gain, w_attn_out, w_out, w_up, w_down):
    batch, seq, d = x.shape
    ctx_len = ctx.shape[1]
    depth = w_in.shape[0]
    in_w = w_in.shape[-1]
    conv_width = conv_w.shape[-1]
    ssm_w = ssm_d.shape[-1]
    attn_w = w_attn_out.shape[1]
    gate_w = 3 * d
    kv_w = (in_w - 3 * conv_width - ssm_w - attn_w - gate_w) // 2
    groups = ssm_lam_re.shape[2]
    assert batch + 1 <= MOD_ROWS and SUBLANES % batch == 0
    assert kv_w == N_KV_HEADS * HEAD_DIM and ssm_lam_re.shape[3] == SSM_STATE
    assert ssm_w == groups * SSM_GROUP and groups % 2 == 0
    assert seq % GRID_W == 0 and (batch * seq) % ctx_len == 0 and ctx_len % LANES == 0
    a_w = 3 * conv_width + ssm_w
    dims = dict(
        batch=batch, seq=seq, ctx=ctx_len, d=d, n_lat=batch * seq, n_ctx=batch * ctx_len,
        conv_w=conv_width, ssm_w=ssm_w, attn_w=attn_w, kv_w=kv_w, gate_w=gate_w, pairs=groups // 2,
        o_q=gate_w, o_k=gate_w + attn_w, o_v=gate_w + attn_w + kv_w,
    )
    n_lat, nt = dims["n_lat"], dims["n_lat"] + dims["n_ctx"]

    cc = jnp.concatenate([c, c_ctx[None, :], jnp.zeros((MOD_ROWS - batch - 1, d), F32)], axis=0)
    mod = _modulation(cc, w_mod, b_mod)
    modr = mod.reshape(depth, MOD_ROWS, 1, N_MOD * d)

    row_vec = lambda a: a.reshape(depth, 1, a.shape[-1])
    g_pre_mix, g_post_mix, g_pre_mlp, g_post_mlp = map(row_vec, (g_pre_mix, g_post_mix, g_pre_mlp, g_post_mlp))
    perm = _head_permutation()
    q_gain, k_gain = row_vec(q_gain[:, perm]), row_vec(k_gain[:, perm])
    w_in_a = w_in[:, :, :a_w].astype(BF16)
    w_qk = w_in[:, :, a_w:a_w + attn_w + kv_w].reshape(depth, d, -1, HEAD_DIM)[..., perm]
    w_in_b = jnp.concatenate([w_in[:, :, in_w - gate_w:], w_qk.reshape(depth, d, attn_w + kv_w),
                              w_in[:, :, a_w + attn_w + kv_w:in_w - gate_w]], axis=-1).astype(BF16)
    w_conv_out, w_glu, w_attn_out, w_out, w_up, w_down = (
        w.astype(BF16) for w in (w_conv_out, w_glu, w_attn_out, w_out, w_up, w_down))
    tn_a = _pick([a_w], [1280, 768, 512, 256, 128])
    tn_b = _pick([w_in_b.shape[-1]], [1536, 768, 512, 256, 128])

    cos_t, sin_t = _rope_tables(seq, _qk_tile(dims))
    r1, v_mat, a16 = _ssm_parameters(ssm_lam_re, ssm_lam_im, ssm_log_dt, ssm_b_re, ssm_b_im, ssm_c_re, ssm_c_im)
    dsk = jnp.broadcast_to(
        ssm_d.astype(F32).reshape(depth, groups // 2, 2, 1, SSM_GROUP),
        (depth, groups // 2, 2, SSM_CHUNK, SSM_GROUP)).reshape(depth, groups // 2, 1, 2 * SSM_CHUNK * SSM_GROUP)

    xt, h = _prenorm(x.reshape(n_lat, d), ctx.reshape(dims["n_ctx"], d), g_pre_mix, modr, dims)
    for l in range(depth):
        need_ctx = l < depth - 1
        rows = nt if need_ctx else n_lat
        proj_a, u32 = _projection(h, w_in_a, l, dims, tn_a, f32_cols=(3 * conv_width, ssm_w))
        (proj_b,) = _projection(h, w_in_b, l, dims, tn_b)
        u = _ssm_pack(u32, dims)
        ys = _ssm_unpack(_ssm_core(u, r1, v_mat, a16, dsk, l, dims), dims)
        q, k = _qk_prepare(proj_b, cos_t, sin_t, q_gain, k_gain, l, dims)
        attn = _attention(q, k, proj_b, None, dims)
        if need_ctx:
            attn = _attention(q, k, proj_b, attn, dims)
        xt, h = _merge(proj_a, proj_b, ys, attn, conv_w, w_conv_out, w_glu, w_attn_out, w_out, xt, modr,
                       g_post_mix, g_pre_mlp, l, dims, rows)
        if need_ctx:
            xt, h = _mlp(xt, h, g_post_mlp, modr, w_up, w_down, l, dims, rows, g_next=g_pre_mix)
        else:
            (xt,) = _mlp(xt, h, g_post_mlp, modr, w_up, w_down, l, dims, rows)
    return xt.reshape(batch, seq, d)
```

```python
import functools
import math

import jax
import jax.numpy as jnp
from jax import lax
from jax.experimental import pallas as pl
from jax.experimental.pallas import tpu as pltpu

F32 = jnp.float32
BF16 = jnp.bfloat16

NORM_EPS = 1e-6
N_MOD = 6
HEAD_DIM = 128
N_KV_HEADS = 2
GRID_W = 64
ROPE_BASE = 10000.0
CONV_K = 3
SSM_GROUP = 16
SSM_STATE = 64
SSM_RE_MAX = -1e-4
SSM_CHUNK = 16
SUBLANES = 8
LANES = 128
HALO_ROWS = 16
MXU_TILE = 256
MAX_KEY_CHUNK = 4352
MLP_TAIL_PARTS = 2
MOD_ROWS = 8
VMEM_LIMIT = 56 * 2**20


def _params(sem, vmem=VMEM_LIMIT):
    return pltpu.CompilerParams(dimension_semantics=sem, vmem_limit_bytes=vmem)


def _pick(dims, cands):
    for c in cands:
        if all(d % c == 0 for d in dims):
            return c
    raise ValueError(f"no tile in {cands} divides {dims}")


def _rms(x, g):
    ms = jnp.mean(x * x, axis=-1, keepdims=True)
    return x * lax.rsqrt(ms + NORM_EPS) * g


def _mod_kernel(c_ref, w_ref, b_ref, o_ref):
    c = c_ref[...]
    s = (c * jax.nn.sigmoid(c)).astype(BF16)
    o_ref[...] = jnp.dot(s, w_ref[...].astype(BF16), preferred_element_type=F32) + b_ref[...]


def _modulation(cc, w_mod, b_mod):
    depth, d, nout = w_mod.shape
    tn = _pick([nout], [1024, 512, 256, 128])
    return pl.pallas_call(
        _mod_kernel,
        grid=(depth, nout // tn),
        in_specs=[
            pl.BlockSpec((MOD_ROWS, d), lambda l, j: (0, 0)),
            pl.BlockSpec((None, d, tn), lambda l, j: (l, 0, j)),
            pl.BlockSpec((None, 1, tn), lambda l, j: (l, 0, j)),
        ],
        out_specs=pl.BlockSpec((None, MOD_ROWS, tn), lambda l, j: (l, 0, j)),
        out_shape=jax.ShapeDtypeStruct((depth, MOD_ROWS, nout), F32),
        compiler_params=_params(("parallel", "parallel")),
        name="modulation",
    )(cc, w_mod, b_mod.reshape(depth, 1, nout))


def _norm_mod(x, g, shift, scale):
    return (_rms(x, g) * (1.0 + scale) + shift).astype(BF16)


def _prenorm_kernel(x_ref, c_ref, g_ref, sh_ref, sc_ref, xt_ref, h_ref, *, lat_tiles):
    x = jnp.where(pl.program_id(0) < lat_tiles, x_ref[...], c_ref[...])
    xt_ref[...] = x
    h_ref[...] = _norm_mod(x, g_ref[...], sh_ref[...], sc_ref[...])


def _row_group(dims, tm):
    return lambda i: jnp.minimum((i * tm) // dims["seq"], dims["batch"])


def _prenorm(x, ctx, gain, modr, dims):
    d = x.shape[1]
    nt = x.shape[0] + ctx.shape[0]
    tm = _pick([dims["seq"], dims["n_ctx"]], [512, 256, 128])
    lat_tiles = x.shape[0] // tm
    grp = _row_group(dims, tm)
    row_tile = pl.BlockSpec((tm, d), lambda i: (i, 0))
    return pl.pallas_call(
        functools.partial(_prenorm_kernel, lat_tiles=lat_tiles),
        grid=(nt // tm,),
        in_specs=[
            pl.BlockSpec((tm, d), lambda i: (jnp.minimum(i, lat_tiles - 1), 0)),
            pl.BlockSpec((tm, d), lambda i: (jnp.maximum(i - lat_tiles, 0), 0)),
            pl.BlockSpec((None, 1, d), lambda i: (0, 0, 0)),
            pl.BlockSpec((None, None, 1, d), lambda i: (0, grp(i), 0, 0)),
            pl.BlockSpec((None, None, 1, d), lambda i: (0, grp(i), 0, 1)),
        ],
        out_specs=[row_tile, row_tile],
        out_shape=[jax.ShapeDtypeStruct((nt, d), F32), jax.ShapeDtypeStruct((nt, d), BF16)],
        compiler_params=_params(("parallel",)),
        name="prenorm",
    )(x, ctx, gain, modr, modr)


def _proj_kernel(h_ref, w_ref, o_ref, *u_refs, u_tile, u_off):
    r = jnp.dot(h_ref[...], w_ref[...], preferred_element_type=F32)
    o_ref[...] = r.astype(o_ref.dtype)
    if u_refs:
        (ou_ref,) = u_refs

        @pl.when(pl.program_id(1) == u_tile)
        def _():
            for t in range(ou_ref.shape[0]):
                ou_ref[t] = r[:, u_off + LANES * t:u_off + LANES * (t + 1)]


def _projection(h, w, l, dims, tn, f32_cols=None):
    nt, d = h.shape
    n_out = w.shape[-1]
    tm = _pick([dims["seq"], dims["n_ctx"]], [1024, 512, 256, 128])
    assert n_out % tn == 0
    out_specs = [pl.BlockSpec((tm, tn), lambda i, j: (i, j))]
    out_shape = [jax.ShapeDtypeStruct((nt, n_out), BF16)]
    u_tile = u_off = 0
    if f32_cols is not None:
        start, width = f32_cols
        u_tile, u_off = start // tn, start % tn
        assert u_off + width <= tn and u_off % LANES == 0 and width % LANES == 0
        out_specs.append(pl.BlockSpec((width // LANES, tm, LANES), lambda i, j: (0, i, 0)))
        out_shape.append(jax.ShapeDtypeStruct((width // LANES, nt, LANES), F32))
    return pl.pallas_call(
        functools.partial(_proj_kernel, u_tile=u_tile, u_off=u_off),
        grid=(nt // tm, n_out // tn),
        in_specs=[
            pl.BlockSpec((tm, d), lambda i, j: (i, 0)),
            pl.BlockSpec((None, d, tn), lambda i, j: (l, 0, j)),
        ],
        out_specs=out_specs,
        out_shape=out_shape,
        compiler_params=_params(("parallel", "arbitrary")),
        name="in_projection",
    )(h, w)


def _head_permutation():
    quarter = HEAD_DIM // 4
    return jnp.concatenate([jnp.arange(quarter * k, quarter * (k + 1)) for k in (0, 2, 1, 3)])


def _qk_kernel(xq_ref, xk_ref, cos_ref, sin_ref, qg_ref, kg_ref, q_ref, k_ref, *, n_q, n_kv, scale):
    cos = cos_ref[...]
    sin = sin_ref[...]
    mean_cols = jnp.full((HEAD_DIM, HEAD_DIM), 1.0 / HEAD_DIM, BF16)
    for h in range(n_q + n_kv):
        if h < n_q:
            xh = xq_ref[:, HEAD_DIM * h:HEAD_DIM * (h + 1)].astype(F32)
        else:
            xh = xk_ref[:, HEAD_DIM * (h - n_q):HEAD_DIM * (h - n_q + 1)].astype(F32)
        gain = qg_ref[...] * scale if h < n_q else kg_ref[...]
        sq = xh * xh
        hi = sq.astype(BF16)
        lo = (sq - hi.astype(F32)).astype(BF16)
        ms = (jnp.dot(hi, mean_cols, preferred_element_type=F32)
              + jnp.dot(lo, mean_cols, preferred_element_type=F32))
        y = xh * lax.rsqrt(ms + NORM_EPS) * gain
        r = (y * cos + pltpu.roll(y, HEAD_DIM // 2, 1) * sin).astype(BF16)
        if h < n_q:
            q_ref[:, HEAD_DIM * h:HEAD_DIM * (h + 1)] = r
        else:
            k_ref[:, HEAD_DIM * (h - n_q):HEAD_DIM * (h - n_q + 1)] = r


def _qk_tile(dims):
    return _pick([dims["seq"], dims["n_ctx"]], [512, 256, 128, 64, 32])


def _qk_prepare(proj, cos_t, sin_t, q_gain, k_gain, l, dims):
    nt = proj.shape[0]
    attn_w, kv_w = dims["attn_w"], dims["kv_w"]
    tm = _qk_tile(dims)
    assert dims["o_q"] % attn_w == 0 and dims["o_k"] % kv_w == 0
    n_lat = dims["n_lat"] // tm
    n_seq = dims["seq"] // tm
    tab = lambda i: (jnp.where(i < n_lat, i % n_seq, n_seq), 0)
    return pl.pallas_call(
        functools.partial(_qk_kernel, n_q=attn_w // HEAD_DIM, n_kv=kv_w // HEAD_DIM,
                          scale=HEAD_DIM ** -0.5 * math.log2(math.e)),
        grid=(nt // tm,),
        in_specs=[
            pl.BlockSpec((tm, attn_w), lambda i: (i, dims["o_q"] // attn_w)),
            pl.BlockSpec((tm, kv_w), lambda i: (i, dims["o_k"] // kv_w)),
            pl.BlockSpec((tm, HEAD_DIM), tab),
            pl.BlockSpec((tm, HEAD_DIM), tab),
            pl.BlockSpec((None, 1, HEAD_DIM), lambda i: (l, 0, 0)),
            pl.BlockSpec((None, 1, HEAD_DIM), lambda i: (l, 0, 0)),
        ],
        out_specs=[
            pl.BlockSpec((tm, attn_w), lambda i: (i, 0)),
            pl.BlockSpec((tm, kv_w), lambda i: (i, 0)),
        ],
        out_shape=[
            jax.ShapeDtypeStruct((nt, attn_w), BF16),
            jax.ShapeDtypeStruct((nt, kv_w), BF16),
        ],
        compiler_params=_params(("parallel",)),
        name="qk_prepare",
    )(proj, proj, cos_t, sin_t, q_gain, k_gain)


def _rope_tables(seq, pad_rows):
    rows = seq // GRID_W
    row = jnp.repeat(jnp.arange(rows), GRID_W)
    col = jnp.tile(jnp.arange(GRID_W), rows)
    half = HEAD_DIM // 4
    inv_freq = ROPE_BASE ** (-jnp.arange(half, dtype=F32) / half)
    ang_r = row.astype(F32)[:, None] * inv_freq
    ang_c = col.astype(F32)[:, None] * inv_freq
    cr, sr, cc, sc = jnp.cos(ang_r), jnp.sin(ang_r), jnp.cos(ang_c), jnp.sin(ang_c)
    cos_t = jnp.concatenate([cr, cc, cr, cc], axis=-1)
    sin_t = jnp.concatenate([-sr, -sc, sr, sc], axis=-1)
    cos_t = jnp.concatenate([cos_t, jnp.ones((pad_rows, HEAD_DIM), F32)], axis=0)
    sin_t = jnp.concatenate([sin_t, jnp.zeros((pad_rows, HEAD_DIM), F32)], axis=0)
    return cos_t, sin_t


def _key_chunks(n_keys):
    unit = MXU_TILE if n_keys % MXU_TILE == 0 else LANES
    units = n_keys // unit
    n_chunks = -(-n_keys // MAX_KEY_CHUNK)
    sizes = [(units // n_chunks + (c < units % n_chunks)) * unit for c in range(n_chunks)]
    return [(sum(sizes[:c]), sizes[c]) for c in range(n_chunks)]


def _attn_kernel(*refs, tq, chunks, group, n_src):
    q_ref = refs[0]
    k_refs, v_refs = refs[1:1 + 2 * n_src:2], refs[2:2 + 2 * n_src:2]
    o_ref, ke_scr, ve_scr, s_scr, m_scr, acc_scr = refs[-6:]

    @pl.when(pl.program_id(2) == 0)
    def _():
        r0 = 0
        for k_ref, v_ref in zip(k_refs, v_refs):
            n = k_ref.shape[0]
            ke_scr[r0:r0 + n, :] = k_ref[...]
            ve_scr[r0:r0 + n, :HEAD_DIM] = v_ref[...]
            r0 += n
        ve_scr[:, HEAD_DIM:] = jnp.ones((ve_scr.shape[0], HEAD_DIM), BF16)

    q = jnp.concatenate([q_ref[:, HEAD_DIM * g:HEAD_DIM * (g + 1)] for g in range(group)], axis=0)

    one = jnp.minimum(pl.program_id(2) + 1, 1)

    def first(_, carry):
        m = None
        for k0, nk in chunks:
            s = lax.dot_general(q, ke_scr[k0:k0 + nk, :], (((1,), (1,)), ((), ())), preferred_element_type=F32)
            s_scr[:, k0:k0 + nk] = s
            for t in range(nk // LANES):
                slab = s[:, LANES * t:LANES * (t + 1)]
                m = slab if m is None else jnp.maximum(m, slab)
        m_scr[...] = m
        return carry

    lax.fori_loop(0, one, first, 0)
    m_scr[...] = jnp.broadcast_to(jnp.max(m_scr[...], axis=-1, keepdims=True), m_scr.shape)

    def second(_, carry):
        m_rep = m_scr[...]
        acc = None
        for k0, nk in chunks:
            p = jnp.concatenate(
                [jnp.exp2(s_scr[:, k0 + LANES * t:k0 + LANES * (t + 1)] - m_rep).astype(BF16)
                 for t in range(nk // LANES)], axis=1)
            part = jnp.dot(p, ve_scr[k0:k0 + nk, :], preferred_element_type=F32)
            acc = part if acc is None else acc + part
        acc_scr[...] = acc
        return carry

    lax.fori_loop(0, one, second, 0)
    acc = acc_scr[...]
    o = acc[:, :HEAD_DIM] / acc[:, HEAD_DIM:]
    for g in range(group):
        o_ref[:, HEAD_DIM * g:HEAD_DIM * (g + 1)] = o[tq * g:tq * (g + 1)].astype(o_ref.dtype)


def _attention(q, k, main, prev, dims):
    nt = q.shape[0]
    attn_w, seq, ctx, batch = dims["attn_w"], dims["seq"], dims["ctx"], dims["batch"]
    latent = prev is None
    group = attn_w // HEAD_DIM // N_KV_HEADS
    gw = group * HEAD_DIM
    v_blk = dims["o_v"] // HEAD_DIM
    lq = seq if latent else ctx
    tq = _pick([lq], [256, 128, 64, 32, 16])
    n_keys = ctx + seq if latent else ctx
    row0 = 0 if latent else dims["n_lat"] // tq
    ctx0 = dims["n_lat"] // ctx
    rows = group * tq
    in_specs = [
        pl.BlockSpec((tq, gw), lambda b, h, i: (row0 + b * (lq // tq) + i, h)),
        pl.BlockSpec((ctx, HEAD_DIM), lambda b, h, i: (ctx0 + b, h)),
        pl.BlockSpec((ctx, HEAD_DIM), lambda b, h, i: (ctx0 + b, v_blk + h)),
    ]
    args = [q, k, main]
    aliases = {}
    if latent:
        in_specs += [
            pl.BlockSpec((seq, HEAD_DIM), lambda b, h, i: (b, h), pipeline_mode=pl.Buffered(1)),
            pl.BlockSpec((seq, HEAD_DIM), lambda b, h, i: (b, v_blk + h), pipeline_mode=pl.Buffered(1)),
        ]
        args += [k, main]
    n_src = (len(args) - 1) // 2
    if not latent:
        in_specs.append(pl.BlockSpec(memory_space=pl.ANY))
        args.append(prev)
        aliases = {len(args) - 1: 0}
    scratch = [pltpu.VMEM((n_keys, HEAD_DIM), BF16), pltpu.VMEM((n_keys, 2 * HEAD_DIM), BF16),
               pltpu.VMEM((rows, n_keys), F32), pltpu.VMEM((rows, LANES), F32),
               pltpu.VMEM((rows, 2 * HEAD_DIM), F32)]
    return pl.pallas_call(
        functools.partial(_attn_kernel, tq=tq, chunks=_key_chunks(n_keys), group=group, n_src=n_src),
        grid=(batch, N_KV_HEADS, lq // tq),
        in_specs=in_specs,
        out_specs=pl.BlockSpec((tq, gw), lambda b, h, i: (row0 + b * (lq // tq) + i, h)),
        out_shape=jax.ShapeDtypeStruct((nt, attn_w), BF16),
        scratch_shapes=scratch,
        input_output_aliases=aliases,
        compiler_params=_params(("parallel", "parallel", "arbitrary")),
        name="attention_latent" if latent else "attention_context",
    )(*args)


def _ssm_params_kernel(lrr_ref, lir_ref, ldr_ref, lrc_ref, lic_ref, ldc_ref, btr_ref, bti_ref,
                       ctr_ref, cti_ref, r1_ref, v_ref, a_ref):
    t = SSM_CHUNK
    pw = SSM_GROUP * t
    r1_ref[...] = jnp.zeros(r1_ref.shape, r1_ref.dtype)
    v_ref[...] = jnp.zeros(v_ref.shape, v_ref.dtype)
    lane = lax.broadcasted_iota(jnp.int32, (SSM_GROUP, LANES), 1)
    col = lax.broadcasted_iota(jnp.int32, (SSM_GROUP, pw), 1)
    tau_col = (lax.broadcasted_iota(jnp.int32, (LANES, pw), 1) // SSM_GROUP).astype(F32)
    tau_row = lax.broadcasted_iota(jnp.int32, (t, LANES), 0).astype(F32)
    masks = (lane < SSM_STATE, lane >= SSM_STATE)
    toep = []
    for d in range(2):
        lr = jnp.minimum(lrr_ref[d], SSM_RE_MAX)
        li = lir_ref[d]
        dt = jnp.exp(ldr_ref[d])
        mag = jnp.exp(lr * dt)
        ab_re = mag * jnp.cos(li * dt)
        ab_im = mag * jnp.sin(li * dt)
        nr = ab_re - 1.0
        den = lr * lr + li * li
        f_re = (nr * lr + ab_im * li) / den
        f_im = (ab_im * lr - nr * li) / den
        bb_re = f_re * btr_ref[d] - f_im * bti_ref[d]
        bb_im = f_re * bti_ref[d] + f_im * btr_ref[d]
        e16 = jnp.exp(lr * dt * float(t))
        a_ref[2 * d:2 * d + 1, :] = e16 * jnp.cos(li * dt * float(t))
        a_ref[2 * d + 1:2 * d + 2, :] = e16 * jnp.sin(li * dt * float(t))
        tau_s = (t - 1.0) - tau_row if d == 0 else tau_row
        es = jnp.exp(lr * dt * tau_s)
        pw_re = es * jnp.cos(li * dt * tau_s)
        pw_im = es * jnp.sin(li * dt * tau_s)
        for s in range(t):
            ar = pw_re[s:s + 1]
            ai = pw_im[s:s + 1]
            w_re = ar * bb_re - ai * bb_im
            w_im = ar * bb_im + ai * bb_re
            for gi in range(2):
                r0 = gi * pw + SSM_GROUP * s
                c0 = 2 * pw + 2 * LANES * d
                r1_ref[r0:r0 + SSM_GROUP, c0:c0 + LANES] = jnp.where(masks[gi], w_re, 0.0).astype(r1_ref.dtype)
                r1_ref[r0:r0 + SSM_GROUP, c0 + LANES:c0 + 2 * LANES] = (
                    jnp.where(masks[gi], w_im, 0.0).astype(r1_ref.dtype))
        lrc = jnp.minimum(lrc_ref[d], SSM_RE_MAX)
        lic = lic_ref[d]
        dtc = jnp.exp(ldc_ref[d])
        tau_q = tau_col if d == 0 else (t - 1.0) - tau_col
        eq = jnp.exp(lrc * dtc * tau_q)
        p_re = eq * jnp.cos(lic * dtc * tau_q)
        p_im = eq * jnp.sin(lic * dtc * tau_q)
        c_re = ctr_ref[d]
        c_im = cti_ref[d]
        q_re = p_re * c_re - p_im * c_im
        q_im = p_re * c_im + p_im * c_re
        per_group = []
        for gi in range(2):
            k_mat = (jnp.dot(jnp.where(masks[gi], bb_re, 0.0), q_re, precision=lax.Precision.HIGHEST,
                             preferred_element_type=F32)
                     - jnp.dot(jnp.where(masks[gi], bb_im, 0.0), q_im, precision=lax.Precision.HIGHEST,
                               preferred_element_type=F32))
            per_group.append(k_mat)
        toep.append(per_group)
        magc = jnp.exp(lrc * dtc)
        abc_re = magc * jnp.cos(lic * dtc)
        abc_im = magc * jnp.sin(lic * dtc)
        q1_re = abc_re * q_re - abc_im * q_im
        q1_im = abc_re * q_im + abc_im * q_re
        for gi in range(2):
            rows = slice(SSM_STATE * gi, SSM_STATE * (gi + 1))
            r0 = 2 * LANES * d + SSM_STATE * gi
            v_ref[r0:r0 + SSM_STATE, pw * gi:pw * (gi + 1)] = q1_re[rows].astype(v_ref.dtype)
            v_ref[r0 + LANES:r0 + LANES + SSM_STATE, pw * gi:pw * (gi + 1)] = (-q1_im[rows]).astype(v_ref.dtype)
    for gi in range(2):
        k_f, k_b = toep[0][gi], toep[1][gi]
        for s in range(t):
            sh = SSM_GROUP * s
            f = k_f if sh == 0 else jnp.where(col >= sh, pltpu.roll(k_f, sh, 1), 0.0)
            back = SSM_GROUP * (t - 1 - s)
            b = k_b if back == 0 else jnp.where(col < pw - back, pltpu.roll(k_b, pw - back, 1), 0.0)
            r0 = gi * pw + sh
            r1_ref[r0:r0 + SSM_GROUP, pw * gi:pw * (gi + 1)] = (f + b).astype(r1_ref.dtype)


def _pair_rows(a):
    depth, _, groups, n = a.shape
    return a.reshape(depth, 2, groups // 2, 2 * n).transpose(0, 2, 1, 3)


def _ssm_parameters(lam_re, lam_im, log_dt, b_re, b_im, c_re, c_im):
    depth, _, groups, n = lam_re.shape
    pairs = groups // 2
    t = SSM_CHUNK
    pw = SSM_GROUP * t
    ld = jnp.broadcast_to(log_dt[..., None], lam_re.shape)
    rows = [_pair_rows(a.astype(F32))[:, :, :, None, :] for a in (lam_re, lam_im, ld)]
    cols = [_pair_rows(a.astype(F32))[:, :, :, :, None] for a in (lam_re, lam_im, ld)]

    def bt(b):
        return (b.astype(F32).reshape(depth, 2, pairs, 2, n, SSM_GROUP).transpose(0, 2, 1, 5, 3, 4)
                .reshape(depth, pairs, 2, SSM_GROUP, 2 * n))

    def ct(c):
        c = (c.astype(F32).reshape(depth, 2, pairs, 2, SSM_GROUP, n).transpose(0, 2, 1, 3, 5, 4)
             .reshape(depth, pairs, 2, 2 * n, SSM_GROUP))
        return jnp.tile(c, (1, 1, 1, 1, t))

    row_spec = pl.BlockSpec((None, None, 2, 1, 2 * n), lambda l, j: (l, j, 0, 0, 0))
    col_spec = pl.BlockSpec((None, None, 2, 2 * n, 1), lambda l, j: (l, j, 0, 0, 0))
    bt_spec = pl.BlockSpec((None, None, 2, SSM_GROUP, 2 * n), lambda l, j: (l, j, 0, 0, 0))
    ct_spec = pl.BlockSpec((None, None, 2, 2 * n, pw), lambda l, j: (l, j, 0, 0, 0))
    return pl.pallas_call(
        _ssm_params_kernel,
        grid=(depth, pairs),
        in_specs=[row_spec] * 3 + [col_spec] * 3 + [bt_spec] * 2 + [ct_spec] * 2,
        out_specs=[
            pl.BlockSpec((None, None, 2 * pw, 4 * pw), lambda l, j: (l, j, 0, 0)),
            pl.BlockSpec((None, None, 8 * n, 2 * pw), lambda l, j: (l, j, 0, 0)),
            pl.BlockSpec((None, 4, 2 * n), lambda l, j: (l, 0, j)),
        ],
        out_shape=[
            jax.ShapeDtypeStruct((depth, pairs, 2 * pw, 4 * pw), BF16),
            jax.ShapeDtypeStruct((depth, pairs, 8 * n, 2 * pw), BF16),
            jax.ShapeDtypeStruct((depth, 4, groups * n), F32),
        ],
        compiler_params=_params(("parallel", "parallel")),
        name="ssm_parameters",
    )(*rows, *cols, bt(b_re), bt(b_im), ct(c_re), ct(c_im))


def _chunk_rows(u_ref):
    return jnp.concatenate([u_ref[t] for t in range(u_ref.shape[0])], axis=-1)


def _scan_chunk_states(s_ref, a_ref, h_ref, *, batch, lat_tiles, ctx_tiles):
    w = s_ref.shape[-1]
    per_tile = SUBLANES // batch
    row_grp = lax.broadcasted_iota(jnp.int32, (SUBLANES, w), 0) // batch
    zero = jnp.zeros((SUBLANES, w), F32)

    def make_tile(k, order, shift):
        ar = a_ref[k:k + 1, :]
        ai = a_ref[k + 1:k + 2, :]

        def tile(ti, carry):
            cur_re, cur_im = carry
            rows = pl.ds(pl.multiple_of(ti * SUBLANES, SUBLANES), SUBLANES)
            s_re = s_ref[k, rows, :]
            s_im = s_ref[k + 1, rows, :]
            h_re, h_im = zero, zero
            for pos in order:
                sel = row_grp == pos
                h_re = jnp.where(sel, cur_re, h_re)
                h_im = jnp.where(sel, cur_im, h_im)
                nxt_re = ar * cur_re - ai * cur_im + s_re
                nxt_im = ar * cur_im + ai * cur_re + s_im
                if batch == SUBLANES:
                    cur_re, cur_im = nxt_re, nxt_im
                else:
                    cur_re = pltpu.roll(nxt_re, shift, 0)
                    cur_im = pltpu.roll(nxt_im, shift, 0)
            h_ref[k, rows, :] = h_re
            h_ref[k + 1, rows, :] = h_im
            return cur_re, cur_im

        return tile

    fwd = make_tile(0, range(per_tile), batch)
    bwd = make_tile(2, range(per_tile - 1, -1, -1), SUBLANES - batch)

    def both(first_tile, n_tiles):
        def step(i, carry):
            return fwd(first_tile + i, carry[0]), bwd(first_tile + n_tiles - 1 - i, carry[1])
        return step

    carry = lax.fori_loop(0, ctx_tiles, both(lat_tiles, ctx_tiles), ((zero, zero), (zero, zero)))
    lax.fori_loop(0, lat_tiles, both(0, lat_tiles), carry)


def _ssm_core_kernel(u_ref, r1_ref, v_ref, a_ref, d_ref, y_ref, s_scr, h_scr, *, batch, lat_tiles, ctx_tiles):
    n_t = u_ref.shape[0]
    kw = n_t * LANES
    u = _chunk_rows(u_ref)
    r = jnp.dot(u, r1_ref[...], preferred_element_type=F32)
    for t in range(n_t):
        y_ref[t] = r[:, LANES * t:LANES * (t + 1)]
    for k in range(4):
        s_scr[k] = r[:, kw + LANES * k:kw + LANES * (k + 1)]
    _scan_chunk_states(s_scr, a_ref, h_scr, batch=batch, lat_tiles=lat_tiles, ctx_tiles=ctx_tiles)
    h = jnp.concatenate([h_scr[k] for k in range(4)], axis=-1).astype(BF16)
    y = jnp.dot(h, v_ref[...], preferred_element_type=F32) + d_ref[...] * u.astype(F32)
    for t in range(n_t):
        y_ref[t] += y[:, LANES * t:LANES * (t + 1)]


def _ssm_core(u, r1, v, a16, dsk, l, dims):
    pairs, n_t, nc, _ = u.shape
    kw = n_t * LANES
    batch = dims["batch"]
    lat_tiles = dims["seq"] // SSM_CHUNK * batch // SUBLANES
    ctx_tiles = dims["ctx"] // SSM_CHUNK * batch // SUBLANES
    return pl.pallas_call(
        functools.partial(_ssm_core_kernel, batch=batch, lat_tiles=lat_tiles, ctx_tiles=ctx_tiles),
        grid=(pairs,),
        in_specs=[
            pl.BlockSpec((None, n_t, nc, LANES), lambda j: (j, 0, 0, 0)),
            pl.BlockSpec((None, None, kw, 2 * kw), lambda j: (l, j, 0, 0)),
            pl.BlockSpec((None, None, 4 * LANES, kw), lambda j: (l, j, 0, 0)),
            pl.BlockSpec((None, 4, LANES), lambda j: (l, 0, j)),
            pl.BlockSpec((None, None, 1, kw), lambda j: (l, j, 0, 0)),
        ],
        out_specs=pl.BlockSpec((None, n_t, nc, LANES), lambda j: (j, 0, 0, 0)),
        out_shape=jax.ShapeDtypeStruct((pairs, n_t, nc, LANES), F32),
        scratch_shapes=[pltpu.VMEM((4, nc, LANES), F32), pltpu.VMEM((4, nc, LANES), F32)],
        compiler_params=_params(("parallel",)),
        name="ssm_core",
    )(u, r1, v, a16, dsk)


GROUPS_PER_TILE = LANES // SSM_GROUP


def _lane_slot(rows):
    return lax.broadcasted_iota(jnp.int32, (rows, LANES), 1) // SSM_GROUP


def _block_transposes(sets, slot):
    sets = [list(t) for t in sets]
    d = GROUPS_PER_TILE // 2
    while d:
        low = (slot & d) == 0
        for t in sets:
            for a in range(GROUPS_PER_TILE):
                if not a & d:
                    lo, hi = t[a], t[a + d]
                    t[a] = jnp.where(low, lo, pltpu.roll(hi, d * SSM_GROUP, 1))
                    t[a + d] = jnp.where(low, pltpu.roll(lo, LANES - d * SSM_GROUP, 1), hi)
        d //= 2
    return sets


def _chunk_tiling(dims):
    ncl, ncc = dims["seq"] // SSM_CHUNK, dims["ctx"] // SSM_CHUNK
    ct = _pick([ncl, ncc], [16, 8, 4, 2, 1])
    nlt, nct = ncl // ct, ncc // ct

    def token_block(k, b):
        return jnp.where(k < nlt, b * nlt + k, dims["batch"] * nlt + b * nct + (k - nlt))

    return ct, nlt + nct, token_block


def _ssm_pack_kernel(*refs, batch, ct):
    x_refs, o_ref, scr = refs[:batch], refs[batch], refs[batch + 1]
    slot = _lane_slot(ct)
    halves = SSM_CHUNK // GROUPS_PER_TILE
    for b in range(batch):
        where = [(j, half) for j in range(x_refs[b].shape[0]) for half in range(halves)]
        sets = [[x_refs[b][j, pl.ds(GROUPS_PER_TILE * half + k, ct, stride=SSM_CHUNK), :]
                 for k in range(GROUPS_PER_TILE)] for j, half in where]
        for (j, half), outs in zip(where, _block_transposes(sets, slot)):
            for gg, out in enumerate(outs):
                g = GROUPS_PER_TILE * j + gg
                scr[g // 2, (g % 2) * halves + half, pl.ds(b, ct, stride=batch), :] = out
    o_ref[...] = scr[...].astype(o_ref.dtype)


def _ssm_pack(u32, dims):
    batch, pairs = dims["batch"], dims["pairs"]
    ct, n_tiles, token_block = _chunk_tiling(dims)
    n_in, nt, _ = u32.shape
    n_t = 2 * SSM_CHUNK * SSM_GROUP // LANES
    return pl.pallas_call(
        functools.partial(_ssm_pack_kernel, batch=batch, ct=ct),
        grid=(n_tiles,),
        in_specs=[pl.BlockSpec((n_in, ct * SSM_CHUNK, LANES),
                               functools.partial(lambda k, b: (0, token_block(k, b), 0), b=b))
                  for b in range(batch)],
        out_specs=pl.BlockSpec((pairs, n_t, ct * batch, LANES), lambda k: (0, 0, k, 0)),
        out_shape=jax.ShapeDtypeStruct((pairs, n_t, nt // SSM_CHUNK, LANES), BF16),
        scratch_shapes=[pltpu.VMEM((pairs, n_t, ct * batch, LANES), F32)],
        compiler_params=_params(("parallel",)),
        name="ssm_pack",
    )(*([u32] * batch))


def _ssm_unpack_kernel(y_ref, o_ref, *, batch, ct):
    slot = _lane_slot(ct)
    halves = SSM_CHUNK // GROUPS_PER_TILE
    tokens = ct * SSM_CHUNK
    where = [(j, half) for j in range(o_ref.shape[0]) for half in range(halves)]
    for b in range(batch):
        sets = []
        for j, half in where:
            groups = [GROUPS_PER_TILE * j + gg for gg in range(GROUPS_PER_TILE)]
            sets.append([y_ref[g // 2, (g % 2) * halves + half, pl.ds(b, ct, stride=batch), :] for g in groups])
        for (j, half), outs in zip(where, _block_transposes(sets, slot)):
            for k, out in enumerate(outs):
                o_ref[j, pl.ds(tokens * b + GROUPS_PER_TILE * half + k, ct, stride=SSM_CHUNK), :] = out


def _unpacked_block(dims, tm):
    ct, _, _ = _chunk_tiling(dims)
    assert tm == ct * SSM_CHUNK
    batch = dims["batch"]
    nlt, nct = dims["seq"] // tm, dims["ctx"] // tm

    def block(i):
        c = i - batch * nlt
        return jnp.where(i < batch * nlt, (i % nlt) * batch + i // nlt, (nlt + c % nct) * batch + c // nct)

    return block


def _ssm_unpack(y, dims):
    batch = dims["batch"]
    pairs, n_t, nc, _ = y.shape
    ct, n_tiles, _ = _chunk_tiling(dims)
    n_out = dims["ssm_w"] // LANES
    return pl.pallas_call(
        functools.partial(_ssm_unpack_kernel, batch=batch, ct=ct),
        grid=(n_tiles,),
        in_specs=[pl.BlockSpec((pairs, n_t, ct * batch, LANES), lambda k: (0, 0, k, 0))],
        out_specs=pl.BlockSpec((n_out, batch * ct * SSM_CHUNK, LANES), lambda k: (0, k, 0)),
        out_shape=jax.ShapeDtypeStruct((n_out, nc * SSM_CHUNK, LANES), F32),
        compiler_params=_params(("parallel",)),
        name="ssm_unpack",
    )(y)


def _merge_kernel(pb_ref, pc_ref, pv_ref, hpc_ref, hpv_ref, hnc_ref, hnv_ref, cw_ref, ys_ref, at_ref,
                  g_ref, wc_ref, wg_ref, wa_ref, wo_ref, x_ref, gate_ref, gpost_ref, gn_ref, shn_ref, scn_ref,
                  o_ref, h_ref, m_scr, *, tm, tn, seq, ctx, n_lat):
    d = o_ref.shape[1]
    r0 = pl.program_id(0) * tm
    period = jnp.where(r0 >= n_lat, ctx, seq)
    starts = r0 % period == 0
    ends = (r0 + tm) % period == 0
    w = pc_ref[...].astype(F32) * pv_ref[...].astype(F32)
    prev = (hpc_ref[...].astype(F32) * hpv_ref[...].astype(F32))[HALO_ROWS - 1:HALO_ROWS]
    nxt = (hnc_ref[...].astype(F32) * hnv_ref[...].astype(F32))[0:1]
    prev = jnp.where(starts, 0.0, prev)
    nxt = jnp.where(ends, 0.0, nxt)
    row = lax.broadcasted_iota(jnp.int32, w.shape, 0)
    up = jnp.where(row == 0, prev, pltpu.roll(w, 1, 0))
    dn = jnp.where(row == tm - 1, nxt, pltpu.roll(w, tm - 1, 0))
    cw = cw_ref[...]
    cf = (pb_ref[...].astype(F32) * (cw[0:1] * up + cw[1:2] * w + cw[2:3] * dn)).astype(BF16)
    yg = jax.nn.gelu(_chunk_rows(ys_ref)).astype(BF16)
    at = at_ref[...]
    for c in range(d // tn):
        cols = slice(tn * c, tn * (c + 1))
        gate = lambda k: jax.nn.sigmoid(g_ref[:, k * d + tn * c:k * d + tn * (c + 1)].astype(F32))
        ya = jnp.dot(cf, wc_ref[:, cols], preferred_element_type=F32)
        yc = jnp.dot(at, wa_ref[:, cols], preferred_element_type=F32)
        ys = jnp.dot(yg, wg_ref[:, cols], preferred_element_type=F32) * jax.nn.sigmoid(
            jnp.dot(yg, wg_ref[:, d + tn * c:d + tn * (c + 1)], preferred_element_type=F32))
        m_scr[:, cols] = (gate(0) * ya + gate(1) * ys + gate(2) * yc).astype(m_scr.dtype)
    half = tm // 2
    for rows in (slice(0, half), slice(half, 2 * half)):
        r = jnp.dot(m_scr[rows, :], wo_ref[...], preferred_element_type=F32)
        x = x_ref[rows, :] + gate_ref[...] * _rms(r, gpost_ref[...])
        o_ref[rows, :] = x
        h_ref[rows, :] = _norm_mod(x, gn_ref[...], shn_ref[...], scn_ref[...])


def _merge(main, gates, ys, attn, conv_w, w_conv_out, w_glu, w_attn_out, w_out, xt, modr, g_post, g_pre_mlp,
           l, dims, rows):
    d = dims["d"]
    cw_, sw, aw = dims["conv_w"], dims["ssm_w"], dims["attn_w"]
    tm = _pick([dims["seq"], dims["ctx"]], [512, 256, 128, 64, 32])
    tn = _pick([d], [512, 256, 128])
    n_halo = main.shape[0] // HALO_ROWS
    per = tm // HALO_ROWS
    prev_blk = lambda i: jnp.maximum(i * per - 1, 0)
    next_blk = lambda i: jnp.minimum((i + 1) * per, n_halo - 1)
    grp = _row_group(dims, tm)
    ys_block = _unpacked_block(dims, tm)
    mod = lambda k: pl.BlockSpec((None, None, 1, d), lambda i: (l, grp(i), 0, k))
    vec = pl.BlockSpec((None, 1, d), lambda i: (l, 0, 0))
    row_tile = pl.BlockSpec((tm, d), lambda i: (i, 0))
    resident = lambda shape: pl.BlockSpec((None,) + shape, lambda i: (l, 0, 0), pipeline_mode=pl.Buffered(1))
    return pl.pallas_call(
        functools.partial(_merge_kernel, tm=tm, tn=tn, seq=dims["seq"], ctx=dims["ctx"], n_lat=dims["n_lat"]),
        grid=(rows // tm,),
        in_specs=[
            pl.BlockSpec((tm, cw_), lambda i: (i, 0)),
            pl.BlockSpec((tm, cw_), lambda i: (i, 1)),
            pl.BlockSpec((tm, cw_), lambda i: (i, 2)),
            pl.BlockSpec((HALO_ROWS, cw_), lambda i: (prev_blk(i), 1)),
            pl.BlockSpec((HALO_ROWS, cw_), lambda i: (prev_blk(i), 2)),
            pl.BlockSpec((HALO_ROWS, cw_), lambda i: (next_blk(i), 1)),
            pl.BlockSpec((HALO_ROWS, cw_), lambda i: (next_blk(i), 2)),
            resident((CONV_K, cw_)),
            pl.BlockSpec((sw // LANES, tm, LANES), lambda i: (0, ys_block(i), 0)),
            pl.BlockSpec((tm, aw), lambda i: (i, 0)),
            pl.BlockSpec((tm, 3 * d), lambda i: (i, 0)),
            resident((cw_, d)),
            resident((sw, 2 * d)),
            resident((aw, d)),
            resident((d, d)),
            row_tile, mod(2), vec, vec, mod(3), mod(4),
        ],
        out_specs=[row_tile, row_tile],
        out_shape=[jax.ShapeDtypeStruct(xt.shape, F32), jax.ShapeDtypeStruct((rows, d), BF16)],
        scratch_shapes=[pltpu.VMEM((tm, d), BF16)],
        input_output_aliases={15: 0},
        compiler_params=_params(("parallel",)),
        name="mixer_output",
    )(main, main, main, main, main, main, main, conv_w, ys, attn, gates, w_conv_out, w_glu, w_attn_out,
      w_out, xt, modr, g_post, g_pre_mlp, modr, modr)


def _mlp_kernel(x_ref, h_ref, wu_ref, wd_ref, gate_ref, gpost_ref, *rest, with_next):
    j = pl.program_id(1)
    if with_next:
        gn_ref, shn_ref, scn_ref, o_ref, hn_ref, acc_scr = rest
    else:
        o_ref, acc_scr = rest

    last = pl.num_programs(1) - 1

    def partial_out(rows):
        u = jnp.maximum(jnp.dot(h_ref[rows, :], wu_ref[...], preferred_element_type=F32), 0.0)
        return jnp.dot((u * u).astype(BF16), wd_ref[...], preferred_element_type=F32)

    every = slice(None)

    @pl.when(j == 0)
    def _():
        acc_scr[...] = partial_out(every)

    @pl.when(jnp.logical_and(j > 0, j < last))
    def _():
        acc_scr[...] += partial_out(every)

    @pl.when(j == last)
    def _():
        part = acc_scr.shape[0] // MLP_TAIL_PARTS
        for k in range(MLP_TAIL_PARTS):
            rows = slice(part * k, part * (k + 1))
            r = acc_scr[rows, :] + partial_out(rows)
            x = x_ref[rows, :] + gate_ref[...] * _rms(r, gpost_ref[...])
            o_ref[rows, :] = x
            if with_next:
                hn_ref[rows, :] = _norm_mod(x, gn_ref[...], shn_ref[...], scn_ref[...])


def _mlp(xt, h, g_post, modr, w_up, w_down, l, dims, rows, g_next=None):
    d = xt.shape[1]
    d_ff = w_up.shape[-1]
    tm = _pick([dims["seq"], dims["n_ctx"]], [512, 256, 128])
    tf = _pick([d_ff], [1024, 512, 256, 128])
    grp = _row_group(dims, tm)
    mod = lambda layer, k: pl.BlockSpec((None, None, 1, d), lambda i, j: (layer, grp(i), 0, k))
    vec = lambda layer: pl.BlockSpec((None, 1, d), lambda i, j: (layer, 0, 0))
    row_tile = pl.BlockSpec((tm, d), lambda i, j: (i, 0))
    in_specs = [
        row_tile, row_tile,
        pl.BlockSpec((None, d, tf), lambda i, j: (l, 0, j)),
        pl.BlockSpec((None, tf, d), lambda i, j: (l, j, 0)),
        mod(l, 5), vec(l),
    ]
    args = [xt, h, w_up, w_down, modr, g_post]
    out_specs, out_shape = [row_tile], [jax.ShapeDtypeStruct((rows, d), F32)]
    if g_next is not None:
        assert rows == xt.shape[0]
        in_specs += [vec(l + 1), mod(l + 1, 0), mod(l + 1, 1)]
        args += [g_next, modr, modr]
        out_specs.append(row_tile)
        out_shape.append(jax.ShapeDtypeStruct((rows, d), BF16))
    return pl.pallas_call(
        functools.partial(_mlp_kernel, with_next=g_next is not None),
        grid=(rows // tm, d_ff // tf),
        in_specs=in_specs,
        out_specs=out_specs,
        out_shape=out_shape,
        scratch_shapes=[pltpu.VMEM((tm, d), F32)],
        input_output_aliases={0: 0} if g_next is not None else {},
        compiler_params=_params(("parallel", "arbitrary")),
        name="mlp",
    )(*args)


def kernel(x, c, ctx, c_ctx, w_mod, b_mod, g_pre_mix, g_post_mix, g_pre_mlp, g_post_mlp, w_in, conv_w,
           w_conv_out, ssm_lam_re, ssm_lam_im, ssm_log_dt, ssm_b_re, ssm_b_im, ssm_c_re, ssm_c_im, ssm_d,
           w_glu, q_gain, k_gain, w_attn_out, w_out, w_up, w_down):
    batch, seq, d = x.shape
    ctx_len = ctx.shape[1]
    depth = w_in.shape[0]
    in_w = w_in.shape[-1]
    conv_width = conv_w.shape[-1]
    ssm_w = ssm_d.shape[-1]
    attn_w = w_attn_out.shape[1]
    gate_w = 3 * d
    kv_w = (in_w - 3 * conv_width - ssm_w - attn_w - gate_w) // 2
    groups = ssm_lam_re.shape[2]
    assert batch + 1 <= MOD_ROWS and SUBLANES % batch == 0
    assert kv_w == N_KV_HEADS * HEAD_DIM and ssm_lam_re.shape[3] == SSM_STATE
    assert ssm_w == groups * SSM_GROUP and groups % 2 == 0
    assert seq % GRID_W == 0 and (batch * seq) % ctx_len == 0 and ctx_len % LANES == 0
    a_w = 3 * conv_width + ssm_w
    dims = dict(
        batch=batch, seq=seq, ctx=ctx_len, d=d, n_lat=batch * seq, n_ctx=batch * ctx_len,
        conv_w=conv_width, ssm_w=ssm_w, attn_w=attn_w, kv_w=kv_w, gate_w=gate_w, pairs=groups // 2,
        o_q=gate_w, o_k=gate_w + attn_w, o_v=gate_w + attn_w + kv_w,
    )
    n_lat, nt = dims["n_lat"], dims["n_lat"] + dims["n_ctx"]

    cc = jnp.concatenate([c, c_ctx[None, :], jnp.zeros((MOD_ROWS - batch - 1, d), F32)], axis=0)
    mod = _modulation(cc, w_mod, b_mod)
    modr = mod.reshape(depth, MOD_ROWS, 1, N_MOD * d)

    row_vec = lambda a: a.reshape(depth, 1, a.shape[-1])
    g_pre_mix, g_post_mix, g_pre_mlp, g_post_mlp = map(row_vec, (g_pre_mix, g_post_mix, g_pre_mlp, g_post_mlp))
    perm = _head_permutation()
    q_gain, k_gain = row_vec(q_gain[:, perm]), row_vec(k_gain[:, perm])
    w_in_a = w_in[:, :, :a_w].astype(BF16)
    w_qk = w_in[:, :, a_w:a_w + attn_w + kv_w].reshape(depth, d, -1, HEAD_DIM)[..., perm]
    w_in_b = jnp.concatenate([w_in[:, :, in_w - gate_w:], w_qk.reshape(depth, d, attn_w + kv_w),
                              w_in[:, :, a_w + attn_w + kv_w:in_w - gate_w]], axis=-1).astype(BF16)
    w_conv_out, w_glu, w_attn_out, w_out, w_up, w_down = (
        w.astype(BF16) for w in (w_conv_out, w_glu, w_attn_out, w_out, w_up, w_down))
    tn_a = _pick([a_w], [1280, 768, 512, 256, 128])
    tn_b = _pick([w_in_b.shape[-1]], [1536, 768, 512, 256, 128])

    cos_t, sin_t = _rope_tables(seq, _qk_tile(dims))
    r1, v_mat, a16 = _ssm_parameters(ssm_lam_re, ssm_lam_im, ssm_log_dt, ssm_b_re, ssm_b_im, ssm_c_re, ssm_c_im)
    dsk = jnp.broadcast_to(
        ssm_d.astype(F32).reshape(depth, groups // 2, 2, 1, SSM_GROUP),
        (depth, groups // 2, 2, SSM_CHUNK, SSM_GROUP)).reshape(depth, groups // 2, 1, 2 * SSM_CHUNK * SSM_GROUP)

    xt, h = _prenorm(x.reshape(n_lat, d), ctx.reshape(dims["n_ctx"], d), g_pre_mix, modr, dims)
    for l in range(depth):
        need_ctx = l < depth - 1
        rows = nt if need_ctx else n_lat
        proj_a, u32 = _projection(h, w_in_a, l, dims, tn_a, f32_cols=(3 * conv_width, ssm_w))
        (proj_b,) = _projection(h, w_in_b, l, dims, tn_b)
        u = _ssm_pack(u32, dims)
        ys = _ssm_unpack(_ssm_core(u, r1, v_mat, a16, dsk, l, dims), dims)
        q, k = _qk_prepare(proj_b, cos_t, sin_t, q_gain, k_gain, l, dims)
        attn = _attention(q, k, proj_b, None, dims)
        if need_ctx:
            attn = _attention(q, k, proj_b, attn, dims)
        xt, h = _merge(proj_a, proj_b, ys, attn, conv_w, w_conv_out, w_glu, w_attn_out, w_out, xt, modr,
                       g_post_mix, g_pre_mlp, l, dims, rows)
        if need_ctx:
            xt, h = _mlp(xt, h, g_post_mlp, modr, w_up, w_down, l, dims, rows, g_next=g_pre_mix)
        else:
            (xt,) = _mlp(xt, h, g_post_mlp, modr, w_up, w_down, l, dims, rows)
    return xt.reshape(batch, seq, d)
```
